```python
import math
import jax
import jax.numpy as jnp
from jax import lax
import numpy as np

D_MODEL = 1024
BATCH = 8
SEQ = 2048
DEPTH = 4

FFN_DIM = 2816
BRANCH_WIDTH = 512
N_BRANCH = 4
SGU_WIDTH = BRANCH_WIDTH
SGU_GROUPS = 4
SGU_CHUNK = 128
SSM_HEADS = 8
SSM_HEAD_DIM = 64
SSM_INNER = SSM_HEADS * SSM_HEAD_DIM
SSM_GROUPS = 2
SSM_STATE = 128
SSM_CONV = 4
SSM_CHUNK = 128
SSM_CONV_DIM = SSM_INNER + 2 * SSM_GROUPS * SSM_STATE
NSA_HEADS = 8
NSA_KV_GROUPS = 2
NSA_HEAD_DIM = 64
NSA_WIDTH = NSA_HEADS * NSA_HEAD_DIM
CMP_BLOCK = 32
CMP_STRIDE = 16
CMP_HIDDEN = 128
SEL_BLOCK = 64
SEL_TOP_N = 8
WINDOW = 256
Q_BLOCK = 128
CONV_WIDTH = BRANCH_WIDTH
CONV_KERNEL = 31
EPS = 1e-6
NEG = -1e30
FORCE = 1e9
NSA_KV_WIDTH = NSA_KV_GROUPS * NSA_HEAD_DIM
SPLIT_SIZES = (2 * SGU_WIDTH, SSM_INNER, SSM_CONV_DIM, SSM_HEADS, NSA_WIDTH,
               NSA_KV_WIDTH, NSA_KV_WIDTH, NSA_KV_WIDTH, NSA_KV_WIDTH, NSA_KV_WIDTH, NSA_KV_WIDTH,
               3 * NSA_HEADS, 2 * CONV_WIDTH, N_BRANCH * D_MODEL)
IN_PROJ_DIM = 2 * SGU_WIDTH + SSM_INNER + SSM_CONV_DIM + SSM_HEADS + NSA_WIDTH + 6 * NSA_KV_WIDTH + 3 * NSA_HEADS + 2 * CONV_WIDTH + N_BRANCH * D_MODEL

kernel_name = "hybrid_sgu_ssd_nsa_conformer_macaron"


def rms_norm(x, g):
    xf = x.astype(jnp.float32)
    y = xf * lax.rsqrt(jnp.mean(xf * xf, axis=-1, keepdims=True) + EPS)
    return (y * g.astype(jnp.float32)).astype(x.dtype)


def swiglu(x, w_in, w_out):
    gate, up = jnp.split(x @ w_in, 2, axis=-1)
    return (jax.nn.silu(gate) * up) @ w_out


def causal_depthwise_conv(x, w, b):
    k, c = w.shape
    y = lax.conv_general_dilated(x, w[:, None, :].astype(x.dtype), window_strides=(1,),
                                 padding=[(k - 1, 0)], dimension_numbers=("NWC", "WIO", "NWC"),
                                 feature_group_count=c)
    return y + b.astype(x.dtype)


def alibi_slopes(n):
    return np.array([2.0 ** (-8.0 * (i + 1) / n) for i in range(n)], dtype=np.float32)


def segsum(a):
    t = a.shape[-1]
    ar = jnp.broadcast_to(a[..., :, None], a.shape + (t,))
    cs = jnp.cumsum(jnp.where(jnp.tril(jnp.ones((t, t), bool), -1), ar, 0.0), axis=-2)
    return jnp.where(jnp.tril(jnp.ones((t, t), bool)), cs, -jnp.inf)


def ssd_scan(x, a, b, c):
    bsz, s, h, p = x.shape
    n = b.shape[-1]
    nc = s // SSM_CHUNK
    x = x.reshape(bsz, nc, SSM_CHUNK, h, p)
    b = b.reshape(bsz, nc, SSM_CHUNK, h, n)
    c = c.reshape(bsz, nc, SSM_CHUNK, h, n)
    a = a.reshape(bsz, nc, SSM_CHUNK, h).transpose(0, 3, 1, 2)
    a_cs = jnp.cumsum(a, axis=-1)
    scores = jnp.einsum("bclhn,bcshn->bhcls", c, b) * jnp.exp(segsum(a))
    y_diag = jnp.einsum("bhcls,bcshp->bclhp", scores, x)
    decay_states = jnp.exp(a_cs[..., -1:] - a_cs)
    states = jnp.einsum("bclhn,bhcl,bclhp->bchpn", b, decay_states, x)
    states = jnp.concatenate([jnp.zeros_like(states[:, :1]), states], axis=1)
    decay_chunk = jnp.exp(segsum(jnp.pad(a_cs[..., -1], ((0, 0), (0, 0), (1, 0)))))
    states = jnp.einsum("bhzc,bchpn->bzhpn", decay_chunk, states)[:, :-1]
    y_off = jnp.einsum("bclhn,bchpn,bhcl->bclhp", c, states, jnp.exp(a_cs))
    return (y_diag + y_off).reshape(bsz, s, h, p)


def sgu_mixer(uv, v_norm, w_s, b_s):
    bsz, s, _ = uv.shape
    u, v = jnp.split(jax.nn.gelu(uv), 2, axis=-1)
    v = rms_norm(v, v_norm).reshape(bsz, s // SGU_CHUNK, SGU_CHUNK, SGU_GROUPS, SGU_WIDTH // SGU_GROUPS)
    w = w_s * jnp.tril(jnp.ones((SGU_CHUNK, SGU_CHUNK), w_s.dtype))
    mixed = jnp.einsum("gts,bcsgd->bctgd", w, v) + b_s.T[None, None, :, :, None]
    return u * mixed.reshape(bsz, s, SGU_WIDTH)


def mamba2_mixer(z, xbc, dt, conv_w, conv_b, dt_bias, a_log, d_skip, norm_g):
    bsz, s, _ = z.shape
    f32 = jnp.float32
    xbc = jax.nn.silu(causal_depthwise_conv(xbc, conv_w, conv_b))
    xs, bm, cm = jnp.split(xbc, [SSM_INNER, SSM_INNER + SSM_GROUPS * SSM_STATE], axis=-1)
    rep = SSM_HEADS // SSM_GROUPS
    xs = xs.reshape(bsz, s, SSM_HEADS, SSM_HEAD_DIM).astype(f32)
    bm = jnp.repeat(bm.reshape(bsz, s, SSM_GROUPS, SSM_STATE).astype(f32), rep, axis=2)
    cm = jnp.repeat(cm.reshape(bsz, s, SSM_GROUPS, SSM_STATE).astype(f32), rep, axis=2)
    dt = jax.nn.softplus(dt.astype(f32) + dt_bias.astype(f32))
    a = -jnp.exp(a_log.astype(f32))
    y = ssd_scan(xs * dt[..., None], dt * a, bm, cm)
    y = (y + xs * d_skip.astype(f32)[:, None]).reshape(bsz, s, SSM_INNER)
    return rms_norm(y * jax.nn.silu(z.astype(f32)), norm_g).astype(z.dtype)


def nsa_mixer(q, kc, vc, ks, vs, kw, vw, gate, q_norm, k_norm,
              pe_k, w1_k, w2_k, pe_v, w1_v, w2_v):
    bsz, s, _ = q.shape
    G, R, E = NSA_KV_GROUPS, NSA_HEADS // NSA_KV_GROUPS, NSA_HEAD_DIM
    f32 = jnp.float32
    n_blk = s // Q_BLOCK
    n_cmp = (s - CMP_BLOCK) // CMP_STRIDE + 1
    n_sel = s // SEL_BLOCK
    top_n = min(SEL_TOP_N, n_sel)
    slopes = jnp.asarray(alibi_slopes(NSA_HEADS).reshape(G, R))

    def kv(t):
        return t.reshape(bsz, s, G, E)

    q = rms_norm(q.reshape(bsz, s, G, R, E), q_norm) * (E ** -0.5)
    ks = rms_norm(kv(ks), k_norm)
    kw = rms_norm(kv(kw), k_norm)

    cidx = np.arange(n_cmp)[:, None] * CMP_STRIDE + np.arange(CMP_BLOCK)[None, :]

    def compress(k, pe, w1, w2):
        blk = k[:, cidx] + pe[:, None, :]
        blk = blk.transpose(0, 1, 3, 2, 4).reshape(bsz, n_cmp, G, CMP_BLOCK * E)
        return jax.nn.gelu(blk @ w1) @ w2

    k_cmp = rms_norm(compress(kv(kc), pe_k, w1_k, w2_k), k_norm)
    v_cmp = compress(kv(vc), pe_v, w1_v, w2_v)
    c_start = np.arange(n_cmp) * CMP_STRIDE
    c_last = jnp.asarray(c_start + CMP_BLOCK - 1)
    c_mid = jnp.asarray((c_start + (CMP_BLOCK - 1) / 2.0).astype(np.float32))
    s_start = np.arange(n_sel) * SEL_BLOCK
    overlap = jnp.asarray(((c_start[:, None] <= s_start[None, :] + SEL_BLOCK - 1)
                           & (c_start[:, None] + CMP_BLOCK - 1 >= s_start[None, :])).astype(np.float32))

    ksb = ks.reshape(bsz, n_sel, SEL_BLOCK, G, E).transpose(0, 3, 1, 2, 4)
    vsb = kv(vs).reshape(bsz, n_sel, SEL_BLOCK, G, E).transpose(0, 3, 1, 2, 4)
    kw_pad = jnp.pad(kw, ((0, 0), (WINDOW, 0), (0, 0), (0, 0)))
    vw_pad = jnp.pad(kv(vw), ((0, 0), (WINDOW, 0), (0, 0), (0, 0)))
    bi = jnp.arange(bsz)[:, None, None, None]
    gi = jnp.arange(G)[None, :, None, None]
    sel_ids = jnp.arange(n_sel)

    def block(args):
        j, qb, gb = args
        t = j * Q_BLOCK + jnp.arange(Q_BLOCK)
        tf = t.astype(f32)
        mask_c = c_last[None, :] <= t[:, None]
        s_c = jnp.einsum("bqgrd,bcgd->bgrqc", qb, k_cmp).astype(f32) \
            - slopes[:, :, None, None] * (tf[:, None] - c_mid[None, :])
        p_c = jax.nn.softmax(jnp.where(mask_c, s_c, NEG), axis=-1) * mask_c
        o_c = jnp.einsum("bgrqc,bcgd->bqgrd", p_c.astype(v_cmp.dtype), v_cmp)
        imp = jnp.einsum("bgrqc,cn->bgqn", p_c, overlap)
        cur = (t // SEL_BLOCK)[:, None]
        forced = (sel_ids == 0) | (sel_ids == cur) | (sel_ids == cur - 1)
        valid = sel_ids * SEL_BLOCK <= t[:, None]
        score = jnp.where(forced, FORCE, jnp.where(valid, imp, NEG))
        _, idx = lax.top_k(score, top_n)
        kb = ksb[bi, gi, idx]
        vb = vsb[bi, gi, idx]
        pos = idx[..., None] * SEL_BLOCK + jnp.arange(SEL_BLOCK)
        rel_s = t[None, None, :, None, None] - pos
        s_s = jnp.einsum("bqgrd,bgqnld->bgrqnl", qb, kb).astype(f32) \
            - slopes[:, :, None, None, None] * rel_s[:, :, None].astype(f32)
        s_s = jnp.where((rel_s >= 0)[:, :, None], s_s, NEG).reshape(bsz, G, R, Q_BLOCK, top_n * SEL_BLOCK)
        p_s = jax.nn.softmax(s_s, axis=-1).reshape(bsz, G, R, Q_BLOCK, top_n, SEL_BLOCK)
        o_s = jnp.einsum("bgrqnl,bgqnld->bqgrd", p_s.astype(vb.dtype), vb)
        kwin = lax.dynamic_slice_in_dim(kw_pad, j * Q_BLOCK, Q_BLOCK + WINDOW, axis=1)
        vwin = lax.dynamic_slice_in_dim(vw_pad, j * Q_BLOCK, Q_BLOCK + WINDOW, axis=1)
        kpos = j * Q_BLOCK - WINDOW + jnp.arange(Q_BLOCK + WINDOW)
        rel_w = t[:, None] - kpos[None, :]
        mask_w = (rel_w >= 0) & (rel_w < WINDOW) & (kpos[None, :] >= 0)
        s_w = jnp.einsum("bqgrd,bkgd->bgrqk", qb, kwin).astype(f32) \
            - slopes[:, :, None, None] * rel_w.astype(f32)
        p_w = jax.nn.softmax(jnp.where(mask_w, s_w, NEG), axis=-1)
        o_w = jnp.einsum("bgrqk,bkgd->bqgrd", p_w.astype(vwin.dtype), vwin)
        return gb[..., 0:1] * o_c + gb[..., 1:2] * o_s + gb[..., 2:3] * o_w

    qbs = q.reshape(bsz, n_blk, Q_BLOCK, G, R, E).swapaxes(0, 1)
    gbs = jax.nn.sigmoid(gate).reshape(bsz, n_blk, Q_BLOCK, G, R, 3).swapaxes(0, 1)
    out = lax.map(block, (jnp.arange(n_blk), qbs, gbs))
    return out.swapaxes(0, 1).reshape(bsz, s, NSA_WIDTH)


def conv_module(ab, dw_w, dw_b, norm_g):
    a, b = jnp.split(ab, 2, axis=-1)
    y = causal_depthwise_conv(a * jax.nn.sigmoid(b), dw_w, dw_b)
    return jax.nn.silu(rms_norm(y, norm_g))


def setup_inputs(seed: int = 0) -> dict:
    key = jax.random.key(seed)
    keys = jax.random.split(key, 48)
    count = [0]
    L = DEPTH

    def nxt():
        k = keys[count[0]]
        count[0] += 1
        return k

    def nrm(shape, scale):
        return jax.random.normal(nxt(), shape, jnp.float32) * scale

    def gain(n):
        return 1.0 + nrm((L, n), 0.02)

    dt0 = jnp.exp(jax.random.uniform(nxt(), (L, SSM_HEADS), jnp.float32, math.log(1e-3), math.log(1e-1)))
    cmp_in = CMP_BLOCK * NSA_HEAD_DIM
    return {
        "x": nrm((BATCH, SEQ, D_MODEL), 1.0),
        "ffn1_norm": gain(D_MODEL),
        "ffn1_w_in": nrm((L, D_MODEL, 2 * FFN_DIM), D_MODEL ** -0.5),
        "ffn1_w_out": nrm((L, FFN_DIM, D_MODEL), FFN_DIM ** -0.5),
        "mix_norm": gain(D_MODEL),
        "w_in": nrm((L, D_MODEL, IN_PROJ_DIM), D_MODEL ** -0.5),
        "sgu_v_norm": gain(SGU_WIDTH),
        "sgu_w": nrm((L, SGU_GROUPS, SGU_CHUNK, SGU_CHUNK), SGU_CHUNK ** -0.5),
        "sgu_b": 1.0 + nrm((L, SGU_GROUPS, SGU_CHUNK), 0.1),
        "ssm_conv_w": nrm((L, SSM_CONV, SSM_CONV_DIM), SSM_CONV ** -0.5),
        "ssm_conv_b": nrm((L, SSM_CONV_DIM), 0.01),
        "ssm_dt_bias": dt0 + jnp.log(-jnp.expm1(-dt0)),
        "ssm_a_log": jnp.log(jax.random.uniform(nxt(), (L, SSM_HEADS), jnp.float32, 1.0, 16.0)),
        "ssm_d": 1.0 + nrm((L, SSM_HEADS), 0.1),
        "ssm_norm": gain(SSM_INNER),
        "nsa_q_norm": gain(NSA_HEAD_DIM),
        "nsa_k_norm": gain(NSA_HEAD_DIM),
        "nsa_pe_k": nrm((L, CMP_BLOCK, NSA_HEAD_DIM), 0.5),
        "nsa_w1_k": nrm((L, cmp_in, CMP_HIDDEN), cmp_in ** -0.5),
        "nsa_w2_k": nrm((L, CMP_HIDDEN, NSA_HEAD_DIM), CMP_HIDDEN ** -0.5),
        "nsa_pe_v": nrm((L, CMP_BLOCK, NSA_HEAD_DIM), 0.5),
        "nsa_w1_v": nrm((L, cmp_in, CMP_HIDDEN), cmp_in ** -0.5),
        "nsa_w2_v": nrm((L, CMP_HIDDEN, NSA_HEAD_DIM), CMP_HIDDEN ** -0.5),
        "conv_dw_w": nrm((L, CONV_KERNEL, CONV_WIDTH), CONV_KERNEL ** -0.5),
        "conv_dw_b": nrm((L, CONV_WIDTH), 0.01),
        "conv_norm": gain(CONV_WIDTH),
        "w_branch": nrm((L, N_BRANCH, BRANCH_WIDTH, D_MODEL), BRANCH_WIDTH ** -0.5),
        "w_out": nrm((L, D_MODEL, D_MODEL), D_MODEL ** -0.5),
        "ffn2_norm": gain(D_MODEL),
        "ffn2_w_in": nrm((L, D_MODEL, 2 * FFN_DIM), D_MODEL ** -0.5),
        "ffn2_w_out": nrm((L, FFN_DIM, D_MODEL), FFN_DIM ** -0.5),
    }


def reference(x, ffn1_norm, ffn1_w_in, ffn1_w_out, mix_norm, w_in,
              sgu_v_norm, sgu_w, sgu_b,
              ssm_conv_w, ssm_conv_b, ssm_dt_bias, ssm_a_log, ssm_d, ssm_norm,
              nsa_q_norm, nsa_k_norm, nsa_pe_k, nsa_w1_k, nsa_w2_k, nsa_pe_v, nsa_w1_v, nsa_w2_v,
              conv_dw_w, conv_dw_b, conv_norm,
              w_branch, w_out, ffn2_norm, ffn2_w_in, ffn2_w_out):
    bsz, s, _ = x.shape
    split_at = [int(v) for v in np.cumsum(SPLIT_SIZES)[:-1]]
    for l in range(DEPTH):
        x = x + 0.5 * swiglu(rms_norm(x, ffn1_norm[l]), ffn1_w_in[l], ffn1_w_out[l])
        h = rms_norm(x, mix_norm[l])
        (a_uv, b_z, b_xbc, b_dt, c_q, c_kc, c_vc, c_ks, c_vs, c_kw, c_vw, c_gate,
         d_ab, merge_logits) = jnp.split(h @ w_in[l], split_at, axis=-1)
        y_a = sgu_mixer(a_uv, sgu_v_norm[l], sgu_w[l], sgu_b[l])
        y_b = mamba2_mixer(b_z, b_xbc, b_dt, ssm_conv_w[l], ssm_conv_b[l], ssm_dt_bias[l],
                           ssm_a_log[l], ssm_d[l], ssm_norm[l])
        y_c = nsa_mixer(c_q, c_kc, c_vc, c_ks, c_vs, c_kw, c_vw, c_gate, nsa_q_norm[l], nsa_k_norm[l],
                        nsa_pe_k[l], nsa_w1_k[l], nsa_w2_k[l], nsa_pe_v[l], nsa_w1_v[l], nsa_w2_v[l])
        y_d = conv_module(d_ab, conv_dw_w[l], conv_dw_b[l], conv_norm[l])
        ys = jnp.stack([y_a, y_b, y_c, y_d], axis=2)
        branch = jnp.einsum("bsie,ied->bsid", ys, w_branch[l])
        gates = jax.nn.sigmoid(merge_logits.reshape(bsz, s, N_BRANCH, D_MODEL))
        x = x + jnp.sum(gates * branch, axis=2) @ w_out[l]
        x = x + 0.5 * swiglu(rms_norm(x, ffn2_norm[l]), ffn2_w_in[l], ffn2_w_out[l])
    return x
```

```python
import functools
import math

import numpy as np
import jax
import jax.numpy as jnp
from jax import lax
from jax.experimental import pallas as pl
from jax.experimental.pallas import tpu as pltpu

F32 = jnp.float32
BF = jnp.bfloat16

D_MODEL = 1024
FFN_DIM = 2816
WIDTH = 512
N_BRANCH = 4
SGU_GROUPS = 4
CHUNK = 128
SSM_HEADS = 8
SSM_HEAD_DIM = 64
SSM_GROUPS = 2
SSM_STATE = 128
SSM_CONV = 4
SSM_CONV_DIM = WIDTH + 2 * SSM_GROUPS * SSM_STATE
SSM_REP = SSM_HEADS // SSM_GROUPS
NSA_HEADS = 8
NSA_GROUPS = 2
NSA_REP = NSA_HEADS // NSA_GROUPS
HEAD_DIM = 64
KV_WIDTH = NSA_GROUPS * HEAD_DIM
CMP_BLOCK = 32
CMP_STRIDE = 16
CMP_HIDDEN = 128
SEL_BLOCK = 64
SEL_TOP_N = 8
WINDOW = 256
CONV_KERNEL = 31
EPS = 1e-6
NEG = -1e30
FORCE = 1e9

OFF_UV = 0
OFF_Z = OFF_UV + 2 * WIDTH
OFF_XBC = OFF_Z + WIDTH
OFF_DT = OFF_XBC + SSM_CONV_DIM
OFF_Q = OFF_DT + SSM_HEADS
OFF_KV = OFF_Q + WIDTH
OFF_GATE = OFF_KV + 6 * KV_WIDTH
OFF_AB = OFF_GATE + 3 * NSA_HEADS
OFF_MERGE = OFF_AB + 2 * WIDTH
IN_PROJ_DIM = OFF_MERGE + N_BRANCH * D_MODEL

LANES = 128
VMEM_LIMIT_BYTES = 56 * 1024 * 1024

TOKEN_TILE = 256
CONV_HIST = 32
SSM_HIST = 8
SEL_KEYS = 256


def _params(n_axes):
    return pltpu.CompilerParams(dimension_semantics=("arbitrary",) * n_axes,
                                vmem_limit_bytes=VMEM_LIMIT_BYTES)


def _rms(x, g):
    return x * lax.rsqrt(jnp.mean(x * x, axis=-1, keepdims=True) + EPS) * g


def _silu(x):
    return x * jax.nn.sigmoid(x)


def _gelu(x):
    return 0.5 * x * (1.0 + jnp.tanh(math.sqrt(2.0 / math.pi) * (x + 0.044715 * (x * x * x))))


def _dot(a, b):
    return jnp.dot(a, b, preferred_element_type=F32)


def _dot_nt(a, b):
    return lax.dot_general(a, b, (((1,), (1,)), ((), ())), preferred_element_type=F32)


def _dot_tn(a, b):
    return lax.dot_general(a, b, (((0,), (0,)), ((), ())), preferred_element_type=F32)


def _split3(x):
    hi = x.astype(BF)
    r = x - hi.astype(F32)
    mid = r.astype(BF)
    lo = (r - mid.astype(F32)).astype(BF)
    return hi, mid, lo


def _split2(x):
    hi = x.astype(BF)
    lo = (x - hi.astype(F32)).astype(BF)
    return hi, lo


def _layer_spec(shape, layer):
    nd = len(shape)
    return pl.BlockSpec((None,) + tuple(shape[1:]), lambda *_: (layer,) + (0,) * (nd - 1))


def _ffn_body(x_ref, g_ref, win_ref, wout_ref, o_ref, *, n_chunks):
    x = x_ref[...]
    hn = _rms(x, g_ref[...]).astype(BF)
    tf = FFN_DIM // n_chunks
    acc = jnp.zeros_like(x)
    for c in range(n_chunks):
        gate = _dot(hn, win_ref[:, c * tf:(c + 1) * tf])
        up = _dot(hn, win_ref[:, FFN_DIM + c * tf:FFN_DIM + (c + 1) * tf])
        a = (_silu(gate) * up).astype(BF)
        acc = acc + _dot(a, wout_ref[c * tf:(c + 1) * tf, :])
    o_ref[...] = x + 0.5 * acc


def _ffn(x2, norm, w_in, w_out, layer):
    t = x2.shape[0]
    return pl.pallas_call(
        functools.partial(_ffn_body, n_chunks=2),
        out_shape=jax.ShapeDtypeStruct(x2.shape, F32),
        grid=(t // TOKEN_TILE,),
        in_specs=[pl.BlockSpec((TOKEN_TILE, D_MODEL), lambda i: (i, 0)),
                  _layer_spec(norm.shape, layer),
                  _layer_spec(w_in.shape, layer),
                  _layer_spec(w_out.shape, layer)],
        out_specs=pl.BlockSpec((TOKEN_TILE, D_MODEL), lambda i: (i, 0)),
        compiler_params=_params(1),
        name="ffn",
    )(x2, norm, w_in, w_out)


def _sgu_body(x_ref, g_ref, w_ref, vn_ref, ws_ref, bs_ref, o_ref):
    hn = _rms(x_ref[...], g_ref[...]).astype(BF)
    uv = _gelu(_dot(hn, w_ref[...]))
    u = uv[:, :WIDTH]
    v = _rms(uv[:, WIDTH:], vn_ref[...]).astype(BF)
    row = lax.broadcasted_iota(jnp.int32, (CHUNK, CHUNK), 0)
    col = lax.broadcasted_iota(jnp.int32, (CHUNK, CHUNK), 1)
    gw = WIDTH // SGU_GROUPS
    ws = [jnp.where(row >= col, ws_ref[g], 0.0).astype(BF) for g in range(SGU_GROUPS)]
    bias = bs_ref[...]
    for c in range(TOKEN_TILE // CHUNK):
        rows = slice(c * CHUNK, (c + 1) * CHUNK)
        mixed = jnp.concatenate(
            [_dot(ws[g], v[rows, g * gw:(g + 1) * gw]) for g in range(SGU_GROUPS)], axis=1)
        o_ref[rows, :] = (u[rows, :] * (mixed + bias)).astype(BF)


def _sgu(x3, norm, w, v_norm, w_s, b_full, layer):
    b, s, _ = x3.shape
    return pl.pallas_call(
        _sgu_body,
        out_shape=jax.ShapeDtypeStruct((b, s, WIDTH), BF),
        grid=(b, s // TOKEN_TILE),
        in_specs=[pl.BlockSpec((None, TOKEN_TILE, D_MODEL), lambda i, j: (i, j, 0)),
                  _layer_spec(norm.shape, layer),
                  _layer_spec(w.shape, layer),
                  _layer_spec(v_norm.shape, layer),
                  _layer_spec(w_s.shape, layer),
                  _layer_spec(b_full.shape, layer)],
        out_specs=pl.BlockSpec((None, TOKEN_TILE, WIDTH), lambda i, j: (i, j, 0)),
        compiler_params=_params(2),
        name="sgu",
    )(x3, norm, w, v_norm, w_s, b_full)


def _conv_body(x_ref, g_ref, w_ref, dw_ref, db_ref, cn_ref, o_ref, buf_ref):
    @pl.when(pl.program_id(1) == 0)
    def _():
        buf_ref[0:CONV_HIST, :] = jnp.zeros((CONV_HIST, WIDTH), F32)

    hn = _rms(x_ref[...], g_ref[...]).astype(BF)
    ab = _dot(hn, w_ref[...])
    buf_ref[CONV_HIST:CONV_HIST + TOKEN_TILE, :] = ab[:, :WIDTH] * jax.nn.sigmoid(ab[:, WIDTH:])
    acc = jnp.zeros((TOKEN_TILE, WIDTH), F32) + db_ref[...]
    first = CONV_HIST - (CONV_KERNEL - 1)
    for k in range(CONV_KERNEL):
        acc = acc + dw_ref[k:k + 1, :] * buf_ref[first + k:first + k + TOKEN_TILE, :]
    o_ref[...] = _silu(_rms(acc, cn_ref[...])).astype(BF)
    buf_ref[0:CONV_HIST, :] = buf_ref[TOKEN_TILE:TOKEN_TILE + CONV_HIST, :]


def _conv_module(x3, norm, w, dw_w, dw_b, c_norm, layer):
    b, s, _ = x3.shape
    return pl.pallas_call(
        _conv_body,
        out_shape=jax.ShapeDtypeStruct((b, s, WIDTH), BF),
        grid=(b, s // TOKEN_TILE),
        in_specs=[pl.BlockSpec((None, TOKEN_TILE, D_MODEL), lambda i, j: (i, j, 0)),
                  _layer_spec(norm.shape, layer),
                  _layer_spec(w.shape, layer),
                  _layer_spec(dw_w.shape, layer),
                  _layer_spec(dw_b.shape, layer),
                  _layer_spec(c_norm.shape, layer)],
        out_specs=pl.BlockSpec((None, TOKEN_TILE, WIDTH), lambda i, j: (i, j, 0)),
        scratch_shapes=[pltpu.VMEM((TOKEN_TILE + CONV_HIST, WIDTH), F32)],
        compiler_params=_params(2),
        name="conv_module",
    )(x3, norm, w, dw_w, dw_b, c_norm)


def _ssd_body(x_ref, g_ref, w_ref, cw_ref, cb_ref, dtb_ref, a_ref, dsk_ref, ng_ref, o_ref,
              buf_ref, st_ref):
    hw = SSM_REP * SSM_HEAD_DIM

    @pl.when(pl.program_id(1) == 0)
    def _():
        buf_ref[0:SSM_HIST, :] = jnp.zeros((SSM_HIST, SSM_CONV_DIM), F32)
        st_ref[...] = jnp.zeros(st_ref.shape, F32)

    hn = _rms(x_ref[...], g_ref[...]).astype(BF)
    proj = _dot(hn, w_ref[...])
    z = proj[:, :WIDTH]
    buf_ref[SSM_HIST:SSM_HIST + TOKEN_TILE, :] = proj[:, WIDTH:WIDTH + SSM_CONV_DIM]
    dt = jax.nn.softplus(proj[:, WIDTH + SSM_CONV_DIM:] + dtb_ref[...])
    xbc = jnp.zeros((TOKEN_TILE, SSM_CONV_DIM), F32) + cb_ref[...]
    first = SSM_HIST - (SSM_CONV - 1)
    for k in range(SSM_CONV):
        xbc = xbc + cw_ref[k:k + 1, :] * buf_ref[first + k:first + k + TOKEN_TILE, :]
    buf_ref[0:SSM_HIST, :] = buf_ref[TOKEN_TILE:TOKEN_TILE + SSM_HIST, :]
    xbc = _silu(xbc)
    xs = xbc[:, :WIDTH]
    a_all = dt * a_ref[...]
    xdt = xs * dt

    row = lax.broadcasted_iota(jnp.int32, (CHUNK, CHUNK), 0)
    col = lax.broadcasted_iota(jnp.int32, (CHUNK, CHUNK), 1)
    causal = row >= col
    tril = jnp.where(causal, 1.0, 0.0).astype(BF)
    lane = lax.broadcasted_iota(jnp.int32, (CHUNK, 2 * SSM_HEAD_DIM), 1)
    low_half = lane < SSM_HEAD_DIM

    for c in range(TOKEN_TILE // CHUNK):
        rows = slice(c * CHUNK, (c + 1) * CHUNK)
        hi, mid, lo = _split3(a_all[rows, :])
        cs = _dot(tril, hi) + _dot(tril, mid) + _dot(tril, lo)
        cs_t = cs.T
        cs_last = cs[CHUNK - 1:CHUNK, :]
        xc = xdt[rows, :]
        x_decay = (xc * jnp.exp(cs_last - cs)).astype(BF)
        xc_bf = xc.astype(BF)
        grow = jnp.exp(cs)
        y_parts = []
        for g in range(SSM_GROUPS):
            bg = xbc[rows, WIDTH + g * SSM_STATE:WIDTH + (g + 1) * SSM_STATE].astype(BF)
            cg = xbc[rows, WIDTH + (SSM_GROUPS + g) * SSM_STATE:
                     WIDTH + (SSM_GROUPS + g + 1) * SSM_STATE].astype(BF)
            cb = _dot_nt(cg, bg)
            gl = slice(g * hw, (g + 1) * hw)
            y_off = _dot(cg, st_ref[g].astype(BF)) * grow[:, gl]
            y_diag = []
            for pair in range(SSM_REP // 2):
                scores = []
                for hh in range(2):
                    hc = g * hw + (2 * pair + hh) * SSM_HEAD_DIM
                    seg = cs[:, hc:hc + 1] - cs_t[hc:hc + 1, :]
                    decay = jnp.exp(jnp.where(causal, seg, NEG))
                    scores.append((cb * decay).astype(BF))
                xp = xc_bf[:, g * hw + pair * 2 * SSM_HEAD_DIM:g * hw + (pair + 1) * 2 * SSM_HEAD_DIM]
                zero = jnp.zeros_like(xp)
                rhs = jnp.concatenate([jnp.where(low_half, xp, zero),
                                       jnp.where(low_half, zero, xp)], axis=0)
                y_diag.append(_dot(jnp.concatenate(scores, axis=1), rhs))
            y_parts.append(jnp.concatenate(y_diag, axis=1) + y_off)
            st_ref[g] = jnp.exp(cs_last[:, gl]) * st_ref[g] + _dot_tn(bg, x_decay[:, gl])
        y = jnp.concatenate(y_parts, axis=1) + xs[rows, :] * dsk_ref[...]
        y = y * _silu(z[rows, :])
        o_ref[rows, :] = _rms(y, ng_ref[...]).astype(BF)


def _ssd(x3, norm, w, conv_w, conv_b, dt_bias, a_neg, d_skip, n_gain, layer):
    b, s, _ = x3.shape
    ins = (norm, w, conv_w, conv_b, dt_bias, a_neg, d_skip, n_gain)
    return pl.pallas_call(
        _ssd_body,
        out_shape=jax.ShapeDtypeStruct((b, s, WIDTH), BF),
        grid=(b, s // TOKEN_TILE),
        in_specs=[pl.BlockSpec((None, TOKEN_TILE, D_MODEL), lambda i, j: (i, j, 0))]
        + [_layer_spec(a.shape, layer) for a in ins],
        out_specs=pl.BlockSpec((None, TOKEN_TILE, WIDTH), lambda i, j: (i, j, 0)),
        scratch_shapes=[pltpu.VMEM((TOKEN_TILE + SSM_HIST, SSM_CONV_DIM), F32),
                        pltpu.VMEM((SSM_GROUPS, SSM_STATE, SSM_REP * SSM_HEAD_DIM), F32)],
        compiler_params=_params(2),
        name="ssd",
    )(x3, *ins)


def _head_rms(x, ones_blockdiag, gain, scale):
    hi, lo = _split2(x * x)
    ss = _dot(hi, ones_blockdiag) + _dot(lo, ones_blockdiag)
    return x * lax.rsqrt(ss * (1.0 / HEAD_DIM) + EPS) * (gain * scale)


def _nsa_proj_body(x_ref, g_ref, w_ref, qn_ref, kn_ref, bdq_ref, bdk_ref,
                   q_ref, kc_ref, vc_ref, ks_ref, vs_ref, kw_ref, vw_ref, gate_ref):
    hn = _rms(x_ref[...], g_ref[...]).astype(BF)
    proj = _dot(hn, w_ref[...])
    q_ref[...] = _head_rms(proj[:, :WIDTH], bdq_ref[...], qn_ref[...], HEAD_DIM ** -0.5).astype(BF)
    o = WIDTH
    kc_ref[...] = proj[:, o:o + KV_WIDTH].astype(BF)
    vc_ref[...] = proj[:, o + KV_WIDTH:o + 2 * KV_WIDTH].astype(BF)
    ks_ref[...] = _head_rms(proj[:, o + 2 * KV_WIDTH:o + 3 * KV_WIDTH], bdk_ref[...], kn_ref[...], 1.0).astype(BF)
    vs_ref[...] = proj[:, o + 3 * KV_WIDTH:o + 4 * KV_WIDTH].astype(BF)
    kw_ref[...] = _head_rms(proj[:, o + 4 * KV_WIDTH:o + 5 * KV_WIDTH], bdk_ref[...], kn_ref[...], 1.0).astype(BF)
    vw_ref[...] = proj[:, o + 5 * KV_WIDTH:o + 6 * KV_WIDTH].astype(BF)
    gate_ref[...] = jax.nn.sigmoid(proj[:, o + 6 * KV_WIDTH:])


def _nsa_proj(x3, norm, w, q_gain, k_gain, bd_q, bd_k, layer):
    b, s, _ = x3.shape
    tile = lambda n: pl.BlockSpec((None, TOKEN_TILE, n), lambda i, j: (i, j, 0))
    const = lambda a: pl.BlockSpec(a.shape, lambda i, j: (0,) * a.ndim)
    kv = jax.ShapeDtypeStruct((b, s, KV_WIDTH), BF)
    return pl.pallas_call(
        _nsa_proj_body,
        out_shape=(jax.ShapeDtypeStruct((b, s, WIDTH), BF), kv, kv, kv, kv, kv, kv,
                   jax.ShapeDtypeStruct((b, s, LANES), F32)),
        grid=(b, s // TOKEN_TILE),
        in_specs=[tile(D_MODEL), _layer_spec(norm.shape, layer), _layer_spec(w.shape, layer),
                  _layer_spec(q_gain.shape, layer), _layer_spec(k_gain.shape, layer),
                  const(bd_q), const(bd_k)],
        out_specs=(tile(WIDTH),) + (tile(KV_WIDTH),) * 6 + (tile(LANES),),
        compiler_params=_params(2),
        name="nsa_proj",
    )(x3, norm, w, q_gain, k_gain, bd_q, bd_k)


def _softmax_rows(s):
    m = jnp.max(s, axis=1, keepdims=True)
    return jnp.exp(s - m)


def _nsa_attn_body(q_ref, gate_ref, kc_ref, vc_ref, ks_ref, vs_ref, kw_ref, vw_ref,
                   w1k_ref, w1kf_ref, w2k_ref, pek_ref, w1v_ref, w1vf_ref, w2v_ref, pev_ref, kn_ref,
                   cpos_ref, ovt_ref, eye_ref,
                   o_ref, qs_ref, kcmp_ref, vcmp_ref, m_ref, acc_ref):
    j = pl.program_id(2)
    t0 = j * CHUNK
    nq = NSA_REP * CHUNK
    n_sel = ovt_ref.shape[0]

    def compress(x_ref, w1_ref, w1f_ref, w2_ref, pe_ref):
        x = x_ref[...]
        lo_half = _dot(x, w1_ref[0])
        hi_half = _dot(x, w1_ref[1])
        pe_bias = _dot(pe_ref[...], w1f_ref[...])[0:1, :]
        hidden = lo_half + pltpu.roll(hi_half, CHUNK - 1, 0) + pe_bias
        return _dot(_gelu(hidden).astype(BF), w2_ref[...])

    @pl.when(j == 0)
    def _():
        k_cmp = _rms(compress(kc_ref, w1k_ref, w1kf_ref, w2k_ref, pek_ref), kn_ref[...])
        kcmp_ref[...] = jnp.concatenate(
            [k_cmp.astype(BF), cpos_ref[...]], axis=1)
        v_cmp = compress(vc_ref, w1v_ref, w1vf_ref, w2v_ref, pev_ref)
        lane = lax.broadcasted_iota(jnp.int32, (CHUNK, LANES - HEAD_DIM), 1)
        vcmp_ref[...] = jnp.concatenate(
            [v_cmp.astype(BF), jnp.where(lane == 0, 1.0, 0.0).astype(BF)], axis=1)

    q = q_ref[...]
    qi = lax.broadcasted_iota(jnp.int32, (nq, CHUNK), 0) % CHUNK
    ci = lax.broadcasted_iota(jnp.int32, (nq, CHUNK), 1)

    s_c = _dot_nt(q, kcmp_ref[...])
    valid_c = (ci * CMP_STRIDE + (CMP_BLOCK - 1) <= t0 + qi) & (ci < CHUNK - 1)
    e_c = jnp.where(valid_c, _softmax_rows(jnp.where(valid_c, s_c, NEG)), 0.0)
    p_c = e_c / jnp.maximum(jnp.sum(e_c, axis=1, keepdims=True), 1e-30)
    o_c = _dot(p_c.astype(BF), vcmp_ref[...])[:, :HEAD_DIM]

    p_sum = p_c[0:CHUNK] + p_c[CHUNK:2 * CHUNK] + p_c[2 * CHUNK:3 * CHUNK] + p_c[3 * CHUNK:4 * CHUNK]
    p_hi, p_lo = _split2(p_sum)
    imp = _dot_nt(ovt_ref[...], p_hi) + _dot_nt(ovt_ref[...], p_lo)
    blk = lax.broadcasted_iota(jnp.int32, (n_sel, CHUNK), 0)
    tq = t0 + lax.broadcasted_iota(jnp.int32, (n_sel, CHUNK), 1)
    cur = tq // SEL_BLOCK
    forced = (blk == 0) | (blk == cur) | (blk == cur - 1)
    score = jnp.where(forced, FORCE, jnp.where(blk * SEL_BLOCK <= tq, imp, NEG))
    blk_f = blk.astype(F32)
    chosen = jnp.zeros((n_sel, CHUNK), F32)
    for _ in range(SEL_TOP_N):
        best = jnp.max(score, axis=0, keepdims=True)
        first = jnp.min(jnp.where(score == best, blk_f, 1e9), axis=0, keepdims=True)
        pick = blk_f == first
        chosen = jnp.where(pick, 1.0, chosen)
        score = jnp.where(pick, -3e38, score)
    chosen_q = _dot_nt(eye_ref[...], chosen.astype(BF))
    sel_bias = ((chosen_q - 1.0) * (-NEG)).astype(BF)
    qs_ref[...] = q
    for r in range(NSA_REP):
        qs_ref[r * CHUNK:(r + 1) * CHUNK, HEAD_DIM:HEAD_DIM + n_sel] = sel_bias
    qs = qs_ref[...]

    m_ref[...] = jnp.full(m_ref.shape, NEG, F32)
    acc_ref[...] = jnp.zeros(acc_ref.shape, F32)

    def sel_step(start, masked):
        k = ks_ref[pl.ds(start, SEL_KEYS), :]
        s = _dot_nt(qs, k)
        if masked:
            qi2 = lax.broadcasted_iota(jnp.int32, (nq, SEL_KEYS), 0) % CHUNK
            ki2 = lax.broadcasted_iota(jnp.int32, (nq, SEL_KEYS), 1)
            s = jnp.where(ki2 - qi2 <= t0 - start, s, NEG)
        m_old = m_ref[...]
        m_new = jnp.maximum(m_old, jnp.max(s, axis=1, keepdims=True))
        p = jnp.exp(s - m_new).astype(BF)
        acc_ref[...] = jnp.exp(m_old - m_new) * acc_ref[...] + _dot(p, vs_ref[pl.ds(start, SEL_KEYS), :])
        m_ref[...] = m_new

    n_full = j // (SEL_KEYS // CHUNK)

    def loop_body(i, carry):
        sel_step(pl.multiple_of(i * SEL_KEYS, SEL_KEYS), False)
        return carry

    lax.fori_loop(0, n_full, loop_body, 0)
    sel_step(pl.multiple_of(n_full * SEL_KEYS, SEL_KEYS), True)
    acc = acc_ref[...]
    o_s = acc[:, :HEAD_DIM] / acc[:, HEAD_DIM:HEAD_DIM + 1]

    w_start = pl.multiple_of(jnp.maximum(t0 - WINDOW, 0), CHUNK)
    shift = t0 - w_start
    qi_w = lax.broadcasted_iota(jnp.int32, (nq, WINDOW), 0) % CHUNK
    ki_w = lax.broadcasted_iota(jnp.int32, (nq, WINDOW), 1)
    rel_a = shift + qi_w - ki_w
    s_a = jnp.where((rel_a >= 0) & (rel_a < WINDOW), _dot_nt(q, kw_ref[pl.ds(w_start, WINDOW), :]), NEG)
    rel_b = shift - WINDOW + qi - ci
    w_start_b = pl.multiple_of(w_start + WINDOW, CHUNK)
    s_b = jnp.where((rel_b >= 0) & (rel_b < WINDOW), _dot_nt(q, kw_ref[pl.ds(w_start_b, CHUNK), :]), NEG)
    m_w = jnp.maximum(jnp.max(s_a, axis=1, keepdims=True), jnp.max(s_b, axis=1, keepdims=True))
    o_w = (_dot(jnp.exp(s_a - m_w).astype(BF), vw_ref[pl.ds(w_start, WINDOW), :])
           + _dot(jnp.exp(s_b - m_w).astype(BF), vw_ref[pl.ds(w_start_b, CHUNK), :]))
    o_w = o_w[:, :HEAD_DIM] / o_w[:, HEAD_DIM:HEAD_DIM + 1]

    gate = gate_ref[...]
    o_ref[...] = (gate[:, 0:1] * o_c + gate[:, 1:2] * o_s + gate[:, 2:3] * o_w).astype(BF)


def _nsa_attn(q_aug, gates, kc2, vc2, ks_aug, vs_aug, kw_aug, vw_aug,
              w1k, w1kf, w2k, pek, w1v, w1vf, w2v, pev, k_gain, cpos, ov_t, eye, layer):
    b, g, nblk, nq, _ = q_aug.shape
    s = ks_aug.shape[2]
    blk = lambda n: pl.BlockSpec((None, None, None, nq, n), lambda i, h, j: (i, h, j, 0, 0))
    seq = lambda a: pl.BlockSpec((None, None) + a.shape[2:], lambda i, h, j: (i, h, 0, 0))
    per_b = lambda a: pl.BlockSpec((None,) + a.shape[1:], lambda i, h, j: (i, 0, 0))
    per_g = lambda a: pl.BlockSpec((None, None) + a.shape[2:], lambda i, h, j: (layer, h) + (0,) * (a.ndim - 2))
    lay = lambda a: pl.BlockSpec((None,) + a.shape[1:], lambda i, h, j: (layer,) + (0,) * (a.ndim - 1))
    const = lambda a: pl.BlockSpec(a.shape, lambda i, h, j: (0,) * a.ndim)
    return pl.pallas_call(
        _nsa_attn_body,
        out_shape=jax.ShapeDtypeStruct((b, g, nblk, nq, HEAD_DIM), BF),
        grid=(b, g, nblk),
        in_specs=[blk(LANES), blk(3), per_b(kc2), per_b(vc2),
                  seq(ks_aug), seq(vs_aug), seq(kw_aug), seq(vw_aug),
                  per_g(w1k), lay(w1kf), lay(w2k), lay(pek),
                  per_g(w1v), lay(w1vf), lay(w2v), lay(pev), lay(k_gain),
                  const(cpos), const(ov_t), const(eye)],
        out_specs=blk(HEAD_DIM),
        scratch_shapes=[pltpu.VMEM((nq, LANES), BF),
                        pltpu.VMEM((CHUNK, LANES), BF),
                        pltpu.VMEM((CHUNK, LANES), BF),
                        pltpu.VMEM((nq, 1), F32),
                        pltpu.VMEM((nq, LANES), F32)],
        compiler_params=_params(3),
        name="nsa_attn",
    )(q_aug, gates, kc2, vc2, ks_aug, vs_aug, kw_aug, vw_aug,
      w1k, w1kf, w2k, pek, w1v, w1vf, w2v, pev, k_gain, cpos, ov_t, eye)


def _merge_body(x_ref, ya_ref, yb_ref, yc_ref, yd_ref, g_ref, wg_ref, wb_ref, wo_ref, o_ref):
    x = x_ref[...]
    hn = _rms(x, g_ref[...]).astype(BF)
    merged = jnp.zeros_like(x)
    for i, y_ref in enumerate((ya_ref, yb_ref, yc_ref, yd_ref)):
        gate = jax.nn.sigmoid(_dot(hn, wg_ref[:, i * D_MODEL:(i + 1) * D_MODEL]))
        merged = merged + gate * _dot(y_ref[...], wb_ref[i])
    o_ref[...] = x + _dot(merged.astype(BF), wo_ref[...])


def _merge(x2, ya, yb, yc, yd, norm, w_gate, w_branch, w_out, layer):
    t = x2.shape[0]
    tile = lambda n: pl.BlockSpec((TOKEN_TILE, n), lambda i: (i, 0))
    return pl.pallas_call(
        _merge_body,
        out_shape=jax.ShapeDtypeStruct(x2.shape, F32),
        grid=(t // TOKEN_TILE,),
        in_specs=[tile(D_MODEL)] + [tile(WIDTH)] * 4
        + [_layer_spec(a.shape, layer) for a in (norm, w_gate, w_branch, w_out)],
        out_specs=tile(D_MODEL),
        compiler_params=_params(1),
        name="merge",
    )(x2, ya, yb, yc, yd, norm, w_gate, w_branch, w_out)


def _alibi_slopes():
    return np.array([2.0 ** (-8.0 * (i + 1) / NSA_HEADS) for i in range(NSA_HEADS)], np.float32)


def _nsa_constants(seq):
    n_sel = seq // SEL_BLOCK
    n_cmp = (seq - CMP_BLOCK) // CMP_STRIDE + 1
    assert n_cmp == CHUNK - 1 and n_sel <= SEL_BLOCK // 2
    pos_cols = LANES - HEAD_DIM - n_sel
    slopes = _alibi_slopes().reshape(NSA_GROUPS, NSA_REP)
    q_const = np.zeros((NSA_GROUPS, NSA_REP * CHUNK, LANES - HEAD_DIM), np.float32)
    row_slope = np.repeat(slopes, CHUNK, axis=1)
    q_const[:, :, n_sel] = row_slope * SEL_BLOCK
    q_const[:, :, n_sel + 1] = row_slope
    pos = np.arange(seq)
    pos_part = np.zeros((seq, pos_cols), np.float32)
    pos_part[:, 0] = pos // SEL_BLOCK
    pos_part[:, 1] = pos % SEL_BLOCK
    onehot = (pos[:, None] // SEL_BLOCK == np.arange(n_sel)[None, :]).astype(np.float32)
    k_sel_const = np.concatenate([onehot, pos_part], axis=1)
    k_win_const = np.concatenate([np.zeros_like(onehot), pos_part], axis=1)
    v_const = np.zeros((seq, LANES - HEAD_DIM), np.float32)
    v_const[:, 0] = 1.0
    c_mid = np.arange(CHUNK) * CMP_STRIDE + (CMP_BLOCK - 1) / 2.0
    cmp_const = np.zeros((CHUNK, LANES - HEAD_DIM), np.float32)
    cmp_const[:, n_sel] = c_mid // SEL_BLOCK
    cmp_const[:, n_sel + 1] = c_mid % SEL_BLOCK
    c_start = np.arange(CHUNK) * CMP_STRIDE
    s_start = np.arange(n_sel) * SEL_BLOCK
    overlap_t = ((c_start[None, :] <= s_start[:, None] + SEL_BLOCK - 1)
                 & (c_start[None, :] + CMP_BLOCK - 1 >= s_start[:, None])
                 & (np.arange(CHUNK)[None, :] < n_cmp)).astype(np.float32)
    bf = lambda a: jnp.asarray(a, BF)
    return dict(q_const=bf(q_const), k_sel=bf(k_sel_const), k_win=bf(k_win_const), v_const=bf(v_const),
                cmp_const=bf(cmp_const), overlap_t=bf(overlap_t), eye=bf(np.eye(CHUNK, dtype=np.float32)))


def _block_diag_ones(n):
    idx = np.arange(n) // HEAD_DIM
    return jnp.asarray(idx[:, None] == idx[None, :], BF)


def _compress_weights(w1):
    nl = w1.shape[0]
    halves = w1.reshape(nl, 2, CMP_STRIDE, 1, HEAD_DIM, CMP_HIDDEN)
    group_sel = jnp.eye(NSA_GROUPS, dtype=w1.dtype)
    big = halves[:, None] * group_sel[None, :, None, None, :, None, None]
    return big.reshape(nl, NSA_GROUPS, 2, CMP_STRIDE * KV_WIDTH, CMP_HIDDEN).astype(BF)


def _pe_rows(pe):
    nl = pe.shape[0]
    flat = pe.reshape(nl, 1, CMP_BLOCK * HEAD_DIM)
    return jnp.concatenate([flat, jnp.zeros((nl, 7, CMP_BLOCK * HEAD_DIM), pe.dtype)], axis=1).astype(BF)


def _nsa_params(seq, q_norm, k_norm, pe_k, w1_k, w2_k, pe_v, w1_v, w2_v):
    row = lambda a: a[:, None, :]
    p = _nsa_constants(seq)
    p.update(q_gain=row(jnp.tile(q_norm, (1, NSA_HEADS))), k_gain2=row(jnp.tile(k_norm, (1, NSA_GROUPS))),
             k_gain=row(k_norm), w1k=_compress_weights(w1_k), w1v=_compress_weights(w1_v),
             w1kf=w1_k.astype(BF), w1vf=w1_v.astype(BF), w2k=w2_k.astype(BF), w2v=w2_v.astype(BF),
             pek=_pe_rows(pe_k), pev=_pe_rows(pe_v),
             bd_q=_block_diag_ones(WIDTH), bd_k=_block_diag_ones(KV_WIDTH))
    return p


def _nsa_mixer(x3, mxn, w_c, p, layer):
    bsz, seq, _ = x3.shape
    n_blk = seq // CHUNK
    qn, kc, vc, ksn, vs, kwn, vw, gate = _nsa_proj(x3, mxn, w_c, p["q_gain"], p["k_gain2"], p["bd_q"], p["bd_k"], layer)

    def heads_major(t, const):
        t = jnp.swapaxes(t.reshape(bsz, seq, NSA_GROUPS, HEAD_DIM), 1, 2)
        return jnp.concatenate([t, jnp.broadcast_to(const, (bsz, NSA_GROUPS) + const.shape)], axis=-1)

    q_blk = qn.reshape(bsz, n_blk, CHUNK, NSA_GROUPS, NSA_REP, HEAD_DIM).transpose(0, 3, 1, 4, 2, 5)
    q_blk = q_blk.reshape(bsz, NSA_GROUPS, n_blk, NSA_REP * CHUNK, HEAD_DIM)
    q_aug = jnp.concatenate(
        [q_blk, jnp.broadcast_to(p["q_const"][None, :, None], q_blk.shape[:4] + (LANES - HEAD_DIM,))], axis=-1)
    g_blk = gate[:, :, :3 * NSA_HEADS].reshape(bsz, n_blk, CHUNK, NSA_GROUPS, NSA_REP, 3)
    g_blk = g_blk.transpose(0, 3, 1, 4, 2, 5).reshape(bsz, NSA_GROUPS, n_blk, NSA_REP * CHUNK, 3)
    y_c = _nsa_attn(q_aug, g_blk,
                    kc.reshape(bsz, seq // CMP_STRIDE, CMP_STRIDE * KV_WIDTH),
                    vc.reshape(bsz, seq // CMP_STRIDE, CMP_STRIDE * KV_WIDTH),
                    heads_major(ksn, p["k_sel"]), heads_major(vs, p["v_const"]),
                    heads_major(kwn, p["k_win"]), heads_major(vw, p["v_const"]),
                    p["w1k"], p["w1kf"], p["w2k"], p["pek"], p["w1v"], p["w1vf"], p["w2v"], p["pev"], p["k_gain"],
                    p["cmp_const"], p["overlap_t"], p["eye"], layer)
    y_c = y_c.reshape(bsz, NSA_GROUPS, n_blk, NSA_REP, CHUNK, HEAD_DIM).transpose(0, 2, 4, 1, 3, 5)
    return y_c.reshape(bsz, seq, WIDTH)


def kernel(x, ffn1_norm, ffn1_w_in, ffn1_w_out, mix_norm, w_in, sgu_v_norm, sgu_w, sgu_b, ssm_conv_w, ssm_conv_b, ssm_dt_bias, ssm_a_log, ssm_d, ssm_norm, nsa_q_norm, nsa_k_norm, nsa_pe_k, nsa_w1_k, nsa_w2_k, nsa_pe_v, nsa_w1_v, nsa_w2_v, conv_dw_w, conv_dw_b, conv_norm, w_branch, w_out, ffn2_norm, ffn2_w_in, ffn2_w_out):
    bsz, seq, _ = x.shape
    depth = w_in.shape[0]
    n_blk = seq // CHUNK
    row = lambda a: a[:, None, :]
    rep = lambda a, n: jnp.repeat(a, n, axis=-1)

    f1n, f2n, mxn = row(ffn1_norm), row(ffn2_norm), row(mix_norm)
    f1_in, f1_out = ffn1_w_in.astype(BF), ffn1_w_out.astype(BF)
    f2_in, f2_out = ffn2_w_in.astype(BF), ffn2_w_out.astype(BF)
    w_a = w_in[:, :, OFF_UV:OFF_Z].astype(BF)
    w_b = jnp.concatenate([w_in[:, :, OFF_Z:OFF_DT], rep(w_in[:, :, OFF_DT:OFF_Q], SSM_HEAD_DIM)], axis=-1).astype(BF)
    w_c = jnp.concatenate([w_in[:, :, OFF_Q:OFF_AB],
                           jnp.zeros((depth, D_MODEL, LANES - 3 * NSA_HEADS), w_in.dtype)], axis=-1).astype(BF)
    w_d = w_in[:, :, OFF_AB:OFF_MERGE].astype(BF)
    w_g = w_in[:, :, OFF_MERGE:].astype(BF)
    sgu_bias = rep(jnp.swapaxes(sgu_b, 1, 2), WIDTH // SGU_GROUPS)
    dt_bias = row(rep(ssm_dt_bias, SSM_HEAD_DIM))
    a_neg = row(rep(-jnp.exp(ssm_a_log), SSM_HEAD_DIM))
    d_skip = row(rep(ssm_d, SSM_HEAD_DIM))
    wb, wo = w_branch.astype(BF), w_out.astype(BF)
    nsa_p = _nsa_params(seq, nsa_q_norm, nsa_k_norm, nsa_pe_k, nsa_w1_k, nsa_w2_k, nsa_pe_v, nsa_w1_v, nsa_w2_v)

    x2 = x.reshape(bsz * seq, D_MODEL)
    for l in range(depth):
        x2 = _ffn(x2, f1n, f1_in, f1_out, l)
        x3 = x2.reshape(bsz, seq, D_MODEL)
        y_a = _sgu(x3, mxn, w_a, row(sgu_v_norm), sgu_w, sgu_bias, l)
        y_b = _ssd(x3, mxn, w_b, ssm_conv_w, row(ssm_conv_b), dt_bias, a_neg, d_skip, row(ssm_norm), l)
        y_d = _conv_module(x3, mxn, w_d, conv_dw_w, row(conv_dw_b), row(conv_norm), l)
        y_c = _nsa_mixer(x3, mxn, w_c, nsa_p, l)
        flat = lambda y: y.reshape(bsz * seq, WIDTH)
        x2 = _merge(x2, flat(y_a), flat(y_b), flat(y_c), flat(y_d), mxn, w_g, wb, wo, l)
        x2 = _ffn(x2, f2n, f2_in, f2_out, l)
    return x2.reshape(bsz, seq, D_MODEL)
```

```python
import functools
import math

import numpy as np
import jax
import jax.numpy as jnp
from jax import lax
from jax.experimental import pallas as pl
from jax.experimental.pallas import tpu as pltpu

F32 = jnp.float32
BF = jnp.bfloat16

D_MODEL = 1024
FFN_DIM = 2816
WIDTH = 512
N_BRANCH = 4
SGU_GROUPS = 4
CHUNK = 128
SSM_HEADS = 8
SSM_HEAD_DIM = 64
SSM_GROUPS = 2
SSM_STATE = 128
SSM_CONV = 4
SSM_CONV_DIM = WIDTH + 2 * SSM_GROUPS * SSM_STATE
SSM_REP = SSM_HEADS // SSM_GROUPS
NSA_HEADS = 8
NSA_GROUPS = 2
NSA_REP = NSA_HEADS // NSA_GROUPS
HEAD_DIM = 64
KV_WIDTH = NSA_GROUPS * HEAD_DIM
CMP_BLOCK = 32
CMP_STRIDE = 16
CMP_HIDDEN = 128
SEL_BLOCK = 64
SEL_TOP_N = 8
WINDOW = 256
CONV_KERNEL = 31
EPS = 1e-6
NEG = -1e30
FORCE = 1e9

OFF_UV = 0
OFF_Z = OFF_UV + 2 * WIDTH
OFF_XBC = OFF_Z + WIDTH
OFF_DT = OFF_XBC + SSM_CONV_DIM
OFF_Q = OFF_DT + SSM_HEADS
OFF_KV = OFF_Q + WIDTH
OFF_GATE = OFF_KV + 6 * KV_WIDTH
OFF_AB = OFF_GATE + 3 * NSA_HEADS
OFF_MERGE = OFF_AB + 2 * WIDTH
IN_PROJ_DIM = OFF_MERGE + N_BRANCH * D_MODEL

LANES = 128
SUBLANES = 8
VMEM_LIMIT_BYTES = 56 * 1024 * 1024

TOKEN_TILE = 256
CONV_HIST = 32
SSM_HIST = 8
SEL_KEYS = 256
V_ROWS = 80


def _params(n_axes):
    return pltpu.CompilerParams(dimension_semantics=("arbitrary",) * n_axes,
                                vmem_limit_bytes=VMEM_LIMIT_BYTES)


def _rms(x, g):
    return x * lax.rsqrt(jnp.mean(x * x, axis=-1, keepdims=True) + EPS) * g


def _silu(x):
    return x * jax.nn.sigmoid(x)


def _gelu(x):
    return 0.5 * x * (1.0 + jnp.tanh(math.sqrt(2.0 / math.pi) * (x + 0.044715 * (x * x * x))))


def _dot(a, b):
    return jnp.dot(a, b, preferred_element_type=F32)


def _dot_nt(a, b):
    return lax.dot_general(a, b, (((1,), (1,)), ((), ())), preferred_element_type=F32)


def _dot_tn(a, b):
    return lax.dot_general(a, b, (((0,), (0,)), ((), ())), preferred_element_type=F32)


def _split3(x):
    hi = x.astype(BF)
    r = x - hi.astype(F32)
    mid = r.astype(BF)
    lo = (r - mid.astype(F32)).astype(BF)
    return hi, mid, lo


def _split2(x):
    hi = x.astype(BF)
    lo = (x - hi.astype(F32)).astype(BF)
    return hi, lo


def _layer_spec(shape, layer):
    nd = len(shape)
    return pl.BlockSpec((None,) + tuple(shape[1:]), lambda *_: (layer,) + (0,) * (nd - 1))


def _ffn_body(x_ref, g_ref, win_ref, wout_ref, o_ref, *, n_chunks):
    x = x_ref[...]
    hn = _rms(x, g_ref[...]).astype(BF)
    tf = FFN_DIM // n_chunks
    acc = jnp.zeros_like(x)
    for c in range(n_chunks):
        gate = _dot(hn, win_ref[:, c * tf:(c + 1) * tf])
        up = _dot(hn, win_ref[:, FFN_DIM + c * tf:FFN_DIM + (c + 1) * tf])
        a = (_silu(gate) * up).astype(BF)
        acc = acc + _dot(a, wout_ref[c * tf:(c + 1) * tf, :])
    o_ref[...] = x + 0.5 * acc


def _ffn(x2, norm, w_in, w_out, layer):
    t = x2.shape[0]
    return pl.pallas_call(
        functools.partial(_ffn_body, n_chunks=2),
        out_shape=jax.ShapeDtypeStruct(x2.shape, F32),
        grid=(t // TOKEN_TILE,),
        in_specs=[pl.BlockSpec((TOKEN_TILE, D_MODEL), lambda i: (i, 0)),
                  _layer_spec(norm.shape, layer),
                  _layer_spec(w_in.shape, layer),
                  _layer_spec(w_out.shape, layer)],
        out_specs=pl.BlockSpec((TOKEN_TILE, D_MODEL), lambda i: (i, 0)),
        compiler_params=_params(1),
        name="ffn",
    )(x2, norm, w_in, w_out)


def _sgu_body(x_ref, g_ref, w_ref, vn_ref, ws_ref, bs_ref, o_ref):
    hn = _rms(x_ref[...], g_ref[...]).astype(BF)
    uv = _gelu(_dot(hn, w_ref[...]))
    u = uv[:, :WIDTH]
    v = _rms(uv[:, WIDTH:], vn_ref[...]).astype(BF)
    row = lax.broadcasted_iota(jnp.int32, (CHUNK, CHUNK), 0)
    col = lax.broadcasted_iota(jnp.int32, (CHUNK, CHUNK), 1)
    gw = WIDTH // SGU_GROUPS
    ws = [jnp.where(row >= col, ws_ref[g], 0.0).astype(BF) for g in range(SGU_GROUPS)]
    bias = bs_ref[...]
    for c in range(TOKEN_TILE // CHUNK):
        rows = slice(c * CHUNK, (c + 1) * CHUNK)
        mixed = jnp.concatenate(
            [_dot(ws[g], v[rows, g * gw:(g + 1) * gw]) for g in range(SGU_GROUPS)], axis=1)
        o_ref[rows, :] = (u[rows, :] * (mixed + bias)).astype(BF)


def _sgu(x3, norm, w, v_norm, w_s, b_full, layer):
    b, s, _ = x3.shape
    return pl.pallas_call(
        _sgu_body,
        out_shape=jax.ShapeDtypeStruct((b, s, WIDTH), BF),
        grid=(b, s // TOKEN_TILE),
        in_specs=[pl.BlockSpec((None, TOKEN_TILE, D_MODEL), lambda i, j: (i, j, 0)),
                  _layer_spec(norm.shape, layer),
                  _layer_spec(w.shape, layer),
                  _layer_spec(v_norm.shape, layer),
                  _layer_spec(w_s.shape, layer),
                  _layer_spec(b_full.shape, layer)],
        out_specs=pl.BlockSpec((None, TOKEN_TILE, WIDTH), lambda i, j: (i, j, 0)),
        compiler_params=_params(2),
        name="sgu",
    )(x3, norm, w, v_norm, w_s, b_full)


def _conv_body(x_ref, g_ref, w_ref, dw_ref, db_ref, cn_ref, o_ref, buf_ref):
    @pl.when(pl.program_id(1) == 0)
    def _():
        buf_ref[0:CONV_HIST, :] = jnp.zeros((CONV_HIST, WIDTH), F32)

    hn = _rms(x_ref[...], g_ref[...]).astype(BF)
    ab = _dot(hn, w_ref[...])
    buf_ref[CONV_HIST:CONV_HIST + TOKEN_TILE, :] = ab[:, :WIDTH] * jax.nn.sigmoid(ab[:, WIDTH:])
    acc = jnp.zeros((TOKEN_TILE, WIDTH), F32) + db_ref[...]
    first = CONV_HIST - (CONV_KERNEL - 1)
    for res in range(min(SUBLANES, CONV_KERNEL)):
        n_taps = (CONV_KERNEL - 1 - res) // SUBLANES + 1
        rows = TOKEN_TILE + (n_taps - 1) * SUBLANES
        shifted = buf_ref[first + res:first + res + rows, :]
        for a in range(n_taps):
            k = a * SUBLANES + res
            acc = acc + dw_ref[k:k + 1, :] * shifted[a * SUBLANES:a * SUBLANES + TOKEN_TILE, :]
    o_ref[...] = _silu(_rms(acc, cn_ref[...])).astype(BF)
    buf_ref[0:CONV_HIST, :] = buf_ref[TOKEN_TILE:TOKEN_TILE + CONV_HIST, :]


def _conv_module(x3, norm, w, dw_w, dw_b, c_norm, layer):
    b, s, _ = x3.shape
    return pl.pallas_call(
        _conv_body,
        out_shape=jax.ShapeDtypeStruct((b, s, WIDTH), BF),
        grid=(b, s // TOKEN_TILE),
        in_specs=[pl.BlockSpec((None, TOKEN_TILE, D_MODEL), lambda i, j: (i, j, 0)),
                  _layer_spec(norm.shape, layer),
                  _layer_spec(w.shape, layer),
                  _layer_spec(dw_w.shape, layer),
                  _layer_spec(dw_b.shape, layer),
                  _layer_spec(c_norm.shape, layer)],
        out_specs=pl.BlockSpec((None, TOKEN_TILE, WIDTH), lambda i, j: (i, j, 0)),
        scratch_shapes=[pltpu.VMEM((TOKEN_TILE + CONV_HIST, WIDTH), F32)],
        compiler_params=_params(2),
        name="conv_module",
    )(x3, norm, w, dw_w, dw_b, c_norm)


def _ssd_body(x_ref, g_ref, w_ref, cw_ref, cb_ref, dtb_ref, a_ref, dsk_ref, ng_ref, o_ref,
              buf_ref, st_ref):
    hw = SSM_REP * SSM_HEAD_DIM

    @pl.when(pl.program_id(1) == 0)
    def _():
        buf_ref[0:SSM_HIST, :] = jnp.zeros((SSM_HIST, SSM_CONV_DIM), F32)
        st_ref[...] = jnp.zeros(st_ref.shape, F32)

    hn = _rms(x_ref[...], g_ref[...]).astype(BF)
    proj = _dot(hn, w_ref[...])
    z = proj[:, :WIDTH]
    buf_ref[SSM_HIST:SSM_HIST + TOKEN_TILE, :] = proj[:, WIDTH:WIDTH + SSM_CONV_DIM]
    dt = jax.nn.softplus(proj[:, WIDTH + SSM_CONV_DIM:] + dtb_ref[...])
    xbc = jnp.zeros((TOKEN_TILE, SSM_CONV_DIM), F32) + cb_ref[...]
    first = SSM_HIST - (SSM_CONV - 1)
    for k in range(SSM_CONV):
        xbc = xbc + cw_ref[k:k + 1, :] * buf_ref[first + k:first + k + TOKEN_TILE, :]
    buf_ref[0:SSM_HIST, :] = buf_ref[TOKEN_TILE:TOKEN_TILE + SSM_HIST, :]
    xbc = _silu(xbc)
    xs = xbc[:, :WIDTH]
    a_all = dt * a_ref[...]
    xdt = xs * dt

    row = lax.broadcasted_iota(jnp.int32, (CHUNK, CHUNK), 0)
    col = lax.broadcasted_iota(jnp.int32, (CHUNK, CHUNK), 1)
    causal = row >= col
    tril = jnp.where(causal, 1.0, 0.0).astype(BF)
    lane = lax.broadcasted_iota(jnp.int32, (CHUNK, 2 * SSM_HEAD_DIM), 1)
    low_half = lane < SSM_HEAD_DIM

    for c in range(TOKEN_TILE // CHUNK):
        rows = slice(c * CHUNK, (c + 1) * CHUNK)
        hi, mid, lo = _split3(a_all[rows, :])
        cs = _dot(tril, hi) + _dot(tril, mid) + _dot(tril, lo)
        cs_t = cs.T
        cs_last = cs[CHUNK - 1:CHUNK, :]
        xc = xdt[rows, :]
        x_decay = (xc * jnp.exp(cs_last - cs)).astype(BF)
        xc_bf = xc.astype(BF)
        grow = jnp.exp(cs)
        y_parts = []
        for g in range(SSM_GROUPS):
            bg = xbc[rows, WIDTH + g * SSM_STATE:WIDTH + (g + 1) * SSM_STATE].astype(BF)
            cg = xbc[rows, WIDTH + (SSM_GROUPS + g) * SSM_STATE:
                     WIDTH + (SSM_GROUPS + g + 1) * SSM_STATE].astype(BF)
            cb = _dot_nt(cg, bg)
            gl = slice(g * hw, (g + 1) * hw)
            y_off = _dot(cg, st_ref[g].astype(BF)) * grow[:, gl]
            y_diag = []
            for pair in range(SSM_REP // 2):
                scores = []
                for hh in range(2):
                    hc = g * hw + (2 * pair + hh) * SSM_HEAD_DIM
                    seg = cs[:, hc:hc + 1] - cs_t[hc:hc + 1, :]
                    decay = jnp.exp(jnp.where(causal, seg, NEG))
                    scores.append((cb * decay).astype(BF))
                xp = xc_bf[:, g * hw + pair * 2 * SSM_HEAD_DIM:g * hw + (pair + 1) * 2 * SSM_HEAD_DIM]
                zero = jnp.zeros_like(xp)
                rhs = jnp.concatenate([jnp.where(low_half, xp, zero),
                                       jnp.where(low_half, zero, xp)], axis=0)
                y_diag.append(_dot(jnp.concatenate(scores, axis=1), rhs))
            y_parts.append(jnp.concatenate(y_diag, axis=1) + y_off)
            st_ref[g] = jnp.exp(cs_last[:, gl]) * st_ref[g] + _dot_tn(bg, x_decay[:, gl])
        y = jnp.concatenate(y_parts, axis=1) + xs[rows, :] * dsk_ref[...]
        y = y * _silu(z[rows, :])
        o_ref[rows, :] = _rms(y, ng_ref[...]).astype(BF)


def _ssd(x3, norm, w, conv_w, conv_b, dt_bias, a_neg, d_skip, n_gain, layer):
    b, s, _ = x3.shape
    ins = (norm, w, conv_w, conv_b, dt_bias, a_neg, d_skip, n_gain)
    return pl.pallas_call(
        _ssd_body,
        out_shape=jax.ShapeDtypeStruct((b, s, WIDTH), BF),
        grid=(b, s // TOKEN_TILE),
        in_specs=[pl.BlockSpec((None, TOKEN_TILE, D_MODEL), lambda i, j: (i, j, 0))]
        + [_layer_spec(a.shape, layer) for a in ins],
        out_specs=pl.BlockSpec((None, TOKEN_TILE, WIDTH), lambda i, j: (i, j, 0)),
        scratch_shapes=[pltpu.VMEM((TOKEN_TILE + SSM_HIST, SSM_CONV_DIM), F32),
                        pltpu.VMEM((SSM_GROUPS, SSM_STATE, SSM_REP * SSM_HEAD_DIM), F32)],
        compiler_params=_params(2),
        name="ssd",
    )(x3, *ins)


def _head_rms(x, ones_blockdiag, gain, scale):
    hi, lo = _split2(x * x)
    ss = _dot(hi, ones_blockdiag) + _dot(lo, ones_blockdiag)
    return x * lax.rsqrt(ss * (1.0 / HEAD_DIM) + EPS) * (gain * scale)


def _nsa_proj_body(x_ref, g_ref, w_ref, qn_ref, kn_ref, bdq_ref, bdk_ref,
                   q_ref, kc_ref, vc_ref, ks_ref, vs_ref, kw_ref, vw_ref, gate_ref):
    hn = _rms(x_ref[...], g_ref[...]).astype(BF)
    proj = _dot(hn, w_ref[...])
    q_ref[...] = _head_rms(proj[:, :WIDTH], bdq_ref[...], qn_ref[...], HEAD_DIM ** -0.5).astype(BF)
    o = WIDTH
    kc_ref[...] = proj[:, o:o + KV_WIDTH].astype(BF)
    vc_ref[...] = proj[:, o + KV_WIDTH:o + 2 * KV_WIDTH].astype(BF)
    ks_ref[...] = _head_rms(proj[:, o + 2 * KV_WIDTH:o + 3 * KV_WIDTH], bdk_ref[...], kn_ref[...], 1.0).astype(BF)
    vs_ref[...] = proj[:, o + 3 * KV_WIDTH:o + 4 * KV_WIDTH].astype(BF)
    kw_ref[...] = _head_rms(proj[:, o + 4 * KV_WIDTH:o + 5 * KV_WIDTH], bdk_ref[...], kn_ref[...], 1.0).astype(BF)
    vw_ref[...] = proj[:, o + 5 * KV_WIDTH:o + 6 * KV_WIDTH].astype(BF)
    gate_ref[...] = jax.nn.sigmoid(proj[:, o + 6 * KV_WIDTH:])


def _nsa_proj(x3, norm, w, q_gain, k_gain, bd_q, bd_k, layer):
    b, s, _ = x3.shape
    tile = lambda n: pl.BlockSpec((None, TOKEN_TILE, n), lambda i, j: (i, j, 0))
    const = lambda a: pl.BlockSpec(a.shape, lambda i, j: (0,) * a.ndim)
    kv = jax.ShapeDtypeStruct((b, s, KV_WIDTH), BF)
    return pl.pallas_call(
        _nsa_proj_body,
        out_shape=(jax.ShapeDtypeStruct((b, s, WIDTH), BF), kv, kv, kv, kv, kv, kv,
                   jax.ShapeDtypeStruct((b, s, LANES), F32)),
        grid=(b, s // TOKEN_TILE),
        in_specs=[tile(D_MODEL), _layer_spec(norm.shape, layer), _layer_spec(w.shape, layer),
                  _layer_spec(q_gain.shape, layer), _layer_spec(k_gain.shape, layer),
                  const(bd_q), const(bd_k)],
        out_specs=(tile(WIDTH),) + (tile(KV_WIDTH),) * 6 + (tile(LANES),),
        compiler_params=_params(2),
        name="nsa_proj",
    )(x3, norm, w, q_gain, k_gain, bd_q, bd_k)


def _nsa_attn_body(qt_ref, gate_ref, kc_ref, vc_ref, ks_ref, vst_ref, kw_ref, vwt_ref,
                   w1k_ref, w1kf_ref, w2k_ref, pek_ref, w1v_ref, w1vf_ref, w2v_ref, pev_ref, kn_ref,
                   cpos_ref, ovt_ref,
                   o_ref, qs_ref, kcmp_ref, vcmpt_ref, m_ref, acc_ref, part_ref, s_ref):
    j = pl.program_id(1)
    t0 = j * CHUNK
    nq = NSA_REP * CHUNK
    n_sel = ovt_ref.shape[0]
    groups = range(NSA_GROUPS)

    def compress(x, w1_ref, g, pe_bias, w2_ref):
        hidden = _dot(x, w1_ref[g, 0]) + pltpu.roll(_dot(x, w1_ref[g, 1]), CHUNK - 1, 0) + pe_bias
        return _dot(_gelu(hidden).astype(BF), w2_ref[...])

    @pl.when(j == 0)
    def _():
        kc, vc = kc_ref[...], vc_ref[...]
        pe_bias_k = _dot(pek_ref[...], w1kf_ref[...])[0:1, :]
        pe_bias_v = _dot(pev_ref[...], w1vf_ref[...])[0:1, :]
        lane = lax.broadcasted_iota(jnp.int32, (CHUNK, LANES - HEAD_DIM), 1)
        ones_col = jnp.where(lane == 0, 1.0, 0.0)
        for g in groups:
            k_cmp = _rms(compress(kc, w1k_ref, g, pe_bias_k, w2k_ref), kn_ref[...])
            kcmp_ref[g] = jnp.concatenate([k_cmp.astype(BF), cpos_ref[...]], axis=1)
            v_pad = jnp.concatenate([compress(vc, w1v_ref, g, pe_bias_v, w2v_ref), ones_col], axis=1)
            vcmpt_ref[g] = v_pad.T[0:V_ROWS, :].astype(BF)

    def q_index(rows):
        return lax.broadcasted_iota(jnp.int32, (rows, nq), 1) % CHUNK

    def k_index(rows):
        return lax.broadcasted_iota(jnp.int32, (rows, nq), 0)

    def select_blocks(p_c):
        p_sum = p_c[:, 0:CHUNK]
        for r in range(1, NSA_REP):
            p_sum = p_sum + p_c[:, r * CHUNK:(r + 1) * CHUNK]
        p_hi, p_lo = _split2(p_sum)
        imp = _dot(ovt_ref[...], p_hi) + _dot(ovt_ref[...], p_lo)
        blk = lax.broadcasted_iota(jnp.int32, (n_sel, CHUNK), 0)
        tq = t0 + lax.broadcasted_iota(jnp.int32, (n_sel, CHUNK), 1)
        cur = tq // SEL_BLOCK
        forced = (blk == 0) | (blk == cur) | (blk == cur - 1)
        score = jnp.where(forced, FORCE, jnp.where(blk * SEL_BLOCK <= tq, imp, NEG))
        blk_f = blk.astype(F32)
        chosen = jnp.zeros((n_sel, CHUNK), F32)
        for _ in range(SEL_TOP_N):
            best = jnp.max(score, axis=0, keepdims=True)
            first = jnp.min(jnp.where(score == best, blk_f, 1e9), axis=0, keepdims=True)
            pick = blk_f == first
            chosen = jnp.where(pick, 1.0, chosen)
            score = jnp.where(pick, -3e38, score)
        return ((chosen - 1.0) * (-NEG)).astype(BF)

    n_w = WINDOW + CHUNK
    w_blk = jnp.maximum(j - WINDOW // CHUNK, 0)
    w_start = pl.multiple_of(w_blk * CHUNK, CHUNK)
    sub_blocks = SEL_KEYS // CHUNK
    n_full = j // sub_blocks

    for g in groups:
        qt = qt_ref[g]
        gate = gate_ref[g]
        ci = k_index(CHUNK)
        valid_c = (ci * CMP_STRIDE + (CMP_BLOCK - 1) <= t0 + q_index(CHUNK)) & (ci < CHUNK - 1)
        s_c = jnp.where(valid_c, _dot(kcmp_ref[g], qt), NEG)
        e_c = jnp.where(valid_c, jnp.exp(s_c - jnp.max(s_c, axis=0, keepdims=True)), 0.0)
        p_c = e_c / jnp.maximum(jnp.sum(e_c, axis=0, keepdims=True), 1e-30)
        o_c = _dot(vcmpt_ref[g], p_c.astype(BF))[0:HEAD_DIM]
        qs_ref[g] = qt
        qs_ref[g, HEAD_DIM:HEAD_DIM + n_sel, :] = jnp.concatenate([select_blocks(p_c)] * NSA_REP, axis=1)
        rel = (t0 - w_start) + q_index(n_w) - k_index(n_w)
        s_w = jnp.where((rel >= 0) & (rel < WINDOW), _dot(kw_ref[g, pl.ds(w_start, n_w), :], qt), NEG)
        p_w = jnp.exp(s_w - jnp.max(s_w, axis=0, keepdims=True)).astype(BF)
        o_w = _dot(vwt_ref[g, w_blk], p_w[0:CHUNK])
        for sub in range(1, n_w // CHUNK):
            o_w = o_w + _dot(vwt_ref[g, w_blk + sub], p_w[sub * CHUNK:(sub + 1) * CHUNK])
        part_ref[g] = gate[0:1] * o_c + gate[2:3] * (o_w[0:HEAD_DIM] / o_w[HEAD_DIM:HEAD_DIM + 1])

    m_ref[...] = jnp.full(m_ref.shape, NEG, F32)
    acc_ref[...] = jnp.zeros(acc_ref.shape, F32)

    def scores(g, blk0):
        start = pl.multiple_of(blk0 * CHUNK, SEL_KEYS)
        return _dot(ks_ref[g, pl.ds(start, SEL_KEYS), :], qs_ref[g])

    def absorb(g, s, blk0):
        m_old = m_ref[g]
        m_new = jnp.maximum(m_old, jnp.max(s, axis=0, keepdims=True))
        p = jnp.exp(s - m_new).astype(BF)
        pv = _dot(vst_ref[g, blk0], p[0:CHUNK])
        for sub in range(1, sub_blocks):
            pv = pv + _dot(vst_ref[g, blk0 + sub], p[sub * CHUNK:(sub + 1) * CHUNK])
        acc_ref[g] = jnp.exp(m_old - m_new) * acc_ref[g] + pv
        m_ref[g] = m_new

    for g in groups:
        s_ref[g] = scores(g, 0)

    def loop_body(i, carry):
        for g in groups:
            s = s_ref[g]
            s_ref[g] = scores(g, (i + 1) * sub_blocks)
            absorb(g, s, i * sub_blocks)
        return carry

    lax.fori_loop(0, n_full, loop_body, 0)
    for g in groups:
        last = n_full * sub_blocks
        causal = last * CHUNK + k_index(SEL_KEYS) <= t0 + q_index(SEL_KEYS)
        absorb(g, jnp.where(causal, s_ref[g], NEG), last)
        acc = acc_ref[g]
        o_s = acc[0:HEAD_DIM] / acc[HEAD_DIM:HEAD_DIM + 1]
        o_ref[g] = (part_ref[g] + gate_ref[g][1:2] * o_s).astype(BF)


def _nsa_attn(qt_aug, gates_t, kc2, vc2, ks_aug, vs_t, kw_aug, vw_t,
              w1k, w1kf, w2k, pek, w1v, w1vf, w2v, pev, k_gain, cpos, ov_t, layer):
    b, g, nblk, _, nq = qt_aug.shape
    blk = lambda a: pl.BlockSpec((None, g, None) + a.shape[3:], lambda i, j: (i, 0, j, 0, 0))
    seq = lambda a: pl.BlockSpec((None,) + a.shape[1:], lambda i, j: (i,) + (0,) * (a.ndim - 1))
    lay = lambda a: pl.BlockSpec((None,) + a.shape[1:], lambda i, j: (layer,) + (0,) * (a.ndim - 1))
    const = lambda a: pl.BlockSpec(a.shape, lambda i, j: (0,) * a.ndim)
    return pl.pallas_call(
        _nsa_attn_body,
        out_shape=jax.ShapeDtypeStruct((b, g, nblk, HEAD_DIM, nq), BF),
        grid=(b, nblk),
        in_specs=[blk(qt_aug), blk(gates_t), seq(kc2), seq(vc2),
                  seq(ks_aug), seq(vs_t), seq(kw_aug), seq(vw_t),
                  lay(w1k), lay(w1kf), lay(w2k), lay(pek),
                  lay(w1v), lay(w1vf), lay(w2v), lay(pev), lay(k_gain),
                  const(cpos), const(ov_t)],
        out_specs=pl.BlockSpec((None, g, None, HEAD_DIM, nq), lambda i, j: (i, 0, j, 0, 0)),
        scratch_shapes=[pltpu.VMEM((g, LANES, nq), BF),
                        pltpu.VMEM((g, CHUNK, LANES), BF),
                        pltpu.VMEM((g, V_ROWS, CHUNK), BF),
                        pltpu.VMEM((g, 1, nq), F32),
                        pltpu.VMEM((g, V_ROWS, nq), F32),
                        pltpu.VMEM((g, HEAD_DIM, nq), F32),
                        pltpu.VMEM((g, SEL_KEYS, nq), F32)],
        compiler_params=_params(2),
        name="nsa_attn",
    )(qt_aug, gates_t, kc2, vc2, ks_aug, vs_t, kw_aug, vw_t,
      w1k, w1kf, w2k, pek, w1v, w1vf, w2v, pev, k_gain, cpos, ov_t)


def _merge_body(x_ref, ya_ref, yb_ref, yc_ref, yd_ref, g_ref, wg_ref, wb_ref, wo_ref, o_ref):
    x = x_ref[...]
    hn = _rms(x, g_ref[...]).astype(BF)
    merged = jnp.zeros_like(x)
    for i, y_ref in enumerate((ya_ref, yb_ref, yc_ref, yd_ref)):
        gate = jax.nn.sigmoid(_dot(hn, wg_ref[:, i * D_MODEL:(i + 1) * D_MODEL]))
        merged = merged + gate * _dot(y_ref[...], wb_ref[i])
    o_ref[...] = x + _dot(merged.astype(BF), wo_ref[...])


def _merge(x2, ya, yb, yc, yd, norm, w_gate, w_branch, w_out, layer):
    t = x2.shape[0]
    tile = lambda n: pl.BlockSpec((TOKEN_TILE, n), lambda i: (i, 0))
    return pl.pallas_call(
        _merge_body,
        out_shape=jax.ShapeDtypeStruct(x2.shape, F32),
        grid=(t // TOKEN_TILE,),
        in_specs=[tile(D_MODEL)] + [tile(WIDTH)] * 4
        + [_layer_spec(a.shape, layer) for a in (norm, w_gate, w_branch, w_out)],
        out_specs=tile(D_MODEL),
        compiler_params=_params(1),
        name="merge",
    )(x2, ya, yb, yc, yd, norm, w_gate, w_branch, w_out)


def _alibi_slopes():
    return np.array([2.0 ** (-8.0 * (i + 1) / NSA_HEADS) for i in range(NSA_HEADS)], np.float32)


def _nsa_constants(seq):
    n_sel = seq // SEL_BLOCK
    n_cmp = (seq - CMP_BLOCK) // CMP_STRIDE + 1
    assert n_cmp == CHUNK - 1 and n_sel <= SEL_BLOCK // 2
    pos_cols = LANES - HEAD_DIM - n_sel
    slopes = _alibi_slopes().reshape(NSA_GROUPS, NSA_REP)
    q_const = np.zeros((NSA_GROUPS, LANES - HEAD_DIM, NSA_REP * CHUNK), np.float32)
    lane_slope = np.repeat(slopes, CHUNK, axis=1)
    q_const[:, n_sel, :] = lane_slope * SEL_BLOCK
    q_const[:, n_sel + 1, :] = lane_slope
    pos = np.arange(seq)
    pos_part = np.zeros((seq, pos_cols), np.float32)
    pos_part[:, 0] = pos // SEL_BLOCK
    pos_part[:, 1] = pos % SEL_BLOCK
    onehot = (pos[:, None] // SEL_BLOCK == np.arange(n_sel)[None, :]).astype(np.float32)
    k_sel_const = np.concatenate([onehot, pos_part], axis=1)
    k_win_const = np.concatenate([np.zeros_like(onehot), pos_part], axis=1)
    v_const = np.zeros((V_ROWS - HEAD_DIM, CHUNK), np.float32)
    v_const[0, :] = 1.0
    c_mid = np.arange(CHUNK) * CMP_STRIDE + (CMP_BLOCK - 1) / 2.0
    cmp_const = np.zeros((CHUNK, LANES - HEAD_DIM), np.float32)
    cmp_const[:, n_sel] = c_mid // SEL_BLOCK
    cmp_const[:, n_sel + 1] = c_mid % SEL_BLOCK
    c_start = np.arange(CHUNK) * CMP_STRIDE
    s_start = np.arange(n_sel) * SEL_BLOCK
    overlap_t = ((c_start[None, :] <= s_start[:, None] + SEL_BLOCK - 1)
                 & (c_start[None, :] + CMP_BLOCK - 1 >= s_start[:, None])
                 & (np.arange(CHUNK)[None, :] < n_cmp)).astype(np.float32)
    bf = lambda a: jnp.asarray(a, BF)
    return dict(q_const=bf(q_const), k_sel=bf(k_sel_const), k_win=bf(k_win_const), v_const=bf(v_const),
                cmp_const=bf(cmp_const), overlap_t=bf(overlap_t))


def _block_diag_ones(n):
    idx = np.arange(n) // HEAD_DIM
    return jnp.asarray(idx[:, None] == idx[None, :], BF)


def _compress_weights(w1):
    nl = w1.shape[0]
    halves = w1.reshape(nl, 2, CMP_STRIDE, 1, HEAD_DIM, CMP_HIDDEN)
    group_sel = jnp.eye(NSA_GROUPS, dtype=w1.dtype)
    big = halves[:, None] * group_sel[None, :, None, None, :, None, None]
    return big.reshape(nl, NSA_GROUPS, 2, CMP_STRIDE * KV_WIDTH, CMP_HIDDEN).astype(BF)


def _pe_rows(pe):
    nl = pe.shape[0]
    flat = pe.reshape(nl, 1, CMP_BLOCK * HEAD_DIM)
    return jnp.concatenate([flat, jnp.zeros((nl, 7, CMP_BLOCK * HEAD_DIM), pe.dtype)], axis=1).astype(BF)


def _nsa_params(seq, q_norm, k_norm, pe_k, w1_k, w2_k, pe_v, w1_v, w2_v):
    row = lambda a: a[:, None, :]
    p = _nsa_constants(seq)
    p.update(q_gain=row(jnp.tile(q_norm, (1, NSA_HEADS))), k_gain2=row(jnp.tile(k_norm, (1, NSA_GROUPS))),
             k_gain=row(k_norm), w1k=_compress_weights(w1_k), w1v=_compress_weights(w1_v),
             w1kf=w1_k.astype(BF), w1vf=w1_v.astype(BF), w2k=w2_k.astype(BF), w2v=w2_v.astype(BF),
             pek=_pe_rows(pe_k), pev=_pe_rows(pe_v),
             bd_q=_block_diag_ones(WIDTH), bd_k=_block_diag_ones(KV_WIDTH))
    return p


def _nsa_mixer(x3, mxn, w_c, p, layer):
    bsz, seq, _ = x3.shape
    n_blk = seq // CHUNK
    qn, kc, vc, ksn, vs, kwn, vw, gate = _nsa_proj(x3, mxn, w_c, p["q_gain"], p["k_gain2"], p["bd_q"], p["bd_k"], layer)

    def keys_aug(t, const):
        t = jnp.swapaxes(t.reshape(bsz, seq, NSA_GROUPS, HEAD_DIM), 1, 2)
        return jnp.concatenate([t, jnp.broadcast_to(const, (bsz, NSA_GROUPS) + const.shape)], axis=-1)

    def values_t(t):
        t = t.reshape(bsz, n_blk, CHUNK, NSA_GROUPS, HEAD_DIM).transpose(0, 3, 1, 4, 2)
        return jnp.concatenate([t, jnp.broadcast_to(p["v_const"], t.shape[:3] + p["v_const"].shape)], axis=-2)

    qt = qn.reshape(bsz, n_blk, CHUNK, NSA_GROUPS, NSA_REP, HEAD_DIM).transpose(0, 3, 1, 5, 4, 2)
    qt = qt.reshape(bsz, NSA_GROUPS, n_blk, HEAD_DIM, NSA_REP * CHUNK)
    qt_aug = jnp.concatenate(
        [qt, jnp.broadcast_to(p["q_const"][None, :, None], qt.shape[:3] + p["q_const"].shape[1:])], axis=-2)
    gt = gate[:, :, :3 * NSA_HEADS].reshape(bsz, n_blk, CHUNK, NSA_GROUPS, NSA_REP, 3)
    gt = gt.transpose(0, 3, 1, 5, 4, 2).reshape(bsz, NSA_GROUPS, n_blk, 3, NSA_REP * CHUNK)
    gt = jnp.concatenate([gt, jnp.zeros(gt.shape[:3] + (5, NSA_REP * CHUNK), gt.dtype)], axis=-2)
    y_t = _nsa_attn(qt_aug, gt,
                    kc.reshape(bsz, seq // CMP_STRIDE, CMP_STRIDE * KV_WIDTH),
                    vc.reshape(bsz, seq // CMP_STRIDE, CMP_STRIDE * KV_WIDTH),
                    keys_aug(ksn, p["k_sel"]), values_t(vs), keys_aug(kwn, p["k_win"]), values_t(vw),
                    p["w1k"], p["w1kf"], p["w2k"], p["pek"], p["w1v"], p["w1vf"], p["w2v"], p["pev"], p["k_gain"],
                    p["cmp_const"], p["overlap_t"], layer)
    y_c = y_t.reshape(bsz, NSA_GROUPS, n_blk, HEAD_DIM, NSA_REP, CHUNK).transpose(0, 2, 5, 1, 4, 3)
    return y_c.reshape(bsz, seq, WIDTH)


def kernel(x, ffn1_norm, ffn1_w_in, ffn1_w_out, mix_norm, w_in, sgu_v_norm, sgu_w, sgu_b, ssm_conv_w, ssm_conv_b, ssm_dt_bias, ssm_a_log, ssm_d, ssm_norm, nsa_q_norm, nsa_k_norm, nsa_pe_k, nsa_w1_k, nsa_w2_k, nsa_pe_v, nsa_w1_v, nsa_w2_v, conv_dw_w, conv_dw_b, conv_norm, w_branch, w_out, ffn2_norm, ffn2_w_in, ffn2_w_out):
    bsz, seq, _ = x.shape
    depth = w_in.shape[0]
    n_blk = seq // CHUNK
    row = lambda a: a[:, None, :]
    rep = lambda a, n: jnp.repeat(a, n, axis=-1)

    f1n, f2n, mxn = row(ffn1_norm), row(ffn2_norm), row(mix_norm)
    f1_in, f1_out = ffn1_w_in.astype(BF), ffn1_w_out.astype(BF)
    f2_in, f2_out = ffn2_w_in.astype(BF), ffn2_w_out.astype(BF)
    w_a = w_in[:, :, OFF_UV:OFF_Z].astype(BF)
    w_b = jnp.concatenate([w_in[:, :, OFF_Z:OFF_DT], rep(w_in[:, :, OFF_DT:OFF_Q], SSM_HEAD_DIM)], axis=-1).astype(BF)
    w_c = jnp.concatenate([w_in[:, :, OFF_Q:OFF_AB],
                           jnp.zeros((depth, D_MODEL, LANES - 3 * NSA_HEADS), w_in.dtype)], axis=-1).astype(BF)
    w_d = w_in[:, :, OFF_AB:OFF_MERGE].astype(BF)
    w_g = w_in[:, :, OFF_MERGE:].astype(BF)
    sgu_bias = rep(jnp.swapaxes(sgu_b, 1, 2), WIDTH // SGU_GROUPS)
    dt_bias = row(rep(ssm_dt_bias, SSM_HEAD_DIM))
    a_neg = row(rep(-jnp.exp(ssm_a_log), SSM_HEAD_DIM))
    d_skip = row(rep(ssm_d, SSM_HEAD_DIM))
    wb, wo = w_branch.astype(BF), w_out.astype(BF)
    nsa_p = _nsa_params(seq, nsa_q_norm, nsa_k_norm, nsa_pe_k, nsa_w1_k, nsa_w2_k, nsa_pe_v, nsa_w1_v, nsa_w2_v)

    x2 = x.reshape(bsz * seq, D_MODEL)
    for l in range(depth):
        x2 = _ffn(x2, f1n, f1_in, f1_out, l)
        x3 = x2.reshape(bsz, seq, D_MODEL)
        y_a = _sgu(x3, mxn, w_a, row(sgu_v_norm), sgu_w, sgu_bias, l)
        y_b = _ssd(x3, mxn, w_b, ssm_conv_w, row(ssm_conv_b), dt_bias, a_neg, d_skip, row(ssm_norm), l)
        y_d = _conv_module(x3, mxn, w_d, conv_dw_w, row(conv_dw_b), row(conv_norm), l)
        y_c = _nsa_mixer(x3, mxn, w_c, nsa_p, l)
        flat = lambda y: y.reshape(bsz * seq, WIDTH)
        x2 = _merge(x2, flat(y_a), flat(y_b), flat(y_c), flat(y_d), mxn, w_g, wb, wo, l)
        x2 = _ffn(x2, f2n, f2_in, f2_out, l)
    return x2.reshape(bsz, seq, D_MODEL)
```

```python
import functools
import math

import numpy as np
import jax
import jax.numpy as jnp
from jax import lax
from jax.experimental import pallas as pl
from jax.experimental.pallas import tpu as pltpu

F32 = jnp.float32
BF = jnp.bfloat16

D_MODEL = 1024
FFN_DIM = 2816
WIDTH = 512
N_BRANCH = 4
SGU_GROUPS = 4
CHUNK = 128
SSM_HEADS = 8
SSM_HEAD_DIM = 64
SSM_GROUPS = 2
SSM_STATE = 128
SSM_CONV = 4
SSM_CONV_DIM = WIDTH + 2 * SSM_GROUPS * SSM_STATE
SSM_REP = SSM_HEADS // SSM_GROUPS
NSA_HEADS = 8
NSA_GROUPS = 2
NSA_REP = NSA_HEADS // NSA_GROUPS
HEAD_DIM = 64
KV_WIDTH = NSA_GROUPS * HEAD_DIM
CMP_BLOCK = 32
CMP_STRIDE = 16
CMP_HIDDEN = 128
SEL_BLOCK = 64
SEL_TOP_N = 8
WINDOW = 256
CONV_KERNEL = 31
EPS = 1e-6
NEG = -1e30
FORCE = 1e9

OFF_UV = 0
OFF_Z = OFF_UV + 2 * WIDTH
OFF_XBC = OFF_Z + WIDTH
OFF_DT = OFF_XBC + SSM_CONV_DIM
OFF_Q = OFF_DT + SSM_HEADS
OFF_KV = OFF_Q + WIDTH
OFF_GATE = OFF_KV + 6 * KV_WIDTH
OFF_AB = OFF_GATE + 3 * NSA_HEADS
OFF_MERGE = OFF_AB + 2 * WIDTH
IN_PROJ_DIM = OFF_MERGE + N_BRANCH * D_MODEL

LANES = 128
SUBLANES = 8
VMEM_LIMIT_BYTES = 56 * 1024 * 1024

TOKEN_TILE = 256
CONV_HIST = 32
SSM_HIST = 8
SEL_KEYS = 256
V_ROWS = 80
GATE_ROWS = 16


def _params(n_axes):
    return pltpu.CompilerParams(dimension_semantics=("arbitrary",) * n_axes,
                                vmem_limit_bytes=VMEM_LIMIT_BYTES)


def _rms(x, g):
    return x * lax.rsqrt(jnp.mean(x * x, axis=-1, keepdims=True) + EPS) * g


def _silu(x):
    return x * jax.nn.sigmoid(x)


def _gelu(x):
    return 0.5 * x * (1.0 + jnp.tanh(math.sqrt(2.0 / math.pi) * (x + 0.044715 * (x * x * x))))


def _dot(a, b):
    return jnp.dot(a, b, preferred_element_type=F32)


def _dot_nt(a, b):
    return lax.dot_general(a, b, (((1,), (1,)), ((), ())), preferred_element_type=F32)


def _dot_tn(a, b):
    return lax.dot_general(a, b, (((0,), (0,)), ((), ())), preferred_element_type=F32)


def _split3(x):
    hi = x.astype(BF)
    r = x - hi.astype(F32)
    mid = r.astype(BF)
    lo = (r - mid.astype(F32)).astype(BF)
    return hi, mid, lo


def _split2(x):
    hi = x.astype(BF)
    lo = (x - hi.astype(F32)).astype(BF)
    return hi, lo


def _layer_spec(shape, layer):
    nd = len(shape)
    return pl.BlockSpec((None,) + tuple(shape[1:]), lambda *_: (layer,) + (0,) * (nd - 1))


def _ffn_body(x_ref, g_ref, win_ref, wout_ref, o_ref, *, n_chunks):
    x = x_ref[...]
    hn = _rms(x, g_ref[...]).astype(BF)
    tf = FFN_DIM // n_chunks
    acc = jnp.zeros_like(x)
    for c in range(n_chunks):
        gate = _dot(hn, win_ref[:, c * tf:(c + 1) * tf])
        up = _dot(hn, win_ref[:, FFN_DIM + c * tf:FFN_DIM + (c + 1) * tf])
        a = (_silu(gate) * up).astype(BF)
        acc = acc + _dot(a, wout_ref[c * tf:(c + 1) * tf, :])
    o_ref[...] = x + 0.5 * acc


def _ffn(x2, norm, w_in, w_out, layer):
    t = x2.shape[0]
    return pl.pallas_call(
        functools.partial(_ffn_body, n_chunks=2),
        out_shape=jax.ShapeDtypeStruct(x2.shape, F32),
        grid=(t // TOKEN_TILE,),
        in_specs=[pl.BlockSpec((TOKEN_TILE, D_MODEL), lambda i: (i, 0)),
                  _layer_spec(norm.shape, layer),
                  _layer_spec(w_in.shape, layer),
                  _layer_spec(w_out.shape, layer)],
        out_specs=pl.BlockSpec((TOKEN_TILE, D_MODEL), lambda i: (i, 0)),
        compiler_params=_params(1),
        name="ffn",
    )(x2, norm, w_in, w_out)


def _sgu_body(x_ref, g_ref, w_ref, vn_ref, ws_ref, bs_ref, o_ref):
    hn = _rms(x_ref[...], g_ref[...]).astype(BF)
    uv = _gelu(_dot(hn, w_ref[...]))
    u = uv[:, :WIDTH]
    v = _rms(uv[:, WIDTH:], vn_ref[...]).astype(BF)
    row = lax.broadcasted_iota(jnp.int32, (CHUNK, CHUNK), 0)
    col = lax.broadcasted_iota(jnp.int32, (CHUNK, CHUNK), 1)
    gw = WIDTH // SGU_GROUPS
    ws = [jnp.where(row >= col, ws_ref[g], 0.0).astype(BF) for g in range(SGU_GROUPS)]
    bias = bs_ref[...]
    for c in range(TOKEN_TILE // CHUNK):
        rows = slice(c * CHUNK, (c + 1) * CHUNK)
        mixed = jnp.concatenate(
            [_dot(ws[g], v[rows, g * gw:(g + 1) * gw]) for g in range(SGU_GROUPS)], axis=1)
        o_ref[rows, :] = (u[rows, :] * (mixed + bias)).astype(BF)


def _sgu(x3, norm, w, v_norm, w_s, b_full, layer):
    b, s, _ = x3.shape
    return pl.pallas_call(
        _sgu_body,
        out_shape=jax.ShapeDtypeStruct((b, s, WIDTH), BF),
        grid=(b, s // TOKEN_TILE),
        in_specs=[pl.BlockSpec((None, TOKEN_TILE, D_MODEL), lambda i, j: (i, j, 0)),
                  _layer_spec(norm.shape, layer),
                  _layer_spec(w.shape, layer),
                  _layer_spec(v_norm.shape, layer),
                  _layer_spec(w_s.shape, layer),
                  _layer_spec(b_full.shape, layer)],
        out_specs=pl.BlockSpec((None, TOKEN_TILE, WIDTH), lambda i, j: (i, j, 0)),
        compiler_params=_params(2),
        name="sgu",
    )(x3, norm, w, v_norm, w_s, b_full)


def _conv_body(x_ref, g_ref, w_ref, dw_ref, db_ref, cn_ref, o_ref, buf_ref, shift_ref):
    @pl.when(pl.program_id(1) == 0)
    def _():
        buf_ref[0:CONV_HIST, :] = jnp.zeros((CONV_HIST, WIDTH), F32)

    hn = _rms(x_ref[...], g_ref[...]).astype(BF)
    ab = _dot(hn, w_ref[...])
    buf_ref[CONV_HIST:CONV_HIST + TOKEN_TILE, :] = ab[:, :WIDTH] * jax.nn.sigmoid(ab[:, WIDTH:])
    acc = jnp.zeros((TOKEN_TILE, WIDTH), F32) + db_ref[...]
    first = CONV_HIST - (CONV_KERNEL - 1)
    for res in range(min(SUBLANES, CONV_KERNEL)):
        n_taps = (CONV_KERNEL - 1 - res) // SUBLANES + 1
        rows = TOKEN_TILE + (n_taps - 1) * SUBLANES
        shift_ref[0:rows, :] = buf_ref[first + res:first + res + rows, :]
        for a in range(n_taps):
            k = a * SUBLANES + res
            acc = acc + dw_ref[k:k + 1, :] * shift_ref[a * SUBLANES:a * SUBLANES + TOKEN_TILE, :]
    o_ref[...] = _silu(_rms(acc, cn_ref[...])).astype(BF)
    buf_ref[0:CONV_HIST, :] = buf_ref[TOKEN_TILE:TOKEN_TILE + CONV_HIST, :]


def _conv_module(x3, norm, w, dw_w, dw_b, c_norm, layer):
    b, s, _ = x3.shape
    return pl.pallas_call(
        _conv_body,
        out_shape=jax.ShapeDtypeStruct((b, s, WIDTH), BF),
        grid=(b, s // TOKEN_TILE),
        in_specs=[pl.BlockSpec((None, TOKEN_TILE, D_MODEL), lambda i, j: (i, j, 0)),
                  _layer_spec(norm.shape, layer),
                  _layer_spec(w.shape, layer),
                  _layer_spec(dw_w.shape, layer),
                  _layer_spec(dw_b.shape, layer),
                  _layer_spec(c_norm.shape, layer)],
        out_specs=pl.BlockSpec((None, TOKEN_TILE, WIDTH), lambda i, j: (i, j, 0)),
        scratch_shapes=[pltpu.VMEM((TOKEN_TILE + CONV_HIST, WIDTH), F32),
                        pltpu.VMEM((TOKEN_TILE + CONV_HIST, WIDTH), F32)],
        compiler_params=_params(2),
        name="conv_module",
    )(x3, norm, w, dw_w, dw_b, c_norm)


def _ssd_body(x_ref, g_ref, w_ref, cw_ref, cb_ref, dtb_ref, a_ref, dsk_ref, ng_ref, o_ref,
              buf_ref, st_ref):
    hw = SSM_REP * SSM_HEAD_DIM

    @pl.when(pl.program_id(1) == 0)
    def _():
        buf_ref[0:SSM_HIST, :] = jnp.zeros((SSM_HIST, SSM_CONV_DIM), F32)
        st_ref[...] = jnp.zeros(st_ref.shape, F32)

    hn = _rms(x_ref[...], g_ref[...]).astype(BF)
    proj = _dot(hn, w_ref[...])
    z = proj[:, :WIDTH]
    buf_ref[SSM_HIST:SSM_HIST + TOKEN_TILE, :] = proj[:, WIDTH:WIDTH + SSM_CONV_DIM]
    dt = jax.nn.softplus(proj[:, WIDTH + SSM_CONV_DIM:] + dtb_ref[...])
    xbc = jnp.zeros((TOKEN_TILE, SSM_CONV_DIM), F32) + cb_ref[...]
    first = SSM_HIST - (SSM_CONV - 1)
    for k in range(SSM_CONV):
        xbc = xbc + cw_ref[k:k + 1, :] * buf_ref[first + k:first + k + TOKEN_TILE, :]
    buf_ref[0:SSM_HIST, :] = buf_ref[TOKEN_TILE:TOKEN_TILE + SSM_HIST, :]
    xbc = _silu(xbc)
    xs = xbc[:, :WIDTH]
    a_all = dt * a_ref[...]
    xdt = xs * dt

    row = lax.broadcasted_iota(jnp.int32, (CHUNK, CHUNK), 0)
    col = lax.broadcasted_iota(jnp.int32, (CHUNK, CHUNK), 1)
    causal = row >= col
    tril = jnp.where(causal, 1.0, 0.0).astype(BF)
    lane = lax.broadcasted_iota(jnp.int32, (CHUNK, 2 * SSM_HEAD_DIM), 1)
    low_half = lane < SSM_HEAD_DIM

    for c in range(TOKEN_TILE // CHUNK):
        rows = slice(c * CHUNK, (c + 1) * CHUNK)
        hi, mid, lo = _split3(a_all[rows, :])
        cs = _dot(tril, hi) + _dot(tril, mid) + _dot(tril, lo)
        cs_t = cs.T
        cs_last = cs[CHUNK - 1:CHUNK, :]
        xc = xdt[rows, :]
        x_decay = (xc * jnp.exp(cs_last - cs)).astype(BF)
        xc_bf = xc.astype(BF)
        grow = jnp.exp(cs)
        y_parts = []
        for g in range(SSM_GROUPS):
            bg = xbc[rows, WIDTH + g * SSM_STATE:WIDTH + (g + 1) * SSM_STATE].astype(BF)
            cg = xbc[rows, WIDTH + (SSM_GROUPS + g) * SSM_STATE:
                     WIDTH + (SSM_GROUPS + g + 1) * SSM_STATE].astype(BF)
            cb = _dot_nt(cg, bg)
            gl = slice(g * hw, (g + 1) * hw)
            y_off = _dot(cg, st_ref[g].astype(BF)) * grow[:, gl]
            y_diag = []
            for pair in range(SSM_REP // 2):
                scores = []
                for hh in range(2):
                    hc = g * hw + (2 * pair + hh) * SSM_HEAD_DIM
                    seg = cs[:, hc:hc + 1] - cs_t[hc:hc + 1, :]
                    decay = jnp.exp(jnp.where(causal, seg, NEG))
                    scores.append((cb * decay).astype(BF))
                xp = xc_bf[:, g * hw + pair * 2 * SSM_HEAD_DIM:g * hw + (pair + 1) * 2 * SSM_HEAD_DIM]
                zero = jnp.zeros_like(xp)
                rhs = jnp.concatenate([jnp.where(low_half, xp, zero),
                                       jnp.where(low_half, zero, xp)], axis=0)
                y_diag.append(_dot(jnp.concatenate(scores, axis=1), rhs))
            y_parts.append(jnp.concatenate(y_diag, axis=1) + y_off)
            st_ref[g] = jnp.exp(cs_last[:, gl]) * st_ref[g] + _dot_tn(bg, x_decay[:, gl])
        y = jnp.concatenate(y_parts, axis=1) + xs[rows, :] * dsk_ref[...]
        y = y * _silu(z[rows, :])
        o_ref[rows, :] = _rms(y, ng_ref[...]).astype(BF)


def _ssd(x3, norm, w, conv_w, conv_b, dt_bias, a_neg, d_skip, n_gain, layer):
    b, s, _ = x3.shape
    ins = (norm, w, conv_w, conv_b, dt_bias, a_neg, d_skip, n_gain)
    return pl.pallas_call(
        _ssd_body,
        out_shape=jax.ShapeDtypeStruct((b, s, WIDTH), BF),
        grid=(b, s // TOKEN_TILE),
        in_specs=[pl.BlockSpec((None, TOKEN_TILE, D_MODEL), lambda i, j: (i, j, 0))]
        + [_layer_spec(a.shape, layer) for a in ins],
        out_specs=pl.BlockSpec((None, TOKEN_TILE, WIDTH), lambda i, j: (i, j, 0)),
        scratch_shapes=[pltpu.VMEM((TOKEN_TILE + SSM_HIST, SSM_CONV_DIM), F32),
                        pltpu.VMEM((SSM_GROUPS, SSM_STATE, SSM_REP * SSM_HEAD_DIM), F32)],
        compiler_params=_params(2),
        name="ssd",
    )(x3, *ins)


def _head_rms(x, ones_blockdiag, gain, scale):
    hi, lo = _split2(x * x)
    ss = _dot(hi, ones_blockdiag) + _dot(lo, ones_blockdiag)
    return x * lax.rsqrt(ss * (1.0 / HEAD_DIM) + EPS) * (gain * scale)


def _nsa_proj_body(x_ref, g_ref, w_ref, qn_ref, kn_ref, bdq_ref, bdk_ref, qc_ref, ksc_ref, kwc_ref, vcn_ref,
                   qt_ref, gate_ref, kc_ref, vc_ref, ks_ref, vst_ref, kw_ref, vwt_ref):
    hn = _rms(x_ref[...], g_ref[...]).astype(BF)
    proj = _dot(hn, w_ref[...])
    o = WIDTH
    kc_ref[...] = proj[:, o:o + KV_WIDTH]
    vc_ref[...] = proj[:, o + KV_WIDTH:o + 2 * KV_WIDTH]
    q = _head_rms(proj[:, :WIDTH], bdq_ref[...], qn_ref[...], HEAD_DIM ** -0.5)
    ks = _head_rms(proj[:, o + 2 * KV_WIDTH:o + 3 * KV_WIDTH], bdk_ref[...], kn_ref[...], 1.0)
    kw = _head_rms(proj[:, o + 4 * KV_WIDTH:o + 5 * KV_WIDTH], bdk_ref[...], kn_ref[...], 1.0)
    vs = proj[:, o + 3 * KV_WIDTH:o + 4 * KV_WIDTH]
    vw = proj[:, o + 5 * KV_WIDTH:o + 6 * KV_WIDTH]
    gate_t = jax.nn.sigmoid(proj[:, o + 6 * KV_WIDTH:]).T
    lane = lax.broadcasted_iota(jnp.int32, (TOKEN_TILE, KV_WIDTH), 1)
    gw = NSA_REP * HEAD_DIM
    for g in range(NSA_GROUPS):
        own = (lane >= g * HEAD_DIM) & (lane < (g + 1) * HEAD_DIM)
        ks_ref[g] = (jnp.where(own, ks, 0.0) + ksc_ref[g]).astype(BF)
        kw_ref[g] = (jnp.where(own, kw, 0.0) + kwc_ref[g]).astype(BF)
    for c in range(TOKEN_TILE // CHUNK):
        rows = slice(c * CHUNK, (c + 1) * CHUNK)
        vs_t = vs[rows, :].T.astype(BF)
        vw_t = vw[rows, :].T.astype(BF)
        for g in range(NSA_GROUPS):
            feat = slice(g * HEAD_DIM, (g + 1) * HEAD_DIM)
            rest = slice((1 - g) * HEAD_DIM, (2 - g) * HEAD_DIM)
            q_t = q[rows, g * gw:(g + 1) * gw].T
            qt_ref[g, c, feat, :] = jnp.concatenate(
                [q_t[r * HEAD_DIM:(r + 1) * HEAD_DIM] for r in range(NSA_REP)], axis=1).astype(BF)
            qt_ref[g, c, rest, :] = qc_ref[g]
            gate_ref[g, c] = gate_t[g * GATE_ROWS:(g + 1) * GATE_ROWS, rows]
            vst_ref[g, c, 0:HEAD_DIM, :] = vs_t[feat]
            vst_ref[g, c, HEAD_DIM:V_ROWS, :] = vcn_ref[...]
            vwt_ref[g, c, 0:HEAD_DIM, :] = vw_t[feat]
            vwt_ref[g, c, HEAD_DIM:V_ROWS, :] = vcn_ref[...]


def _nsa_proj(x3, norm, w, p, layer):
    b, s, _ = x3.shape
    g, nq, blocks = NSA_GROUPS, NSA_REP * CHUNK, TOKEN_TILE // CHUNK
    n_blk = s // CHUNK
    tile = lambda n: pl.BlockSpec((None, TOKEN_TILE, n), lambda i, j: (i, j, 0))
    const = lambda a: pl.BlockSpec(a.shape, lambda i, j: (0,) * a.ndim)
    key_const = lambda a: pl.BlockSpec((g, TOKEN_TILE, LANES), lambda i, j: (0, j, 0))
    per_blk = lambda rows, lanes: pl.BlockSpec((None, g, blocks, rows, lanes), lambda i, j: (i, 0, j, 0, 0))
    keys = pl.BlockSpec((None, g, TOKEN_TILE, LANES), lambda i, j: (i, 0, j, 0))
    raw = jax.ShapeDtypeStruct((b, s, KV_WIDTH), F32)
    k_aug = jax.ShapeDtypeStruct((b, g, s, LANES), BF)
    v_t = jax.ShapeDtypeStruct((b, g, n_blk, V_ROWS, CHUNK), BF)
    return pl.pallas_call(
        _nsa_proj_body,
        out_shape=(jax.ShapeDtypeStruct((b, g, n_blk, LANES, nq), BF),
                   jax.ShapeDtypeStruct((b, g, n_blk, GATE_ROWS, CHUNK), F32),
                   raw, raw, k_aug, v_t, k_aug, v_t),
        grid=(b, s // TOKEN_TILE),
        in_specs=[tile(D_MODEL), _layer_spec(norm.shape, layer), _layer_spec(w.shape, layer),
                  _layer_spec(p["q_gain"].shape, layer), _layer_spec(p["k_gain2"].shape, layer),
                  const(p["bd_q"]), const(p["bd_k"]), const(p["q_const"]),
                  key_const(p["k_sel"]), key_const(p["k_win"]), const(p["v_const"])],
        out_specs=(per_blk(LANES, nq), per_blk(GATE_ROWS, CHUNK), tile(KV_WIDTH), tile(KV_WIDTH),
                   keys, per_blk(V_ROWS, CHUNK), keys, per_blk(V_ROWS, CHUNK)),
        compiler_params=_params(2),
        name="nsa_proj",
    )(x3, norm, w, p["q_gain"], p["k_gain2"], p["bd_q"], p["bd_k"], p["q_const"],
      p["k_sel"], p["k_win"], p["v_const"])


def _nsa_attn_body(qt_ref, gate_ref, kc_ref, vc_ref, ks_ref, vst_ref, kw_ref, vwt_ref,
                   w1k_ref, w1kf_ref, w2k_ref, pek_ref, w1v_ref, w1vf_ref, w2v_ref, pev_ref, kn_ref,
                   cpos_ref, ovt_ref, wmask_ref, dmask_ref,
                   o_ref, qs_ref, kcmp_ref, vcmpt_ref, m_ref, acc_ref, part_ref, s_ref):
    j = pl.program_id(1)
    t0 = j * CHUNK
    nq = NSA_REP * CHUNK
    n_sel = ovt_ref.shape[0]
    groups = range(NSA_GROUPS)

    def compress(x_ref, w1_ref, pe_ref, w1f_ref, w2_ref):
        pe_bias = _dot(pe_ref[...], w1f_ref[...])[0:1, :]
        head = [jnp.zeros((CHUNK, CMP_HIDDEN), F32) for _ in groups]
        tail = [jnp.zeros((CHUNK, CMP_HIDDEN), F32) for _ in groups]
        for i in range(CMP_STRIDE):
            x = x_ref[pl.ds(i, CHUNK, stride=CMP_STRIDE), :].astype(BF)
            for g in groups:
                head[g] = head[g] + _dot(x, w1_ref[g, i])
                tail[g] = tail[g] + _dot(x, w1_ref[g, CMP_STRIDE + i])
        return [_dot(_gelu(head[g] + pltpu.roll(tail[g], CHUNK - 1, 0) + pe_bias).astype(BF), w2_ref[...])
                for g in groups]

    @pl.when(j == 0)
    def _():
        k_cmp = compress(kc_ref, w1k_ref, pek_ref, w1kf_ref, w2k_ref)
        v_cmp = compress(vc_ref, w1v_ref, pev_ref, w1vf_ref, w2v_ref)
        lane = lax.broadcasted_iota(jnp.int32, (CHUNK, LANES - HEAD_DIM), 1)
        ones_col = jnp.where(lane == 0, 1.0, 0.0)
        for g in groups:
            halves = [_rms(k_cmp[g], kn_ref[...]).astype(BF), cpos_ref[...]]
            kcmp_ref[g] = jnp.concatenate(halves if g == 0 else halves[::-1], axis=1)
            vcmpt_ref[g] = jnp.concatenate([v_cmp[g], ones_col], axis=1).T[0:V_ROWS, :].astype(BF)

    def q_index(rows):
        return lax.broadcasted_iota(jnp.int32, (rows, nq), 1) % CHUNK

    def k_index(rows):
        return lax.broadcasted_iota(jnp.int32, (rows, nq), 0)

    def select_blocks(p_c):
        p_sum = p_c[:, 0:CHUNK]
        for r in range(1, NSA_REP):
            p_sum = p_sum + p_c[:, r * CHUNK:(r + 1) * CHUNK]
        p_hi, p_lo = _split2(p_sum)
        imp = _dot(ovt_ref[...], p_hi) + _dot(ovt_ref[...], p_lo)
        blk = lax.broadcasted_iota(jnp.int32, (n_sel, CHUNK), 0)
        tq = t0 + lax.broadcasted_iota(jnp.int32, (n_sel, CHUNK), 1)
        cur = tq // SEL_BLOCK
        forced = (blk == 0) | (blk == cur) | (blk == cur - 1)
        score = jnp.where(forced, FORCE, jnp.where(blk * SEL_BLOCK <= tq, imp, NEG))
        blk_f = blk.astype(F32)
        chosen = jnp.zeros((n_sel, CHUNK), F32)
        for _ in range(SEL_TOP_N):
            best = jnp.max(score, axis=0, keepdims=True)
            first = jnp.min(jnp.where(score == best, blk_f, 1e9), axis=0, keepdims=True)
            pick = blk_f == first
            chosen = jnp.where(pick, 1.0, chosen)
            score = jnp.where(pick, -3e38, score)
        return ((chosen - 1.0) * (-NEG)).astype(BF)

    n_w = WINDOW + CHUNK
    w_blk = jnp.maximum(j - WINDOW // CHUNK, 0)
    w_start = pl.multiple_of(w_blk * CHUNK, CHUNK)
    sub_blocks = SEL_KEYS // CHUNK
    n_full = j // sub_blocks

    def gate_row(g, branch):
        rows = gate_ref[g]
        return jnp.concatenate([rows[branch * NSA_REP + r:branch * NSA_REP + r + 1] for r in range(NSA_REP)], axis=1)

    for g in groups:
        qt = qt_ref[g]
        slot = (1 - g) * HEAD_DIM
        ci = k_index(CHUNK)
        valid_c = (ci * CMP_STRIDE + (CMP_BLOCK - 1) <= t0 + q_index(CHUNK)) & (ci < CHUNK - 1)
        s_c = jnp.where(valid_c, _dot(kcmp_ref[g], qt), NEG)
        e_c = jnp.where(valid_c, jnp.exp(s_c - jnp.max(s_c, axis=0, keepdims=True)), 0.0)
        p_c = e_c / jnp.maximum(jnp.sum(e_c, axis=0, keepdims=True), 1e-30)
        o_c = _dot(vcmpt_ref[g], p_c.astype(BF))[0:HEAD_DIM]
        qs_ref[g] = qt
        qs_ref[g, slot:slot + n_sel, :] = jnp.concatenate([select_blocks(p_c)] * NSA_REP, axis=1)
        s_w = _dot(kw_ref[g, pl.ds(w_start, n_w), :], qt) + wmask_ref[j - w_blk]
        p_w = jnp.exp(s_w - jnp.max(s_w, axis=0, keepdims=True)).astype(BF)
        o_w = _dot(vwt_ref[g, w_blk], p_w[0:CHUNK])
        for sub in range(1, n_w // CHUNK):
            o_w = o_w + _dot(vwt_ref[g, w_blk + sub], p_w[sub * CHUNK:(sub + 1) * CHUNK])
        part_ref[g] = gate_row(g, 0) * o_c + gate_row(g, 2) * (o_w[0:HEAD_DIM] / o_w[HEAD_DIM:HEAD_DIM + 1])

    m_ref[...] = jnp.full(m_ref.shape, NEG, F32)
    acc_ref[...] = jnp.zeros(acc_ref.shape, F32)

    def scores(g, blk0):
        start = pl.multiple_of(blk0 * CHUNK, SEL_KEYS)
        return _dot(ks_ref[g, pl.ds(start, SEL_KEYS), :], qs_ref[g])

    def absorb(g, s, blk0):
        m_old = m_ref[g]
        m_new = jnp.maximum(m_old, jnp.max(s, axis=0, keepdims=True))
        p = jnp.exp(s - m_new).astype(BF)
        pv = _dot(vst_ref[g, blk0], p[0:CHUNK])
        for sub in range(1, sub_blocks):
            pv = pv + _dot(vst_ref[g, blk0 + sub], p[sub * CHUNK:(sub + 1) * CHUNK])
        acc_ref[g] = jnp.exp(m_old - m_new) * acc_ref[g] + pv
        m_ref[g] = m_new

    for g in groups:
        s_ref[g] = scores(g, 0)

    def loop_body(i, carry):
        for g in groups:
            s = s_ref[g]
            s_ref[g] = scores(g, (i + 1) * sub_blocks)
            absorb(g, s, i * sub_blocks)
        return carry

    lax.fori_loop(0, n_full, loop_body, 0)
    for g in groups:
        last = n_full * sub_blocks
        absorb(g, s_ref[g] + dmask_ref[j - last], last)
        acc = acc_ref[g]
        o_s = acc[0:HEAD_DIM] / acc[HEAD_DIM:HEAD_DIM + 1]
        out_t = part_ref[g] + gate_row(g, 1) * o_s
        by_head = jnp.concatenate([out_t[:, r * CHUNK:(r + 1) * CHUNK] for r in range(NSA_REP)], axis=0)
        gw = NSA_REP * HEAD_DIM
        o_ref[:, g * gw:(g + 1) * gw] = by_head.T.astype(BF)


def _nsa_attn(qt_aug, gates_t, kc, vc, ks_aug, vs_t, kw_aug, vw_t,
              w1k, w1kf, w2k, pek, w1v, w1vf, w2v, pev, k_gain, cpos, ov_t, win_mask, diag_mask, layer):
    b, g, nblk, _, nq = qt_aug.shape
    blk = lambda a: pl.BlockSpec((None, g, None) + a.shape[3:], lambda i, j: (i, 0, j, 0, 0))
    seq = lambda a: pl.BlockSpec((None,) + a.shape[1:], lambda i, j: (i,) + (0,) * (a.ndim - 1))
    lay = lambda a: pl.BlockSpec((None,) + a.shape[1:], lambda i, j: (layer,) + (0,) * (a.ndim - 1))
    const = lambda a: pl.BlockSpec(a.shape, lambda i, j: (0,) * a.ndim)
    return pl.pallas_call(
        _nsa_attn_body,
        out_shape=jax.ShapeDtypeStruct((b, nblk * CHUNK, g * NSA_REP * HEAD_DIM), BF),
        grid=(b, nblk),
        in_specs=[blk(qt_aug), blk(gates_t), seq(kc), seq(vc),
                  seq(ks_aug), seq(vs_t), seq(kw_aug), seq(vw_t),
                  lay(w1k), lay(w1kf), lay(w2k), lay(pek),
                  lay(w1v), lay(w1vf), lay(w2v), lay(pev), lay(k_gain),
                  const(cpos), const(ov_t), const(win_mask), const(diag_mask)],
        out_specs=pl.BlockSpec((None, CHUNK, g * NSA_REP * HEAD_DIM), lambda i, j: (i, j, 0)),
        scratch_shapes=[pltpu.VMEM((g, LANES, nq), BF),
                        pltpu.VMEM((g, CHUNK, LANES), BF),
                        pltpu.VMEM((g, V_ROWS, CHUNK), BF),
                        pltpu.VMEM((g, 1, nq), F32),
                        pltpu.VMEM((g, V_ROWS, nq), F32),
                        pltpu.VMEM((g, HEAD_DIM, nq), F32),
                        pltpu.VMEM((g, SEL_KEYS, nq), F32)],
        compiler_params=_params(2),
        name="nsa_attn",
    )(qt_aug, gates_t, kc, vc, ks_aug, vs_t, kw_aug, vw_t,
      w1k, w1kf, w2k, pek, w1v, w1vf, w2v, pev, k_gain, cpos, ov_t, win_mask, diag_mask)


def _merge_body(x_ref, ya_ref, yb_ref, yc_ref, yd_ref, g_ref, wg_ref, wb_ref, wo_ref, o_ref):
    x = x_ref[...]
    hn = _rms(x, g_ref[...]).astype(BF)
    merged = jnp.zeros_like(x)
    for i, y_ref in enumerate((ya_ref, yb_ref, yc_ref, yd_ref)):
        gate = jax.nn.sigmoid(_dot(hn, wg_ref[:, i * D_MODEL:(i + 1) * D_MODEL]))
        merged = merged + gate * _dot(y_ref[...], wb_ref[i])
    o_ref[...] = x + _dot(merged.astype(BF), wo_ref[...])


def _merge(x2, ya, yb, yc, yd, norm, w_gate, w_branch, w_out, layer):
    t = x2.shape[0]
    tile = lambda n: pl.BlockSpec((TOKEN_TILE, n), lambda i: (i, 0))
    return pl.pallas_call(
        _merge_body,
        out_shape=jax.ShapeDtypeStruct(x2.shape, F32),
        grid=(t // TOKEN_TILE,),
        in_specs=[tile(D_MODEL)] + [tile(WIDTH)] * 4
        + [_layer_spec(a.shape, layer) for a in (norm, w_gate, w_branch, w_out)],
        out_specs=tile(D_MODEL),
        compiler_params=_params(1),
        name="merge",
    )(x2, ya, yb, yc, yd, norm, w_gate, w_branch, w_out)


def _alibi_slopes():
    return np.array([2.0 ** (-8.0 * (i + 1) / NSA_HEADS) for i in range(NSA_HEADS)], np.float32)


def _nsa_constants(seq):
    n_sel = seq // SEL_BLOCK
    n_cmp = (seq - CMP_BLOCK) // CMP_STRIDE + 1
    assert n_cmp == CHUNK - 1 and n_sel <= SEL_BLOCK // 2
    pos_cols = LANES - HEAD_DIM - n_sel
    slopes = _alibi_slopes().reshape(NSA_GROUPS, NSA_REP)
    q_const = np.zeros((NSA_GROUPS, LANES - HEAD_DIM, NSA_REP * CHUNK), np.float32)
    lane_slope = np.repeat(slopes, CHUNK, axis=1)
    q_const[:, n_sel, :] = lane_slope * SEL_BLOCK
    q_const[:, n_sel + 1, :] = lane_slope
    pos = np.arange(seq)
    pos_part = np.zeros((seq, pos_cols), np.float32)
    pos_part[:, 0] = pos // SEL_BLOCK
    pos_part[:, 1] = pos % SEL_BLOCK
    onehot = (pos[:, None] // SEL_BLOCK == np.arange(n_sel)[None, :]).astype(np.float32)

    def key_const(first):
        half = np.concatenate([first, pos_part], axis=1)
        return np.stack([np.concatenate([np.zeros_like(half), half][::1 if g == 0 else -1], axis=1)
                         for g in range(NSA_GROUPS)])

    k_sel_const = key_const(onehot)
    k_win_const = key_const(np.zeros_like(onehot))
    v_const = np.zeros((V_ROWS - HEAD_DIM, CHUNK), np.float32)
    v_const[0, :] = 1.0
    c_mid = np.arange(CHUNK) * CMP_STRIDE + (CMP_BLOCK - 1) / 2.0
    cmp_const = np.zeros((CHUNK, LANES - HEAD_DIM), np.float32)
    cmp_const[:, n_sel] = c_mid // SEL_BLOCK
    cmp_const[:, n_sel + 1] = c_mid % SEL_BLOCK
    c_start = np.arange(CHUNK) * CMP_STRIDE
    s_start = np.arange(n_sel) * SEL_BLOCK
    overlap_t = ((c_start[None, :] <= s_start[:, None] + SEL_BLOCK - 1)
                 & (c_start[None, :] + CMP_BLOCK - 1 >= s_start[:, None])
                 & (np.arange(CHUNK)[None, :] < n_cmp)).astype(np.float32)
    qi = np.arange(NSA_REP * CHUNK)[None, :] % CHUNK

    def mask(rows, n_cases, valid):
        ki = np.arange(rows)[:, None]
        return np.stack([np.where(valid(d * CHUNK + qi - ki), 0.0, NEG) for d in range(n_cases)]).astype(np.float32)

    win_mask = mask(WINDOW + CHUNK, WINDOW // CHUNK + 1, lambda rel: (rel >= 0) & (rel < WINDOW))
    diag_mask = mask(SEL_KEYS, SEL_KEYS // CHUNK, lambda rel: rel >= 0)
    bf = lambda a: jnp.asarray(a, BF)
    return dict(q_const=bf(q_const), k_sel=bf(k_sel_const), k_win=bf(k_win_const), v_const=bf(v_const),
                cmp_const=bf(cmp_const), overlap_t=bf(overlap_t),
                win_mask=jnp.asarray(win_mask), diag_mask=jnp.asarray(diag_mask))


def _block_diag_ones(n):
    idx = np.arange(n) // HEAD_DIM
    return jnp.asarray(idx[:, None] == idx[None, :], BF)


def _compress_weights(w1):
    nl = w1.shape[0]
    per_pos = w1.reshape(nl, 1, CMP_BLOCK, 1, HEAD_DIM, CMP_HIDDEN)
    group_sel = jnp.eye(NSA_GROUPS, dtype=w1.dtype)
    big = per_pos * group_sel[None, :, None, :, None, None]
    return big.reshape(nl, NSA_GROUPS, CMP_BLOCK, KV_WIDTH, CMP_HIDDEN).astype(BF)


def _nsa_proj_weights(w_in):
    depth = w_in.shape[0]
    gate = w_in[:, :, OFF_GATE:OFF_AB].reshape(depth, D_MODEL, NSA_GROUPS, NSA_REP, 3)
    gate = jnp.swapaxes(gate, 3, 4).reshape(depth, D_MODEL, NSA_GROUPS, 3 * NSA_REP)
    gate = jnp.pad(gate, ((0, 0), (0, 0), (0, 0), (0, GATE_ROWS - 3 * NSA_REP)))
    gate = gate.reshape(depth, D_MODEL, NSA_GROUPS * GATE_ROWS)
    gate = jnp.pad(gate, ((0, 0), (0, 0), (0, LANES - NSA_GROUPS * GATE_ROWS)))
    return jnp.concatenate([w_in[:, :, OFF_Q:OFF_GATE], gate], axis=-1).astype(BF)


def _pe_rows(pe):
    nl = pe.shape[0]
    flat = pe.reshape(nl, 1, CMP_BLOCK * HEAD_DIM)
    return jnp.concatenate([flat, jnp.zeros((nl, 7, CMP_BLOCK * HEAD_DIM), pe.dtype)], axis=1).astype(BF)


def _nsa_params(seq, q_norm, k_norm, pe_k, w1_k, w2_k, pe_v, w1_v, w2_v):
    row = lambda a: a[:, None, :]
    p = _nsa_constants(seq)
    p.update(q_gain=row(jnp.tile(q_norm, (1, NSA_HEADS))), k_gain2=row(jnp.tile(k_norm, (1, NSA_GROUPS))),
             k_gain=row(k_norm), w1k=_compress_weights(w1_k), w1v=_compress_weights(w1_v),
             w1kf=w1_k.astype(BF), w1vf=w1_v.astype(BF), w2k=w2_k.astype(BF), w2v=w2_v.astype(BF),
             pek=_pe_rows(pe_k), pev=_pe_rows(pe_v),
             bd_q=_block_diag_ones(WIDTH), bd_k=_block_diag_ones(KV_WIDTH))
    return p


def _nsa_mixer(x3, mxn, w_c, p, layer):
    bsz, seq, _ = x3.shape
    n_blk = seq // CHUNK
    qt_aug, gates_t, kc, vc, ks_aug, vs_t, kw_aug, vw_t = _nsa_proj(x3, mxn, w_c, p, layer)
    return _nsa_attn(qt_aug, gates_t, kc, vc, ks_aug, vs_t, kw_aug, vw_t,
                     p["w1k"], p["w1kf"], p["w2k"], p["pek"], p["w1v"], p["w1vf"], p["w2v"], p["pev"], p["k_gain"],
                     p["cmp_const"], p["overlap_t"], p["win_mask"], p["diag_mask"], layer)


def kernel(x, ffn1_norm, ffn1_w_in, ffn1_w_out, mix_norm, w_in, sgu_v_norm, sgu_w, sgu_b, ssm_conv_w, ssm_conv_b, ssm_dt_bias, ssm_a_log, ssm_d, ssm_norm, nsa_q_norm, nsa_k_norm, nsa_pe_k, nsa_w1_k, nsa_w2_k, nsa_pe_v, nsa_w1_v, nsa_w2_v, conv_dw_w, conv_dw_b, conv_norm, w_branch, w_out, ffn2_norm, ffn2_w_in, ffn2_w_out):
    bsz, seq, _ = x.shape
    depth = w_in.shape[0]
    n_blk = seq // CHUNK
    row = lambda a: a[:, None, :]
    rep = lambda a, n: jnp.repeat(a, n, axis=-1)

    f1n, f2n, mxn = row(ffn1_norm), row(ffn2_norm), row(mix_norm)
    f1_in, f1_out = ffn1_w_in.astype(BF), ffn1_w_out.astype(BF)
    f2_in, f2_out = ffn2_w_in.astype(BF), ffn2_w_out.astype(BF)
    w_a = w_in[:, :, OFF_UV:OFF_Z].astype(BF)
    w_b = jnp.concatenate([w_in[:, :, OFF_Z:OFF_DT], rep(w_in[:, :, OFF_DT:OFF_Q], SSM_HEAD_DIM)], axis=-1).astype(BF)
    w_c = _nsa_proj_weights(w_in)
    w_d = w_in[:, :, OFF_AB:OFF_MERGE].astype(BF)
    w_g = w_in[:, :, OFF_MERGE:].astype(BF)
    sgu_bias = rep(jnp.swapaxes(sgu_b, 1, 2), WIDTH // SGU_GROUPS)
    dt_bias = row(rep(ssm_dt_bias, SSM_HEAD_DIM))
    a_neg = row(rep(-jnp.exp(ssm_a_log), SSM_HEAD_DIM))
    d_skip = row(rep(ssm_d, SSM_HEAD_DIM))
    wb, wo = w_branch.astype(BF), w_out.astype(BF)
    nsa_p = _nsa_params(seq, nsa_q_norm, nsa_k_norm, nsa_pe_k, nsa_w1_k, nsa_w2_k, nsa_pe_v, nsa_w1_v, nsa_w2_v)

    x2 = x.reshape(bsz * seq, D_MODEL)
    for l in range(depth):
        x2 = _ffn(x2, f1n, f1_in, f1_out, l)
        x3 = x2.reshape(bsz, seq, D_MODEL)
        y_a = _sgu(x3, mxn, w_a, row(sgu_v_norm), sgu_w, sgu_bias, l)
        y_b = _ssd(x3, mxn, w_b, ssm_conv_w, row(ssm_conv_b), dt_bias, a_neg, d_skip, row(ssm_norm), l)
        y_d = _conv_module(x3, mxn, w_d, conv_dw_w, row(conv_dw_b), row(conv_norm), l)
        y_c = _nsa_mixer(x3, mxn, w_c, nsa_p, l)
        flat = lambda y: y.reshape(bsz * seq, WIDTH)
        x2 = _merge(x2, flat(y_a), flat(y_b), flat(y_c), flat(y_d), mxn, w_g, wb, wo, l)
        x2 = _ffn(x2, f2n, f2_in, f2_out, l)
    return x2.reshape(bsz, seq, D_MODEL)
```

```python
import functools
import math

import numpy as np
import jax
import jax.numpy as jnp
from jax import lax
from jax.experimental import pallas as pl
from jax.experimental.pallas import tpu as pltpu

F32 = jnp.float32
BF = jnp.bfloat16

D_MODEL = 1024
FFN_DIM = 2816
WIDTH = 512
N_BRANCH = 4
SGU_GROUPS = 4
CHUNK = 128
SSM_HEADS = 8
SSM_HEAD_DIM = 64
SSM_GROUPS = 2
SSM_STATE = 128
SSM_CONV = 4
SSM_CONV_DIM = WIDTH + 2 * SSM_GROUPS * SSM_STATE
SSM_REP = SSM_HEADS // SSM_GROUPS
NSA_HEADS = 8
NSA_GROUPS = 2
NSA_REP = NSA_HEADS // NSA_GROUPS
HEAD_DIM = 64
KV_WIDTH = NSA_GROUPS * HEAD_DIM
CMP_BLOCK = 32
CMP_STRIDE = 16
CMP_HIDDEN = 128
SEL_BLOCK = 64
SEL_TOP_N = 8
WINDOW = 256
CONV_KERNEL = 31
EPS = 1e-6
NEG = -1e30
FORCE = 1e9

OFF_UV = 0
OFF_Z = OFF_UV + 2 * WIDTH
OFF_XBC = OFF_Z + WIDTH
OFF_DT = OFF_XBC + SSM_CONV_DIM
OFF_Q = OFF_DT + SSM_HEADS
OFF_KV = OFF_Q + WIDTH
OFF_GATE = OFF_KV + 6 * KV_WIDTH
OFF_AB = OFF_GATE + 3 * NSA_HEADS
OFF_MERGE = OFF_AB + 2 * WIDTH
IN_PROJ_DIM = OFF_MERGE + N_BRANCH * D_MODEL

LANES = 128
SUBLANES = 8
VMEM_LIMIT_BYTES = 56 * 1024 * 1024

TOKEN_TILE = 256
CONV_HIST = 32
SSM_HIST = 8
SEL_KEYS = 256
V_ROWS = 80
GATE_ROWS = 16


def _params(n_axes):
    return pltpu.CompilerParams(dimension_semantics=("arbitrary",) * n_axes,
                                vmem_limit_bytes=VMEM_LIMIT_BYTES)


def _rms(x, g):
    return x * lax.rsqrt(jnp.mean(x * x, axis=-1, keepdims=True) + EPS) * g


def _silu(x):
    return x * jax.nn.sigmoid(x)


def _gelu(x):
    return 0.5 * x * (1.0 + jnp.tanh(math.sqrt(2.0 / math.pi) * (x + 0.044715 * (x * x * x))))


def _dot(a, b):
    return jnp.dot(a, b, preferred_element_type=F32)


def _dot_nt(a, b):
    return lax.dot_general(a, b, (((1,), (1,)), ((), ())), preferred_element_type=F32)


def _dot_tn(a, b):
    return lax.dot_general(a, b, (((0,), (0,)), ((), ())), preferred_element_type=F32)


def _split3(x):
    hi = x.astype(BF)
    r = x - hi.astype(F32)
    mid = r.astype(BF)
    lo = (r - mid.astype(F32)).astype(BF)
    return hi, mid, lo


def _split2(x):
    hi = x.astype(BF)
    lo = (x - hi.astype(F32)).astype(BF)
    return hi, lo


def _layer_spec(shape, layer):
    nd = len(shape)
    return pl.BlockSpec((None,) + tuple(shape[1:]), lambda *_: (layer,) + (0,) * (nd - 1))


def _ffn_body(x_ref, g_ref, win_ref, wout_ref, o_ref, *, n_chunks):
    x = x_ref[...]
    hn = _rms(x, g_ref[...]).astype(BF)
    tf = FFN_DIM // n_chunks
    acc = jnp.zeros_like(x)
    for c in range(n_chunks):
        gate = _dot(hn, win_ref[:, c * tf:(c + 1) * tf])
        up = _dot(hn, win_ref[:, FFN_DIM + c * tf:FFN_DIM + (c + 1) * tf])
        a = (_silu(gate) * up).astype(BF)
        acc = acc + _dot(a, wout_ref[c * tf:(c + 1) * tf, :])
    o_ref[...] = x + 0.5 * acc


def _ffn(x2, norm, w_in, w_out, layer):
    t = x2.shape[0]
    return pl.pallas_call(
        functools.partial(_ffn_body, n_chunks=2),
        out_shape=jax.ShapeDtypeStruct(x2.shape, F32),
        grid=(t // TOKEN_TILE,),
        in_specs=[pl.BlockSpec((TOKEN_TILE, D_MODEL), lambda i: (i, 0)),
                  _layer_spec(norm.shape, layer),
                  _layer_spec(w_in.shape, layer),
                  _layer_spec(w_out.shape, layer)],
        out_specs=pl.BlockSpec((TOKEN_TILE, D_MODEL), lambda i: (i, 0)),
        compiler_params=_params(1),
        name="ffn",
    )(x2, norm, w_in, w_out)


def _sgu_part(hn, w_ref, vn_ref, ws_ref, bs_ref, o_ref):
    uv = _gelu(_dot(hn, w_ref[...]))
    u = uv[:, :WIDTH]
    v = _rms(uv[:, WIDTH:], vn_ref[...]).astype(BF)
    row = lax.broadcasted_iota(jnp.int32, (CHUNK, CHUNK), 0)
    col = lax.broadcasted_iota(jnp.int32, (CHUNK, CHUNK), 1)
    gw = WIDTH // SGU_GROUPS
    ws = [jnp.where(row >= col, ws_ref[g], 0.0).astype(BF) for g in range(SGU_GROUPS)]
    bias = bs_ref[...]
    for c in range(TOKEN_TILE // CHUNK):
        rows = slice(c * CHUNK, (c + 1) * CHUNK)
        mixed = jnp.concatenate(
            [_dot(ws[g], v[rows, g * gw:(g + 1) * gw]) for g in range(SGU_GROUPS)], axis=1)
        o_ref[rows, :] = (u[rows, :] * (mixed + bias)).astype(BF)


def _conv_part(hn, w_ref, dw_ref, db_ref, cn_ref, o_ref, buf_ref, shift_ref):
    ab = _dot(hn, w_ref[...])
    buf_ref[CONV_HIST:CONV_HIST + TOKEN_TILE, :] = ab[:, :WIDTH] * jax.nn.sigmoid(ab[:, WIDTH:])
    acc = jnp.zeros((TOKEN_TILE, WIDTH), F32) + db_ref[...]
    first = CONV_HIST - (CONV_KERNEL - 1)
    for res in range(min(SUBLANES, CONV_KERNEL)):
        n_taps = (CONV_KERNEL - 1 - res) // SUBLANES + 1
        rows = TOKEN_TILE + (n_taps - 1) * SUBLANES
        shift_ref[0:rows, :] = buf_ref[first + res:first + res + rows, :]
        for a in range(n_taps):
            k = a * SUBLANES + res
            acc = acc + dw_ref[k:k + 1, :] * shift_ref[a * SUBLANES:a * SUBLANES + TOKEN_TILE, :]
    o_ref[...] = _silu(_rms(acc, cn_ref[...])).astype(BF)
    buf_ref[0:CONV_HIST, :] = buf_ref[TOKEN_TILE:TOKEN_TILE + CONV_HIST, :]


def _ssd_part(hn, w_ref, cw_ref, cb_ref, dtb_ref, a_ref, dsk_ref, ng_ref, o_ref, buf_ref, st_ref):
    hw = SSM_REP * SSM_HEAD_DIM
    proj = _dot(hn, w_ref[...])
    z = proj[:, :WIDTH]
    buf_ref[SSM_HIST:SSM_HIST + TOKEN_TILE, :] = proj[:, WIDTH:WIDTH + SSM_CONV_DIM]
    dt = jax.nn.softplus(proj[:, WIDTH + SSM_CONV_DIM:] + dtb_ref[...])
    xbc = jnp.zeros((TOKEN_TILE, SSM_CONV_DIM), F32) + cb_ref[...]
    first = SSM_HIST - (SSM_CONV - 1)
    for k in range(SSM_CONV):
        xbc = xbc + cw_ref[k:k + 1, :] * buf_ref[first + k:first + k + TOKEN_TILE, :]
    buf_ref[0:SSM_HIST, :] = buf_ref[TOKEN_TILE:TOKEN_TILE + SSM_HIST, :]
    xbc = _silu(xbc)
    xs = xbc[:, :WIDTH]
    a_all = dt * a_ref[...]
    xdt = xs * dt

    row = lax.broadcasted_iota(jnp.int32, (CHUNK, CHUNK), 0)
    col = lax.broadcasted_iota(jnp.int32, (CHUNK, CHUNK), 1)
    causal = row >= col
    tril = jnp.where(causal, 1.0, 0.0).astype(BF)
    lane = lax.broadcasted_iota(jnp.int32, (CHUNK, 2 * SSM_HEAD_DIM), 1)
    low_half = lane < SSM_HEAD_DIM

    for c in range(TOKEN_TILE // CHUNK):
        rows = slice(c * CHUNK, (c + 1) * CHUNK)
        hi, mid, lo = _split3(a_all[rows, :])
        cs = _dot(tril, hi) + _dot(tril, mid) + _dot(tril, lo)
        cs_t = cs.T
        cs_last = cs[CHUNK - 1:CHUNK, :]
        xc = xdt[rows, :]
        x_decay = (xc * jnp.exp(cs_last - cs)).astype(BF)
        xc_bf = xc.astype(BF)
        grow = jnp.exp(cs)
        y_parts = []
        for g in range(SSM_GROUPS):
            bg = xbc[rows, WIDTH + g * SSM_STATE:WIDTH + (g + 1) * SSM_STATE].astype(BF)
            cg = xbc[rows, WIDTH + (SSM_GROUPS + g) * SSM_STATE:
                     WIDTH + (SSM_GROUPS + g + 1) * SSM_STATE].astype(BF)
            cb = _dot_nt(cg, bg)
            gl = slice(g * hw, (g + 1) * hw)
            y_off = _dot(cg, st_ref[g].astype(BF)) * grow[:, gl]
            y_diag = []
            for pair in range(SSM_REP // 2):
                scores = []
                for hh in range(2):
                    hc = g * hw + (2 * pair + hh) * SSM_HEAD_DIM
                    seg = cs[:, hc:hc + 1] - cs_t[hc:hc + 1, :]
                    decay = jnp.exp(jnp.where(causal, seg, NEG))
                    scores.append((cb * decay).astype(BF))
                xp = xc_bf[:, g * hw + pair * 2 * SSM_HEAD_DIM:g * hw + (pair + 1) * 2 * SSM_HEAD_DIM]
                zero = jnp.zeros_like(xp)
                rhs = jnp.concatenate([jnp.where(low_half, xp, zero),
                                       jnp.where(low_half, zero, xp)], axis=0)
                y_diag.append(_dot(jnp.concatenate(scores, axis=1), rhs))
            y_parts.append(jnp.concatenate(y_diag, axis=1) + y_off)
            st_ref[g] = jnp.exp(cs_last[:, gl]) * st_ref[g] + _dot_tn(bg, x_decay[:, gl])
        y = jnp.concatenate(y_parts, axis=1) + xs[rows, :] * dsk_ref[...]
        y = y * _silu(z[rows, :])
        o_ref[rows, :] = _rms(y, ng_ref[...]).astype(BF)


def _head_rms(x, ones_blockdiag, gain, scale):
    hi, lo = _split2(x * x)
    ss = _dot(hi, ones_blockdiag) + _dot(lo, ones_blockdiag)
    return x * lax.rsqrt(ss * (1.0 / HEAD_DIM) + EPS) * (gain * scale)


def _nsa_proj_part(hn, w_ref, qn_ref, kn_ref, bdq_ref, bdk_ref, qc_ref, ksc_ref, kwc_ref, vcn_ref,
                   qt_ref, gate_ref, kc_ref, vc_ref, ks_ref, vst_ref, kw_ref, vwt_ref):
    proj = _dot(hn, w_ref[...])
    o = WIDTH
    kc_ref[...] = proj[:, o:o + KV_WIDTH]
    vc_ref[...] = proj[:, o + KV_WIDTH:o + 2 * KV_WIDTH]
    q = _head_rms(proj[:, :WIDTH], bdq_ref[...], qn_ref[...], HEAD_DIM ** -0.5)
    ks = _head_rms(proj[:, o + 2 * KV_WIDTH:o + 3 * KV_WIDTH], bdk_ref[...], kn_ref[...], 1.0)
    kw = _head_rms(proj[:, o + 4 * KV_WIDTH:o + 5 * KV_WIDTH], bdk_ref[...], kn_ref[...], 1.0)
    vs = proj[:, o + 3 * KV_WIDTH:o + 4 * KV_WIDTH]
    vw = proj[:, o + 5 * KV_WIDTH:o + 6 * KV_WIDTH]
    gate_t = jax.nn.sigmoid(proj[:, o + 6 * KV_WIDTH:]).T
    lane = lax.broadcasted_iota(jnp.int32, (TOKEN_TILE, KV_WIDTH), 1)
    gw = NSA_REP * HEAD_DIM
    for g in range(NSA_GROUPS):
        own = (lane >= g * HEAD_DIM) & (lane < (g + 1) * HEAD_DIM)
        ks_ref[g] = (jnp.where(own, ks, 0.0) + ksc_ref[g]).astype(BF)
        kw_ref[g] = (jnp.where(own, kw, 0.0) + kwc_ref[g]).astype(BF)
    for c in range(TOKEN_TILE // CHUNK):
        rows = slice(c * CHUNK, (c + 1) * CHUNK)
        vs_t = vs[rows, :].T.astype(BF)
        vw_t = vw[rows, :].T.astype(BF)
        for g in range(NSA_GROUPS):
            feat = slice(g * HEAD_DIM, (g + 1) * HEAD_DIM)
            rest = slice((1 - g) * HEAD_DIM, (2 - g) * HEAD_DIM)
            q_t = q[rows, g * gw:(g + 1) * gw].T
            qt_ref[g, c, feat, :] = jnp.concatenate(
                [q_t[r * HEAD_DIM:(r + 1) * HEAD_DIM] for r in range(NSA_REP)], axis=1).astype(BF)
            qt_ref[g, c, rest, :] = qc_ref[g]
            gate_ref[g, c] = gate_t[g * GATE_ROWS:(g + 1) * GATE_ROWS, rows]
            vst_ref[g, c, 0:HEAD_DIM, :] = vs_t[feat]
            vst_ref[g, c, HEAD_DIM:V_ROWS, :] = vcn_ref[...]
            vwt_ref[g, c, 0:HEAD_DIM, :] = vw_t[feat]
            vwt_ref[g, c, HEAD_DIM:V_ROWS, :] = vcn_ref[...]


N_SGU_IN, N_SSD_IN, N_CONV_IN, N_NSA_IN = 4, 7, 4, 9


def _mixers_body(x_ref, g_ref, *refs):
    n_in = N_SGU_IN + N_SSD_IN + N_CONV_IN + N_NSA_IN
    ins, outs, scratch = refs[:n_in], refs[n_in:n_in + 11], refs[n_in + 11:]
    sgu_in, ins = ins[:N_SGU_IN], ins[N_SGU_IN:]
    ssd_in, ins = ins[:N_SSD_IN], ins[N_SSD_IN:]
    conv_in, nsa_in = ins[:N_CONV_IN], ins[N_CONV_IN:]
    ya_ref, yb_ref, yd_ref = outs[:3]
    conv_buf, conv_shift, ssd_buf, ssd_state = scratch

    @pl.when(pl.program_id(1) == 0)
    def _():
        conv_buf[0:CONV_HIST, :] = jnp.zeros((CONV_HIST, WIDTH), F32)
        ssd_buf[0:SSM_HIST, :] = jnp.zeros((SSM_HIST, SSM_CONV_DIM), F32)
        ssd_state[...] = jnp.zeros(ssd_state.shape, F32)

    hn = _rms(x_ref[...], g_ref[...]).astype(BF)
    _sgu_part(hn, *sgu_in, ya_ref)
    _ssd_part(hn, *ssd_in, yb_ref, ssd_buf, ssd_state)
    _conv_part(hn, *conv_in, yd_ref, conv_buf, conv_shift)
    _nsa_proj_part(hn, *nsa_in, *outs[3:])


def _mixers(x3, norm, sgu_p, ssd_p, conv_p, w_c, p, layer):
    b, s, _ = x3.shape
    g, nq, blocks = NSA_GROUPS, NSA_REP * CHUNK, TOKEN_TILE // CHUNK
    n_blk = s // CHUNK
    tile = lambda n: pl.BlockSpec((None, TOKEN_TILE, n), lambda i, j: (i, j, 0))
    const = lambda a: pl.BlockSpec(a.shape, lambda i, j: (0,) * a.ndim)
    lay = lambda a: _layer_spec(a.shape, layer)
    key_const = lambda a: pl.BlockSpec((g, TOKEN_TILE, LANES), lambda i, j: (0, j, 0))
    per_blk = lambda rows, lanes: pl.BlockSpec((None, g, blocks, rows, lanes), lambda i, j: (i, 0, j, 0, 0))
    keys = pl.BlockSpec((None, g, TOKEN_TILE, LANES), lambda i, j: (i, 0, j, 0))
    y = jax.ShapeDtypeStruct((b, s, WIDTH), BF)
    raw = jax.ShapeDtypeStruct((b, s, KV_WIDTH), F32)
    k_aug = jax.ShapeDtypeStruct((b, g, s, LANES), BF)
    v_t = jax.ShapeDtypeStruct((b, g, n_blk, V_ROWS, CHUNK), BF)
    assert (len(sgu_p), len(ssd_p), len(conv_p)) == (N_SGU_IN, N_SSD_IN, N_CONV_IN)
    nsa_layer = (w_c, p["q_gain"], p["k_gain2"])
    nsa_const = (p["bd_q"], p["bd_k"], p["q_const"])
    layered = tuple(sgu_p) + tuple(ssd_p) + tuple(conv_p) + nsa_layer
    return pl.pallas_call(
        _mixers_body,
        out_shape=(y, y, y,
                   jax.ShapeDtypeStruct((b, g, n_blk, LANES, nq), BF),
                   jax.ShapeDtypeStruct((b, g, n_blk, GATE_ROWS, CHUNK), F32),
                   raw, raw, k_aug, v_t, k_aug, v_t),
        grid=(b, s // TOKEN_TILE),
        in_specs=[tile(D_MODEL), lay(norm)] + [lay(a) for a in layered] + [const(a) for a in nsa_const]
        + [key_const(p["k_sel"]), key_const(p["k_win"]), const(p["v_const"])],
        out_specs=(tile(WIDTH),) * 3
        + (per_blk(LANES, nq), per_blk(GATE_ROWS, CHUNK), tile(KV_WIDTH), tile(KV_WIDTH),
           keys, per_blk(V_ROWS, CHUNK), keys, per_blk(V_ROWS, CHUNK)),
        scratch_shapes=[pltpu.VMEM((TOKEN_TILE + CONV_HIST, WIDTH), F32),
                        pltpu.VMEM((TOKEN_TILE + CONV_HIST, WIDTH), F32),
                        pltpu.VMEM((TOKEN_TILE + SSM_HIST, SSM_CONV_DIM), F32),
                        pltpu.VMEM((SSM_GROUPS, SSM_STATE, SSM_REP * SSM_HEAD_DIM), F32)],
        compiler_params=_params(2),
        name="mixers",
    )(x3, norm, *layered, *nsa_const, p["k_sel"], p["k_win"], p["v_const"])


def _nsa_attn_body(qt_ref, gate_ref, kc_ref, vc_ref, ks_ref, vst_ref, kw_ref, vwt_ref,
                   w1k_ref, w1kf_ref, w2k_ref, pek_ref, w1v_ref, w1vf_ref, w2v_ref, pev_ref, kn_ref,
                   cpos_ref, ovt_ref, wmask_ref, dmask_ref,
                   o_ref, qs_ref, kcmp_ref, vcmpt_ref, m_ref, acc_ref, part_ref, s_ref):
    j = pl.program_id(1)
    t0 = j * CHUNK
    nq = NSA_REP * CHUNK
    n_sel = ovt_ref.shape[0]
    groups = range(NSA_GROUPS)

    def compress(x_ref, w1_ref, pe_ref, w1f_ref, w2_ref):
        pe_bias = _dot(pe_ref[...], w1f_ref[...])[0:1, :]
        head = [jnp.zeros((CHUNK, CMP_HIDDEN), F32) for _ in groups]
        tail = [jnp.zeros((CHUNK, CMP_HIDDEN), F32) for _ in groups]
        for i in range(CMP_STRIDE):
            x = x_ref[pl.ds(i, CHUNK, stride=CMP_STRIDE), :].astype(BF)
            for g in groups:
                head[g] = head[g] + _dot(x, w1_ref[g, i])
                tail[g] = tail[g] + _dot(x, w1_ref[g, CMP_STRIDE + i])
        return [_dot(_gelu(head[g] + pltpu.roll(tail[g], CHUNK - 1, 0) + pe_bias).astype(BF), w2_ref[...])
                for g in groups]

    @pl.when(j == 0)
    def _():
        k_cmp = compress(kc_ref, w1k_ref, pek_ref, w1kf_ref, w2k_ref)
        v_cmp = compress(vc_ref, w1v_ref, pev_ref, w1vf_ref, w2v_ref)
        lane = lax.broadcasted_iota(jnp.int32, (CHUNK, LANES - HEAD_DIM), 1)
        ones_col = jnp.where(lane == 0, 1.0, 0.0)
        for g in groups:
            halves = [_rms(k_cmp[g], kn_ref[...]).astype(BF), cpos_ref[...]]
            kcmp_ref[g] = jnp.concatenate(halves if g == 0 else halves[::-1], axis=1)
            vcmpt_ref[g] = jnp.concatenate([v_cmp[g], ones_col], axis=1).T[0:V_ROWS, :].astype(BF)

    def q_index(rows):
        return lax.broadcasted_iota(jnp.int32, (rows, nq), 1) % CHUNK

    def k_index(rows):
        return lax.broadcasted_iota(jnp.int32, (rows, nq), 0)

    def select_blocks(p_c):
        p_sum = p_c[:, 0:CHUNK]
        for r in range(1, NSA_REP):
            p_sum = p_sum + p_c[:, r * CHUNK:(r + 1) * CHUNK]
        p_hi, p_lo = _split2(p_sum)
        imp = _dot(ovt_ref[...], p_hi) + _dot(ovt_ref[...], p_lo)
        blk = lax.broadcasted_iota(jnp.int32, (n_sel, CHUNK), 0)
        tq = t0 + lax.broadcasted_iota(jnp.int32, (n_sel, CHUNK), 1)
        cur = tq // SEL_BLOCK
        forced = (blk == 0) | (blk == cur) | (blk == cur - 1)
        score = jnp.where(forced, FORCE, jnp.where(blk * SEL_BLOCK <= tq, imp, NEG))
        blk_f = blk.astype(F32)
        chosen = jnp.zeros((n_sel, CHUNK), F32)
        for _ in range(SEL_TOP_N):
            best = jnp.max(score, axis=0, keepdims=True)
            first = jnp.min(jnp.where(score == best, blk_f, 1e9), axis=0, keepdims=True)
            pick = blk_f == first
            chosen = jnp.where(pick, 1.0, chosen)
            score = jnp.where(pick, -3e38, score)
        return ((chosen - 1.0) * (-NEG)).astype(BF)

    n_w = WINDOW + CHUNK
    w_blk = jnp.maximum(j - WINDOW // CHUNK, 0)
    w_start = pl.multiple_of(w_blk * CHUNK, CHUNK)
    sub_blocks = SEL_KEYS // CHUNK
    n_full = j // sub_blocks

    def gate_row(g, branch):
        rows = gate_ref[g]
        return jnp.concatenate([rows[branch * NSA_REP + r:branch * NSA_REP + r + 1] for r in range(NSA_REP)], axis=1)

    for g in groups:
        qt = qt_ref[g]
        slot = (1 - g) * HEAD_DIM
        ci = k_index(CHUNK)
        valid_c = (ci * CMP_STRIDE + (CMP_BLOCK - 1) <= t0 + q_index(CHUNK)) & (ci < CHUNK - 1)
        s_c = jnp.where(valid_c, _dot(kcmp_ref[g], qt), NEG)
        e_c = jnp.where(valid_c, jnp.exp(s_c - jnp.max(s_c, axis=0, keepdims=True)), 0.0)
        p_c = e_c / jnp.maximum(jnp.sum(e_c, axis=0, keepdims=True), 1e-30)
        o_c = _dot(vcmpt_ref[g], p_c.astype(BF))[0:HEAD_DIM]
        qs_ref[g] = qt
        qs_ref[g, slot:slot + n_sel, :] = jnp.concatenate([select_blocks(p_c)] * NSA_REP, axis=1)
        s_w = _dot(kw_ref[g, pl.ds(w_start, n_w), :], qt) + wmask_ref[j - w_blk]
        p_w = jnp.exp(s_w - jnp.max(s_w, axis=0, keepdims=True)).astype(BF)
        o_w = _dot(vwt_ref[g, w_blk], p_w[0:CHUNK])
        for sub in range(1, n_w // CHUNK):
            o_w = o_w + _dot(vwt_ref[g, w_blk + sub], p_w[sub * CHUNK:(sub + 1) * CHUNK])
        part_ref[g] = gate_row(g, 0) * o_c + gate_row(g, 2) * (o_w[0:HEAD_DIM] / o_w[HEAD_DIM:HEAD_DIM + 1])

    m_ref[...] = jnp.full(m_ref.shape, NEG, F32)
    acc_ref[...] = jnp.zeros(acc_ref.shape, F32)

    def scores(g, blk0):
        start = pl.multiple_of(blk0 * CHUNK, SEL_KEYS)
        return _dot(ks_ref[g, pl.ds(start, SEL_KEYS), :], qs_ref[g])

    def absorb(g, s, blk0):
        m_old = m_ref[g]
        m_new = jnp.maximum(m_old, jnp.max(s, axis=0, keepdims=True))
        p = jnp.exp(s - m_new).astype(BF)
        pv = _dot(vst_ref[g, blk0], p[0:CHUNK])
        for sub in range(1, sub_blocks):
            pv = pv + _dot(vst_ref[g, blk0 + sub], p[sub * CHUNK:(sub + 1) * CHUNK])
        acc_ref[g] = jnp.exp(m_old - m_new) * acc_ref[g] + pv
        m_ref[g] = m_new

    for g in groups:
        s_ref[g] = scores(g, 0)

    def loop_body(i, carry):
        for g in groups:
            s = s_ref[g]
            s_ref[g] = scores(g, (i + 1) * sub_blocks)
            absorb(g, s, i * sub_blocks)
        return carry

    lax.fori_loop(0, n_full, loop_body, 0)
    for g in groups:
        last = n_full * sub_blocks
        absorb(g, s_ref[g] + dmask_ref[j - last], last)
        acc = acc_ref[g]
        o_s = acc[0:HEAD_DIM] / acc[HEAD_DIM:HEAD_DIM + 1]
        out_t = part_ref[g] + gate_row(g, 1) * o_s
        by_head = jnp.concatenate([out_t[:, r * CHUNK:(r + 1) * CHUNK] for r in range(NSA_REP)], axis=0)
        gw = NSA_REP * HEAD_DIM
        o_ref[:, g * gw:(g + 1) * gw] = by_head.T.astype(BF)


def _nsa_attn(qt_aug, gates_t, kc, vc, ks_aug, vs_t, kw_aug, vw_t,
              w1k, w1kf, w2k, pek, w1v, w1vf, w2v, pev, k_gain, cpos, ov_t, win_mask, diag_mask, layer):
    b, g, nblk, _, nq = qt_aug.shape
    blk = lambda a: pl.BlockSpec((None, g, None) + a.shape[3:], lambda i, j: (i, 0, j, 0, 0))
    seq = lambda a: pl.BlockSpec((None,) + a.shape[1:], lambda i, j: (i,) + (0,) * (a.ndim - 1))
    lay = lambda a: pl.BlockSpec((None,) + a.shape[1:], lambda i, j: (layer,) + (0,) * (a.ndim - 1))
    const = lambda a: pl.BlockSpec(a.shape, lambda i, j: (0,) * a.ndim)
    return pl.pallas_call(
        _nsa_attn_body,
        out_shape=jax.ShapeDtypeStruct((b, nblk * CHUNK, g * NSA_REP * HEAD_DIM), BF),
        grid=(b, nblk),
        in_specs=[blk(qt_aug), blk(gates_t), seq(kc), seq(vc),
                  seq(ks_aug), seq(vs_t), seq(kw_aug), seq(vw_t),
                  lay(w1k), lay(w1kf), lay(w2k), lay(pek),
                  lay(w1v), lay(w1vf), lay(w2v), lay(pev), lay(k_gain),
                  const(cpos), const(ov_t), const(win_mask), const(diag_mask)],
        out_specs=pl.BlockSpec((None, CHUNK, g * NSA_REP * HEAD_DIM), lambda i, j: (i, j, 0)),
        scratch_shapes=[pltpu.VMEM((g, LANES, nq), BF),
                        pltpu.VMEM((g, CHUNK, LANES), BF),
                        pltpu.VMEM((g, V_ROWS, CHUNK), BF),
                        pltpu.VMEM((g, 1, nq), F32),
                        pltpu.VMEM((g, V_ROWS, nq), F32),
                        pltpu.VMEM((g, HEAD_DIM, nq), F32),
                        pltpu.VMEM((g, SEL_KEYS, nq), F32)],
        compiler_params=_params(2),
        name="nsa_attn",
    )(qt_aug, gates_t, kc, vc, ks_aug, vs_t, kw_aug, vw_t,
      w1k, w1kf, w2k, pek, w1v, w1vf, w2v, pev, k_gain, cpos, ov_t, win_mask, diag_mask)


def _merge_body(x_ref, ya_ref, yb_ref, yc_ref, yd_ref, g_ref, wg_ref, wb_ref, wo_ref, o_ref):
    x = x_ref[...]
    hn = _rms(x, g_ref[...]).astype(BF)
    merged = jnp.zeros_like(x)
    for i, y_ref in enumerate((ya_ref, yb_ref, yc_ref, yd_ref)):
        gate = jax.nn.sigmoid(_dot(hn, wg_ref[:, i * D_MODEL:(i + 1) * D_MODEL]))
        merged = merged + gate * _dot(y_ref[...], wb_ref[i])
    o_ref[...] = x + _dot(merged.astype(BF), wo_ref[...])


def _merge(x2, ya, yb, yc, yd, norm, w_gate, w_branch, w_out, layer):
    t = x2.shape[0]
    tile = lambda n: pl.BlockSpec((TOKEN_TILE, n), lambda i: (i, 0))
    return pl.pallas_call(
        _merge_body,
        out_shape=jax.ShapeDtypeStruct(x2.shape, F32),
        grid=(t // TOKEN_TILE,),
        in_specs=[tile(D_MODEL)] + [tile(WIDTH)] * 4
        + [_layer_spec(a.shape, layer) for a in (norm, w_gate, w_branch, w_out)],
        out_specs=tile(D_MODEL),
        compiler_params=_params(1),
        name="merge",
    )(x2, ya, yb, yc, yd, norm, w_gate, w_branch, w_out)


def _alibi_slopes():
    return np.array([2.0 ** (-8.0 * (i + 1) / NSA_HEADS) for i in range(NSA_HEADS)], np.float32)


def _nsa_constants(seq):
    n_sel = seq // SEL_BLOCK
    n_cmp = (seq - CMP_BLOCK) // CMP_STRIDE + 1
    assert n_cmp == CHUNK - 1 and n_sel <= SEL_BLOCK // 2
    pos_cols = LANES - HEAD_DIM - n_sel
    slopes = _alibi_slopes().reshape(NSA_GROUPS, NSA_REP)
    q_const = np.zeros((NSA_GROUPS, LANES - HEAD_DIM, NSA_REP * CHUNK), np.float32)
    lane_slope = np.repeat(slopes, CHUNK, axis=1)
    q_const[:, n_sel, :] = lane_slope * SEL_BLOCK
    q_const[:, n_sel + 1, :] = lane_slope
    pos = np.arange(seq)
    pos_part = np.zeros((seq, pos_cols), np.float32)
    pos_part[:, 0] = pos // SEL_BLOCK
    pos_part[:, 1] = pos % SEL_BLOCK
    onehot = (pos[:, None] // SEL_BLOCK == np.arange(n_sel)[None, :]).astype(np.float32)

    def key_const(first):
        half = np.concatenate([first, pos_part], axis=1)
        return np.stack([np.concatenate([np.zeros_like(half), half][::1 if g == 0 else -1], axis=1)
                         for g in range(NSA_GROUPS)])

    k_sel_const = key_const(onehot)
    k_win_const = key_const(np.zeros_like(onehot))
    v_const = np.zeros((V_ROWS - HEAD_DIM, CHUNK), np.float32)
    v_const[0, :] = 1.0
    c_mid = np.arange(CHUNK) * CMP_STRIDE + (CMP_BLOCK - 1) / 2.0
    cmp_const = np.zeros((CHUNK, LANES - HEAD_DIM), np.float32)
    cmp_const[:, n_sel] = c_mid // SEL_BLOCK
    cmp_const[:, n_sel + 1] = c_mid % SEL_BLOCK
    c_start = np.arange(CHUNK) * CMP_STRIDE
    s_start = np.arange(n_sel) * SEL_BLOCK
    overlap_t = ((c_start[None, :] <= s_start[:, None] + SEL_BLOCK - 1)
                 & (c_start[None, :] + CMP_BLOCK - 1 >= s_start[:, None])
                 & (np.arange(CHUNK)[None, :] < n_cmp)).astype(np.float32)
    qi = np.arange(NSA_REP * CHUNK)[None, :] % CHUNK

    def mask(rows, n_cases, valid):
        ki = np.arange(rows)[:, None]
        return np.stack([np.where(valid(d * CHUNK + qi - ki), 0.0, NEG) for d in range(n_cases)]).astype(np.float32)

    win_mask = mask(WINDOW + CHUNK, WINDOW // CHUNK + 1, lambda rel: (rel >= 0) & (rel < WINDOW))
    diag_mask = mask(SEL_KEYS, SEL_KEYS // CHUNK, lambda rel: rel >= 0)
    bf = lambda a: jnp.asarray(a, BF)
    return dict(q_const=bf(q_const), k_sel=bf(k_sel_const), k_win=bf(k_win_const), v_const=bf(v_const),
                cmp_const=bf(cmp_const), overlap_t=bf(overlap_t),
                win_mask=jnp.asarray(win_mask), diag_mask=jnp.asarray(diag_mask))


def _block_diag_ones(n):
    idx = np.arange(n) // HEAD_DIM
    return jnp.asarray(idx[:, None] == idx[None, :], BF)


def _compress_weights(w1):
    nl = w1.shape[0]
    per_pos = w1.reshape(nl, 1, CMP_BLOCK, 1, HEAD_DIM, CMP_HIDDEN)
    group_sel = jnp.eye(NSA_GROUPS, dtype=w1.dtype)
    big = per_pos * group_sel[None, :, None, :, None, None]
    return big.reshape(nl, NSA_GROUPS, CMP_BLOCK, KV_WIDTH, CMP_HIDDEN).astype(BF)


def _nsa_proj_weights(w_in):
    depth = w_in.shape[0]
    gate = w_in[:, :, OFF_GATE:OFF_AB].reshape(depth, D_MODEL, NSA_GROUPS, NSA_REP, 3)
    gate = jnp.swapaxes(gate, 3, 4).reshape(depth, D_MODEL, NSA_GROUPS, 3 * NSA_REP)
    gate = jnp.pad(gate, ((0, 0), (0, 0), (0, 0), (0, GATE_ROWS - 3 * NSA_REP)))
    gate = gate.reshape(depth, D_MODEL, NSA_GROUPS * GATE_ROWS)
    gate = jnp.pad(gate, ((0, 0), (0, 0), (0, LANES - NSA_GROUPS * GATE_ROWS)))
    return jnp.concatenate([w_in[:, :, OFF_Q:OFF_GATE], gate], axis=-1).astype(BF)


def _pe_rows(pe):
    nl = pe.shape[0]
    flat = pe.reshape(nl, 1, CMP_BLOCK * HEAD_DIM)
    return jnp.concatenate([flat, jnp.zeros((nl, 7, CMP_BLOCK * HEAD_DIM), pe.dtype)], axis=1).astype(BF)


def _nsa_params(seq, q_norm, k_norm, pe_k, w1_k, w2_k, pe_v, w1_v, w2_v):
    row = lambda a: a[:, None, :]
    p = _nsa_constants(seq)
    p.update(q_gain=row(jnp.tile(q_norm, (1, NSA_HEADS))), k_gain2=row(jnp.tile(k_norm, (1, NSA_GROUPS))),
             k_gain=row(k_norm), w1k=_compress_weights(w1_k), w1v=_compress_weights(w1_v),
             w1kf=w1_k.astype(BF), w1vf=w1_v.astype(BF), w2k=w2_k.astype(BF), w2v=w2_v.astype(BF),
             pek=_pe_rows(pe_k), pev=_pe_rows(pe_v),
             bd_q=_block_diag_ones(WIDTH), bd_k=_block_diag_ones(KV_WIDTH))
    return p


def _token_mixers(x3, mxn, sgu_p, ssd_p, conv_p, w_c, p, layer):
    y_a, y_b, y_d, *attn_in = _mixers(x3, mxn, sgu_p, ssd_p, conv_p, w_c, p, layer)
    y_c = _nsa_attn(*attn_in,
                    p["w1k"], p["w1kf"], p["w2k"], p["pek"], p["w1v"], p["w1vf"], p["w2v"], p["pev"], p["k_gain"],
                    p["cmp_const"], p["overlap_t"], p["win_mask"], p["diag_mask"], layer)
    return y_a, y_b, y_c, y_d


def kernel(x, ffn1_norm, ffn1_w_in, ffn1_w_out, mix_norm, w_in, sgu_v_norm, sgu_w, sgu_b, ssm_conv_w, ssm_conv_b, ssm_dt_bias, ssm_a_log, ssm_d, ssm_norm, nsa_q_norm, nsa_k_norm, nsa_pe_k, nsa_w1_k, nsa_w2_k, nsa_pe_v, nsa_w1_v, nsa_w2_v, conv_dw_w, conv_dw_b, conv_norm, w_branch, w_out, ffn2_norm, ffn2_w_in, ffn2_w_out):
    bsz, seq, _ = x.shape
    depth = w_in.shape[0]
    n_blk = seq // CHUNK
    row = lambda a: a[:, None, :]
    rep = lambda a, n: jnp.repeat(a, n, axis=-1)

    f1n, f2n, mxn = row(ffn1_norm), row(ffn2_norm), row(mix_norm)
    f1_in, f1_out = ffn1_w_in.astype(BF), ffn1_w_out.astype(BF)
    f2_in, f2_out = ffn2_w_in.astype(BF), ffn2_w_out.astype(BF)
    w_a = w_in[:, :, OFF_UV:OFF_Z].astype(BF)
    w_b = jnp.concatenate([w_in[:, :, OFF_Z:OFF_DT], rep(w_in[:, :, OFF_DT:OFF_Q], SSM_HEAD_DIM)], axis=-1).astype(BF)
    w_c = _nsa_proj_weights(w_in)
    w_d = w_in[:, :, OFF_AB:OFF_MERGE].astype(BF)
    w_g = w_in[:, :, OFF_MERGE:].astype(BF)
    sgu_bias = rep(jnp.swapaxes(sgu_b, 1, 2), WIDTH // SGU_GROUPS)
    dt_bias = row(rep(ssm_dt_bias, SSM_HEAD_DIM))
    a_neg = row(rep(-jnp.exp(ssm_a_log), SSM_HEAD_DIM))
    d_skip = row(rep(ssm_d, SSM_HEAD_DIM))
    wb, wo = w_branch.astype(BF), w_out.astype(BF)
    nsa_p = _nsa_params(seq, nsa_q_norm, nsa_k_norm, nsa_pe_k, nsa_w1_k, nsa_w2_k, nsa_pe_v, nsa_w1_v, nsa_w2_v)
    sgu_p = (w_a, row(sgu_v_norm), sgu_w, sgu_bias)
    ssd_p = (w_b, ssm_conv_w, row(ssm_conv_b), dt_bias, a_neg, d_skip, row(ssm_norm))
    conv_p = (w_d, conv_dw_w, row(conv_dw_b), row(conv_norm))

    x2 = x.reshape(bsz * seq, D_MODEL)
    for l in range(depth):
        x2 = _ffn(x2, f1n, f1_in, f1_out, l)
        x3 = x2.reshape(bsz, seq, D_MODEL)
        y_a, y_b, y_c, y_d = _token_mixers(x3, mxn, sgu_p, ssd_p, conv_p, w_c, nsa_p, l)
        flat = lambda y: y.reshape(bsz * seq, WIDTH)
        x2 = _merge(x2, flat(y_a), flat(y_b), flat(y_c), flat(y_d), mxn, w_g, wb, wo, l)
        x2 = _ffn(x2, f2n, f2_in, f2_out, l)
    return x2.reshape(bsz, seq, D_MODEL)
```

```python
import functools
import math

import numpy as np
import jax
import jax.numpy as jnp
from jax import lax
from jax.experimental import pallas as pl
from jax.experimental.pallas import tpu as pltpu

F32 = jnp.float32
BF = jnp.bfloat16

D_MODEL = 1024
FFN_DIM = 2816
WIDTH = 512
N_BRANCH = 4
SGU_GROUPS = 4
CHUNK = 128
SSM_HEADS = 8
SSM_HEAD_DIM = 64
SSM_GROUPS = 2
SSM_STATE = 128
SSM_CONV = 4
SSM_CONV_DIM = WIDTH + 2 * SSM_GROUPS * SSM_STATE
SSM_REP = SSM_HEADS // SSM_GROUPS
NSA_HEADS = 8
NSA_GROUPS = 2
NSA_REP = NSA_HEADS // NSA_GROUPS
HEAD_DIM = 64
KV_WIDTH = NSA_GROUPS * HEAD_DIM
CMP_BLOCK = 32
CMP_STRIDE = 16
CMP_HIDDEN = 128
SEL_BLOCK = 64
SEL_TOP_N = 8
WINDOW = 256
CONV_KERNEL = 31
EPS = 1e-6
NEG = -1e30
FORCE = 1e9

OFF_UV = 0
OFF_Z = OFF_UV + 2 * WIDTH
OFF_XBC = OFF_Z + WIDTH
OFF_DT = OFF_XBC + SSM_CONV_DIM
OFF_Q = OFF_DT + SSM_HEADS
OFF_KV = OFF_Q + WIDTH
OFF_GATE = OFF_KV + 6 * KV_WIDTH
OFF_AB = OFF_GATE + 3 * NSA_HEADS
OFF_MERGE = OFF_AB + 2 * WIDTH
IN_PROJ_DIM = OFF_MERGE + N_BRANCH * D_MODEL

LANES = 128
SUBLANES = 8
VMEM_LIMIT_BYTES = 56 * 1024 * 1024

TOKEN_TILE = 256
FFN_TILE = 512
CONV_HIST = 32
SSM_HIST = 8
SEL_KEYS = 256
ATTN_BLOCKS = SEL_KEYS // CHUNK
V_ROWS = 80
GATE_ROWS = 16


def _params(n_axes):
    return pltpu.CompilerParams(dimension_semantics=("arbitrary",) * n_axes,
                                vmem_limit_bytes=VMEM_LIMIT_BYTES)


def _rms(x, g):
    return x * lax.rsqrt(jnp.mean(x * x, axis=-1, keepdims=True) + EPS) * g


def _silu(x):
    return x * jax.nn.sigmoid(x)


def _gelu(x):
    return 0.5 * x * (1.0 + jnp.tanh(math.sqrt(2.0 / math.pi) * (x + 0.044715 * (x * x * x))))


def _dot(a, b):
    return jnp.dot(a, b, preferred_element_type=F32)


def _dot_nt(a, b):
    return lax.dot_general(a, b, (((1,), (1,)), ((), ())), preferred_element_type=F32)


def _dot_tn(a, b):
    return lax.dot_general(a, b, (((0,), (0,)), ((), ())), preferred_element_type=F32)


def _split3(x):
    hi = x.astype(BF)
    r = x - hi.astype(F32)
    mid = r.astype(BF)
    lo = (r - mid.astype(F32)).astype(BF)
    return hi, mid, lo


def _split2(x):
    hi = x.astype(BF)
    lo = (x - hi.astype(F32)).astype(BF)
    return hi, lo


def _layer_spec(shape, layer):
    nd = len(shape)
    return pl.BlockSpec((None,) + tuple(shape[1:]), lambda *_: (layer,) + (0,) * (nd - 1))


def _ffn_body(x_ref, g_ref, win_ref, wout_ref, o_ref, *, n_chunks):
    x = x_ref[...]
    hn = _rms(x, g_ref[...]).astype(BF)
    tf = FFN_DIM // n_chunks
    acc = jnp.zeros_like(x)
    for c in range(n_chunks):
        gate = _dot(hn, win_ref[:, c * tf:(c + 1) * tf])
        up = _dot(hn, win_ref[:, FFN_DIM + c * tf:FFN_DIM + (c + 1) * tf])
        a = (_silu(gate) * up).astype(BF)
        acc = acc + _dot(a, wout_ref[c * tf:(c + 1) * tf, :])
    o_ref[...] = x + 0.5 * acc


def _ffn(x2, norm, w_in, w_out, layer):
    t = x2.shape[0]
    return pl.pallas_call(
        functools.partial(_ffn_body, n_chunks=2),
        out_shape=jax.ShapeDtypeStruct(x2.shape, F32),
        grid=(t // FFN_TILE,),
        in_specs=[pl.BlockSpec((FFN_TILE, D_MODEL), lambda i: (i, 0)),
                  _layer_spec(norm.shape, layer),
                  _layer_spec(w_in.shape, layer),
                  _layer_spec(w_out.shape, layer)],
        out_specs=pl.BlockSpec((FFN_TILE, D_MODEL), lambda i: (i, 0)),
        compiler_params=_params(1),
        name="ffn",
    )(x2, norm, w_in, w_out)


def _sgu_part(hn, w_ref, vn_ref, ws_ref, bs_ref, o_ref):
    uv = _gelu(_dot(hn, w_ref[...]))
    u = uv[:, :WIDTH]
    v = _rms(uv[:, WIDTH:], vn_ref[...]).astype(BF)
    row = lax.broadcasted_iota(jnp.int32, (CHUNK, CHUNK), 0)
    col = lax.broadcasted_iota(jnp.int32, (CHUNK, CHUNK), 1)
    gw = WIDTH // SGU_GROUPS
    ws = [jnp.where(row >= col, ws_ref[g], 0.0).astype(BF) for g in range(SGU_GROUPS)]
    bias = bs_ref[...]
    for c in range(TOKEN_TILE // CHUNK):
        rows = slice(c * CHUNK, (c + 1) * CHUNK)
        mixed = jnp.concatenate(
            [_dot(ws[g], v[rows, g * gw:(g + 1) * gw]) for g in range(SGU_GROUPS)], axis=1)
        o_ref[rows, :] = (u[rows, :] * (mixed + bias)).astype(BF)


def _conv_part(hn, w_ref, dw_ref, db_ref, cn_ref, o_ref, buf_ref, shift_ref):
    ab = _dot(hn, w_ref[...])
    buf_ref[CONV_HIST:CONV_HIST + TOKEN_TILE, :] = ab[:, :WIDTH] * jax.nn.sigmoid(ab[:, WIDTH:])
    acc = jnp.zeros((TOKEN_TILE, WIDTH), F32) + db_ref[...]
    first = CONV_HIST - (CONV_KERNEL - 1)
    for res in range(min(SUBLANES, CONV_KERNEL)):
        n_taps = (CONV_KERNEL - 1 - res) // SUBLANES + 1
        rows = TOKEN_TILE + (n_taps - 1) * SUBLANES
        shift_ref[0:rows, :] = buf_ref[first + res:first + res + rows, :]
        for a in range(n_taps):
            k = a * SUBLANES + res
            acc = acc + dw_ref[k:k + 1, :] * shift_ref[a * SUBLANES:a * SUBLANES + TOKEN_TILE, :]
    o_ref[...] = _silu(_rms(acc, cn_ref[...])).astype(BF)
    buf_ref[0:CONV_HIST, :] = buf_ref[TOKEN_TILE:TOKEN_TILE + CONV_HIST, :]


def _ssd_part(hn, w_ref, cw_ref, cb_ref, dtb_ref, a_ref, dsk_ref, ng_ref, o_ref, buf_ref, st_ref):
    hw = SSM_REP * SSM_HEAD_DIM
    proj = _dot(hn, w_ref[...])
    z = proj[:, :WIDTH]
    buf_ref[SSM_HIST:SSM_HIST + TOKEN_TILE, :] = proj[:, WIDTH:WIDTH + SSM_CONV_DIM]
    dt = jax.nn.softplus(proj[:, WIDTH + SSM_CONV_DIM:] + dtb_ref[...])
    xbc = jnp.zeros((TOKEN_TILE, SSM_CONV_DIM), F32) + cb_ref[...]
    first = SSM_HIST - (SSM_CONV - 1)
    for k in range(SSM_CONV):
        xbc = xbc + cw_ref[k:k + 1, :] * buf_ref[first + k:first + k + TOKEN_TILE, :]
    buf_ref[0:SSM_HIST, :] = buf_ref[TOKEN_TILE:TOKEN_TILE + SSM_HIST, :]
    xbc = _silu(xbc)
    xs = xbc[:, :WIDTH]
    a_all = dt * a_ref[...]
    xdt = xs * dt

    row = lax.broadcasted_iota(jnp.int32, (CHUNK, CHUNK), 0)
    col = lax.broadcasted_iota(jnp.int32, (CHUNK, CHUNK), 1)
    causal = row >= col
    tril = jnp.where(causal, 1.0, 0.0).astype(BF)
    lane = lax.broadcasted_iota(jnp.int32, (CHUNK, 2 * SSM_HEAD_DIM), 1)
    low_half = lane < SSM_HEAD_DIM

    for c in range(TOKEN_TILE // CHUNK):
        rows = slice(c * CHUNK, (c + 1) * CHUNK)
        hi, mid, lo = _split3(a_all[rows, :])
        cs = _dot(tril, hi) + _dot(tril, mid) + _dot(tril, lo)
        cs_t = cs.T
        cs_last = cs[CHUNK - 1:CHUNK, :]
        xc = xdt[rows, :]
        x_decay = (xc * jnp.exp(cs_last - cs)).astype(BF)
        xc_bf = xc.astype(BF)
        grow = jnp.exp(cs)
        y_parts = []
        for g in range(SSM_GROUPS):
            bg = xbc[rows, WIDTH + g * SSM_STATE:WIDTH + (g + 1) * SSM_STATE].astype(BF)
            cg = xbc[rows, WIDTH + (SSM_GROUPS + g) * SSM_STATE:
                     WIDTH + (SSM_GROUPS + g + 1) * SSM_STATE].astype(BF)
            cb = _dot_nt(cg, bg)
            gl = slice(g * hw, (g + 1) * hw)
            y_off = _dot(cg, st_ref[g].astype(BF)) * grow[:, gl]
            y_diag = []
            for pair in range(SSM_REP // 2):
                scores = []
                for hh in range(2):
                    hc = g * hw + (2 * pair + hh) * SSM_HEAD_DIM
                    seg = cs[:, hc:hc + 1] - cs_t[hc:hc + 1, :]
                    decay = jnp.exp(jnp.where(causal, seg, NEG))
                    scores.append((cb * decay).astype(BF))
                xp = xc_bf[:, g * hw + pair * 2 * SSM_HEAD_DIM:g * hw + (pair + 1) * 2 * SSM_HEAD_DIM]
                zero = jnp.zeros_like(xp)
                rhs = jnp.concatenate([jnp.where(low_half, xp, zero),
                                       jnp.where(low_half, zero, xp)], axis=0)
                y_diag.append(_dot(jnp.concatenate(scores, axis=1), rhs))
            y_parts.append(jnp.concatenate(y_diag, axis=1) + y_off)
            st_ref[g] = jnp.exp(cs_last[:, gl]) * st_ref[g] + _dot_tn(bg, x_decay[:, gl])
        y = jnp.concatenate(y_parts, axis=1) + xs[rows, :] * dsk_ref[...]
        y = y * _silu(z[rows, :])
        o_ref[rows, :] = _rms(y, ng_ref[...]).astype(BF)


def _head_rms(x, ones_blockdiag, gain, scale):
    hi, lo = _split2(x * x)
    ss = _dot(hi, ones_blockdiag) + _dot(lo, ones_blockdiag)
    return x * lax.rsqrt(ss * (1.0 / HEAD_DIM) + EPS) * (gain * scale)


def _nsa_proj_part(hn, w_ref, qn_ref, kn_ref, bdq_ref, bdk_ref, qc_ref, ksc_ref, kwc_ref, vcn_ref,
                   qt_ref, gate_ref, kc_ref, vc_ref, ks_ref, vst_ref, kw_ref, vwt_ref):
    proj = _dot(hn, w_ref[...])
    o = WIDTH
    kc_ref[...] = proj[:, o:o + KV_WIDTH]
    vc_ref[...] = proj[:, o + KV_WIDTH:o + 2 * KV_WIDTH]
    q = _head_rms(proj[:, :WIDTH], bdq_ref[...], qn_ref[...], HEAD_DIM ** -0.5)
    ks = _head_rms(proj[:, o + 2 * KV_WIDTH:o + 3 * KV_WIDTH], bdk_ref[...], kn_ref[...], 1.0)
    kw = _head_rms(proj[:, o + 4 * KV_WIDTH:o + 5 * KV_WIDTH], bdk_ref[...], kn_ref[...], 1.0)
    vs = proj[:, o + 3 * KV_WIDTH:o + 4 * KV_WIDTH]
    vw = proj[:, o + 5 * KV_WIDTH:o + 6 * KV_WIDTH]
    gate_t = jax.nn.sigmoid(proj[:, o + 6 * KV_WIDTH:]).T
    lane = lax.broadcasted_iota(jnp.int32, (TOKEN_TILE, KV_WIDTH), 1)
    gw = NSA_REP * HEAD_DIM
    for g in range(NSA_GROUPS):
        own = (lane >= g * HEAD_DIM) & (lane < (g + 1) * HEAD_DIM)
        ks_ref[g] = (jnp.where(own, ks, 0.0) + ksc_ref[g]).astype(BF)
        kw_ref[g] = (jnp.where(own, kw, 0.0) + kwc_ref[g]).astype(BF)
    for c in range(TOKEN_TILE // CHUNK):
        rows = slice(c * CHUNK, (c + 1) * CHUNK)
        vs_t = vs[rows, :].T.astype(BF)
        vw_t = vw[rows, :].T.astype(BF)
        for g in range(NSA_GROUPS):
            feat = slice(g * HEAD_DIM, (g + 1) * HEAD_DIM)
            rest = slice((1 - g) * HEAD_DIM, (2 - g) * HEAD_DIM)
            q_t = q[rows, g * gw:(g + 1) * gw].T
            qt_ref[g, c, feat, :] = jnp.concatenate(
                [q_t[r * HEAD_DIM:(r + 1) * HEAD_DIM] for r in range(NSA_REP)], axis=1).astype(BF)
            qt_ref[g, c, rest, :] = qc_ref[g]
            gate_ref[g, c] = gate_t[g * GATE_ROWS:(g + 1) * GATE_ROWS, rows]
            vst_ref[g, c, 0:HEAD_DIM, :] = vs_t[feat]
            vst_ref[g, c, HEAD_DIM:V_ROWS, :] = vcn_ref[...]
            vwt_ref[g, c, 0:HEAD_DIM, :] = vw_t[feat]
            vwt_ref[g, c, HEAD_DIM:V_ROWS, :] = vcn_ref[...]


N_SGU_IN, N_SSD_IN, N_CONV_IN, N_NSA_IN = 4, 7, 4, 9


def _mixers_body(x_ref, g_ref, *refs):
    n_in = N_SGU_IN + N_SSD_IN + N_CONV_IN + N_NSA_IN
    ins, outs, scratch = refs[:n_in], refs[n_in:n_in + 11], refs[n_in + 11:]
    sgu_in, ins = ins[:N_SGU_IN], ins[N_SGU_IN:]
    ssd_in, ins = ins[:N_SSD_IN], ins[N_SSD_IN:]
    conv_in, nsa_in = ins[:N_CONV_IN], ins[N_CONV_IN:]
    ya_ref, yb_ref, yd_ref = outs[:3]
    conv_buf, conv_shift, ssd_buf, ssd_state = scratch

    @pl.when(pl.program_id(1) == 0)
    def _():
        conv_buf[0:CONV_HIST, :] = jnp.zeros((CONV_HIST, WIDTH), F32)
        ssd_buf[0:SSM_HIST, :] = jnp.zeros((SSM_HIST, SSM_CONV_DIM), F32)
        ssd_state[...] = jnp.zeros(ssd_state.shape, F32)

    hn = _rms(x_ref[...], g_ref[...]).astype(BF)
    _sgu_part(hn, *sgu_in, ya_ref)
    _ssd_part(hn, *ssd_in, yb_ref, ssd_buf, ssd_state)
    _conv_part(hn, *conv_in, yd_ref, conv_buf, conv_shift)
    _nsa_proj_part(hn, *nsa_in, *outs[3:])


def _mixers(x3, norm, sgu_p, ssd_p, conv_p, w_c, p, layer):
    b, s, _ = x3.shape
    g, nq, blocks = NSA_GROUPS, NSA_REP * CHUNK, TOKEN_TILE // CHUNK
    n_blk = s // CHUNK
    tile = lambda n: pl.BlockSpec((None, TOKEN_TILE, n), lambda i, j: (i, j, 0))
    const = lambda a: pl.BlockSpec(a.shape, lambda i, j: (0,) * a.ndim)
    lay = lambda a: _layer_spec(a.shape, layer)
    key_const = lambda a: pl.BlockSpec((g, TOKEN_TILE, LANES), lambda i, j: (0, j, 0))
    per_blk = lambda rows, lanes: pl.BlockSpec((None, g, blocks, rows, lanes), lambda i, j: (i, 0, j, 0, 0))
    keys = pl.BlockSpec((None, g, TOKEN_TILE, LANES), lambda i, j: (i, 0, j, 0))
    y = jax.ShapeDtypeStruct((b, s, WIDTH), BF)
    raw = jax.ShapeDtypeStruct((b, s, KV_WIDTH), F32)
    k_aug = jax.ShapeDtypeStruct((b, g, s, LANES), BF)
    v_t = jax.ShapeDtypeStruct((b, g, n_blk, V_ROWS, CHUNK), BF)
    assert (len(sgu_p), len(ssd_p), len(conv_p)) == (N_SGU_IN, N_SSD_IN, N_CONV_IN)
    nsa_layer = (w_c, p["q_gain"], p["k_gain2"])
    nsa_const = (p["bd_q"], p["bd_k"], p["q_const"])
    layered = tuple(sgu_p) + tuple(ssd_p) + tuple(conv_p) + nsa_layer
    return pl.pallas_call(
        _mixers_body,
        out_shape=(y, y, y,
                   jax.ShapeDtypeStruct((b, g, n_blk, LANES, nq), BF),
                   jax.ShapeDtypeStruct((b, g, n_blk, GATE_ROWS, CHUNK), F32),
                   raw, raw, k_aug, v_t, k_aug, v_t),
        grid=(b, s // TOKEN_TILE),
        in_specs=[tile(D_MODEL), lay(norm)] + [lay(a) for a in layered] + [const(a) for a in nsa_const]
        + [key_const(p["k_sel"]), key_const(p["k_win"]), const(p["v_const"])],
        out_specs=(tile(WIDTH),) * 3
        + (per_blk(LANES, nq), per_blk(GATE_ROWS, CHUNK), tile(KV_WIDTH), tile(KV_WIDTH),
           keys, per_blk(V_ROWS, CHUNK), keys, per_blk(V_ROWS, CHUNK)),
        scratch_shapes=[pltpu.VMEM((TOKEN_TILE + CONV_HIST, WIDTH), F32),
                        pltpu.VMEM((TOKEN_TILE + CONV_HIST, WIDTH), F32),
                        pltpu.VMEM((TOKEN_TILE + SSM_HIST, SSM_CONV_DIM), F32),
                        pltpu.VMEM((SSM_GROUPS, SSM_STATE, SSM_REP * SSM_HEAD_DIM), F32)],
        compiler_params=_params(2),
        name="mixers",
    )(x3, norm, *layered, *nsa_const, p["k_sel"], p["k_win"], p["v_const"])


def _nsa_attn_body(qt_ref, gate_ref, kc_ref, vc_ref, ks_ref, vst_ref, kw_ref, vwt_ref,
                   w1k_ref, w1kf_ref, w2k_ref, pek_ref, w1v_ref, w1vf_ref, w2v_ref, pev_ref, kn_ref,
                   cpos_ref, ovt_ref, wmask_ref, dmask_ref,
                   o_ref, qs_ref, kcmp_ref, vcmpt_ref, m_ref, acc_ref, part_ref, s_ref):
    step = pl.program_id(1)
    nq = NSA_REP * CHUNK
    n_sel = ovt_ref.shape[0]
    groups = range(NSA_GROUPS)
    q_blocks = range(ATTN_BLOCKS)

    def compress(x_ref, w1_ref, pe_ref, w1f_ref, w2_ref):
        pe_bias = _dot(pe_ref[...], w1f_ref[...])[0:1, :]
        head = [jnp.zeros((CHUNK, CMP_HIDDEN), F32) for _ in groups]
        tail = [jnp.zeros((CHUNK, CMP_HIDDEN), F32) for _ in groups]
        for i in range(CMP_STRIDE):
            x = x_ref[pl.ds(i, CHUNK, stride=CMP_STRIDE), :].astype(BF)
            for g in groups:
                head[g] = head[g] + _dot(x, w1_ref[g, i])
                tail[g] = tail[g] + _dot(x, w1_ref[g, CMP_STRIDE + i])
        return [_dot(_gelu(head[g] + pltpu.roll(tail[g], CHUNK - 1, 0) + pe_bias).astype(BF), w2_ref[...])
                for g in groups]

    @pl.when(step == 0)
    def _():
        k_cmp = compress(kc_ref, w1k_ref, pek_ref, w1kf_ref, w2k_ref)
        v_cmp = compress(vc_ref, w1v_ref, pev_ref, w1vf_ref, w2v_ref)
        lane = lax.broadcasted_iota(jnp.int32, (CHUNK, LANES - HEAD_DIM), 1)
        ones_col = jnp.where(lane == 0, 1.0, 0.0)
        for g in groups:
            halves = [_rms(k_cmp[g], kn_ref[...]).astype(BF), cpos_ref[...]]
            kcmp_ref[g] = jnp.concatenate(halves if g == 0 else halves[::-1], axis=1)
            vcmpt_ref[g] = jnp.concatenate([v_cmp[g], ones_col], axis=1).T[0:V_ROWS, :].astype(BF)

    def q_index(rows):
        return lax.broadcasted_iota(jnp.int32, (rows, nq), 1) % CHUNK

    def k_index(rows):
        return lax.broadcasted_iota(jnp.int32, (rows, nq), 0)

    def select_blocks(p_c, t0):
        p_sum = p_c[:, 0:CHUNK]
        for r in range(1, NSA_REP):
            p_sum = p_sum + p_c[:, r * CHUNK:(r + 1) * CHUNK]
        p_hi, p_lo = _split2(p_sum)
        imp = _dot(ovt_ref[...], p_hi) + _dot(ovt_ref[...], p_lo)
        blk = lax.broadcasted_iota(jnp.int32, (n_sel, CHUNK), 0)
        tq = t0 + lax.broadcasted_iota(jnp.int32, (n_sel, CHUNK), 1)
        cur = tq // SEL_BLOCK
        forced = (blk == 0) | (blk == cur) | (blk == cur - 1)
        score = jnp.where(forced, FORCE, jnp.where(blk * SEL_BLOCK <= tq, imp, NEG))
        blk_f = blk.astype(F32)
        chosen = jnp.zeros((n_sel, CHUNK), F32)
        for _ in range(SEL_TOP_N):
            best = jnp.max(score, axis=0, keepdims=True)
            first = jnp.min(jnp.where(score == best, blk_f, 1e9), axis=0, keepdims=True)
            pick = blk_f == first
            chosen = jnp.where(pick, 1.0, chosen)
            score = jnp.where(pick, -3e38, score)
        return ((chosen - 1.0) * (-NEG)).astype(BF)

    n_w = WINDOW + CHUNK
    sub_blocks = SEL_KEYS // CHUNK

    def gate_row(g, qb, branch):
        rows = gate_ref[g, qb]
        return jnp.concatenate([rows[branch * NSA_REP + r:branch * NSA_REP + r + 1] for r in range(NSA_REP)], axis=1)

    for g in groups:
        for qb in q_blocks:
            j = step * ATTN_BLOCKS + qb
            t0 = j * CHUNK
            lanes = slice(qb * nq, (qb + 1) * nq)
            qt = qt_ref[g, qb]
            slot = (1 - g) * HEAD_DIM
            ci = k_index(CHUNK)
            valid_c = (ci * CMP_STRIDE + (CMP_BLOCK - 1) <= t0 + q_index(CHUNK)) & (ci < CHUNK - 1)
            s_c = jnp.where(valid_c, _dot(kcmp_ref[g], qt), NEG)
            e_c = jnp.where(valid_c, jnp.exp(s_c - jnp.max(s_c, axis=0, keepdims=True)), 0.0)
            p_c = e_c / jnp.maximum(jnp.sum(e_c, axis=0, keepdims=True), 1e-30)
            o_c = _dot(vcmpt_ref[g], p_c.astype(BF))[0:HEAD_DIM]
            qs_ref[g, :, lanes] = qt
            qs_ref[g, slot:slot + n_sel, lanes] = jnp.concatenate([select_blocks(p_c, t0)] * NSA_REP, axis=1)
            w_blk = jnp.maximum(j - WINDOW // CHUNK, 0)
            w_start = pl.multiple_of(w_blk * CHUNK, CHUNK)
            s_w = _dot(kw_ref[g, pl.ds(w_start, n_w), :], qt) + wmask_ref[j - w_blk]
            p_w = jnp.exp(s_w - jnp.max(s_w, axis=0, keepdims=True)).astype(BF)
            o_w = _dot(vwt_ref[g, w_blk], p_w[0:CHUNK])
            for sub in range(1, n_w // CHUNK):
                o_w = o_w + _dot(vwt_ref[g, w_blk + sub], p_w[sub * CHUNK:(sub + 1) * CHUNK])
            part_ref[g, :, lanes] = (gate_row(g, qb, 0) * o_c
                                     + gate_row(g, qb, 2) * (o_w[0:HEAD_DIM] / o_w[HEAD_DIM:HEAD_DIM + 1]))

    m_ref[...] = jnp.full(m_ref.shape, NEG, F32)
    acc_ref[...] = jnp.zeros(acc_ref.shape, F32)

    def scores(g, blk0):
        start = pl.multiple_of(blk0 * CHUNK, SEL_KEYS)
        return _dot(ks_ref[g, pl.ds(start, SEL_KEYS), :], qs_ref[g])

    def absorb(g, s, blk0):
        m_old = m_ref[g]
        m_new = jnp.maximum(m_old, jnp.max(s, axis=0, keepdims=True))
        p = jnp.exp(s - m_new).astype(BF)
        pv = _dot(vst_ref[g, blk0], p[0:CHUNK])
        for sub in range(1, sub_blocks):
            pv = pv + _dot(vst_ref[g, blk0 + sub], p[sub * CHUNK:(sub + 1) * CHUNK])
        acc_ref[g] = jnp.exp(m_old - m_new) * acc_ref[g] + pv
        m_ref[g] = m_new

    for g in groups:
        s_ref[g] = scores(g, 0)

    def loop_body(i, carry):
        for g in groups:
            s = s_ref[g]
            s_ref[g] = scores(g, (i + 1) * sub_blocks)
            absorb(g, s, i * sub_blocks)
        return carry

    lax.fori_loop(0, step, loop_body, 0)
    for g in groups:
        absorb(g, s_ref[g] + dmask_ref[...], step * sub_blocks)
        acc = acc_ref[g]
        o_s = acc[0:HEAD_DIM] / acc[HEAD_DIM:HEAD_DIM + 1]
        gw = NSA_REP * HEAD_DIM
        for qb in q_blocks:
            lanes = slice(qb * nq, (qb + 1) * nq)
            out_t = part_ref[g, :, lanes] + gate_row(g, qb, 1) * o_s[:, lanes]
            by_head = jnp.concatenate([out_t[:, r * CHUNK:(r + 1) * CHUNK] for r in range(NSA_REP)], axis=0)
            o_ref[qb * CHUNK:(qb + 1) * CHUNK, g * gw:(g + 1) * gw] = by_head.T.astype(BF)


def _nsa_attn(qt_aug, gates_t, kc, vc, ks_aug, vs_t, kw_aug, vw_t,
              w1k, w1kf, w2k, pek, w1v, w1vf, w2v, pev, k_gain, cpos, ov_t, win_mask, diag_mask, layer):
    b, g, nblk, _, nq = qt_aug.shape
    lanes = ATTN_BLOCKS * nq
    blk = lambda a: pl.BlockSpec((None, g, ATTN_BLOCKS) + a.shape[3:], lambda i, j: (i, 0, j, 0, 0))
    seq = lambda a: pl.BlockSpec((None,) + a.shape[1:], lambda i, j: (i,) + (0,) * (a.ndim - 1))
    lay = lambda a: pl.BlockSpec((None,) + a.shape[1:], lambda i, j: (layer,) + (0,) * (a.ndim - 1))
    const = lambda a: pl.BlockSpec(a.shape, lambda i, j: (0,) * a.ndim)
    return pl.pallas_call(
        _nsa_attn_body,
        out_shape=jax.ShapeDtypeStruct((b, nblk * CHUNK, g * NSA_REP * HEAD_DIM), BF),
        grid=(b, nblk // ATTN_BLOCKS),
        in_specs=[blk(qt_aug), blk(gates_t), seq(kc), seq(vc),
                  seq(ks_aug), seq(vs_t), seq(kw_aug), seq(vw_t),
                  lay(w1k), lay(w1kf), lay(w2k), lay(pek),
                  lay(w1v), lay(w1vf), lay(w2v), lay(pev), lay(k_gain),
                  const(cpos), const(ov_t), const(win_mask), const(diag_mask)],
        out_specs=pl.BlockSpec((None, ATTN_BLOCKS * CHUNK, g * NSA_REP * HEAD_DIM), lambda i, j: (i, j, 0)),
        scratch_shapes=[pltpu.VMEM((g, LANES, lanes), BF),
                        pltpu.VMEM((g, CHUNK, LANES), BF),
                        pltpu.VMEM((g, V_ROWS, CHUNK), BF),
                        pltpu.VMEM((g, 1, lanes), F32),
                        pltpu.VMEM((g, V_ROWS, lanes), F32),
                        pltpu.VMEM((g, HEAD_DIM, lanes), F32),
                        pltpu.VMEM((g, SEL_KEYS, lanes), F32)],
        compiler_params=_params(2),
        name="nsa_attn",
    )(qt_aug, gates_t, kc, vc, ks_aug, vs_t, kw_aug, vw_t,
      w1k, w1kf, w2k, pek, w1v, w1vf, w2v, pev, k_gain, cpos, ov_t, win_mask, diag_mask)


def _merge_body(x_ref, ya_ref, yb_ref, yc_ref, yd_ref, g_ref, wg_ref, wb_ref, wo_ref, o_ref):
    x = x_ref[...]
    hn = _rms(x, g_ref[...]).astype(BF)
    merged = jnp.zeros_like(x)
    for i, y_ref in enumerate((ya_ref, yb_ref, yc_ref, yd_ref)):
        gate = jax.nn.sigmoid(_dot(hn, wg_ref[:, i * D_MODEL:(i + 1) * D_MODEL]))
        merged = merged + gate * _dot(y_ref[...], wb_ref[i])
    o_ref[...] = x + _dot(merged.astype(BF), wo_ref[...])


def _merge(x2, ya, yb, yc, yd, norm, w_gate, w_branch, w_out, layer):
    t = x2.shape[0]
    tile = lambda n: pl.BlockSpec((FFN_TILE, n), lambda i: (i, 0))
    return pl.pallas_call(
        _merge_body,
        out_shape=jax.ShapeDtypeStruct(x2.shape, F32),
        grid=(t // FFN_TILE,),
        in_specs=[tile(D_MODEL)] + [tile(WIDTH)] * 4
        + [_layer_spec(a.shape, layer) for a in (norm, w_gate, w_branch, w_out)],
        out_specs=tile(D_MODEL),
        compiler_params=_params(1),
        name="merge",
    )(x2, ya, yb, yc, yd, norm, w_gate, w_branch, w_out)


def _alibi_slopes():
    return np.array([2.0 ** (-8.0 * (i + 1) / NSA_HEADS) for i in range(NSA_HEADS)], np.float32)


def _nsa_constants(seq):
    n_sel = seq // SEL_BLOCK
    n_cmp = (seq - CMP_BLOCK) // CMP_STRIDE + 1
    assert n_cmp == CHUNK - 1 and n_sel <= SEL_BLOCK // 2
    pos_cols = LANES - HEAD_DIM - n_sel
    slopes = _alibi_slopes().reshape(NSA_GROUPS, NSA_REP)
    q_const = np.zeros((NSA_GROUPS, LANES - HEAD_DIM, NSA_REP * CHUNK), np.float32)
    lane_slope = np.repeat(slopes, CHUNK, axis=1)
    q_const[:, n_sel, :] = lane_slope * SEL_BLOCK
    q_const[:, n_sel + 1, :] = lane_slope
    pos = np.arange(seq)
    pos_part = np.zeros((seq, pos_cols), np.float32)
    pos_part[:, 0] = pos // SEL_BLOCK
    pos_part[:, 1] = pos % SEL_BLOCK
    onehot = (pos[:, None] // SEL_BLOCK == np.arange(n_sel)[None, :]).astype(np.float32)

    def key_const(first):
        half = np.concatenate([first, pos_part], axis=1)
        return np.stack([np.concatenate([np.zeros_like(half), half][::1 if g == 0 else -1], axis=1)
                         for g in range(NSA_GROUPS)])

    k_sel_const = key_const(onehot)
    k_win_const = key_const(np.zeros_like(onehot))
    v_const = np.zeros((V_ROWS - HEAD_DIM, CHUNK), np.float32)
    v_const[0, :] = 1.0
    c_mid = np.arange(CHUNK) * CMP_STRIDE + (CMP_BLOCK - 1) / 2.0
    cmp_const = np.zeros((CHUNK, LANES - HEAD_DIM), np.float32)
    cmp_const[:, n_sel] = c_mid // SEL_BLOCK
    cmp_const[:, n_sel + 1] = c_mid % SEL_BLOCK
    c_start = np.arange(CHUNK) * CMP_STRIDE
    s_start = np.arange(n_sel) * SEL_BLOCK
    overlap_t = ((c_start[None, :] <= s_start[:, None] + SEL_BLOCK - 1)
                 & (c_start[None, :] + CMP_BLOCK - 1 >= s_start[:, None])
                 & (np.arange(CHUNK)[None, :] < n_cmp)).astype(np.float32)
    qi = np.arange(NSA_REP * CHUNK)[None, :] % CHUNK

    def mask(rows, n_cases, valid):
        ki = np.arange(rows)[:, None]
        return np.stack([np.where(valid(d * CHUNK + qi - ki), 0.0, NEG) for d in range(n_cases)]).astype(np.float32)

    win_mask = mask(WINDOW + CHUNK, WINDOW // CHUNK + 1, lambda rel: (rel >= 0) & (rel < WINDOW))
    diag_mask = np.concatenate(list(mask(SEL_KEYS, ATTN_BLOCKS, lambda rel: rel >= 0)), axis=1)
    bf = lambda a: jnp.asarray(a, BF)
    return dict(q_const=bf(q_const), k_sel=bf(k_sel_const), k_win=bf(k_win_const), v_const=bf(v_const),
                cmp_const=bf(cmp_const), overlap_t=bf(overlap_t),
                win_mask=jnp.asarray(win_mask), diag_mask=jnp.asarray(diag_mask))


def _block_diag_ones(n):
    idx = np.arange(n) // HEAD_DIM
    return jnp.asarray(idx[:, None] == idx[None, :], BF)


def _compress_weights(w1):
    nl = w1.shape[0]
    per_pos = w1.reshape(nl, 1, CMP_BLOCK, 1, HEAD_DIM, CMP_HIDDEN)
    group_sel = jnp.eye(NSA_GROUPS, dtype=w1.dtype)
    big = per_pos * group_sel[None, :, None, :, None, None]
    return big.reshape(nl, NSA_GROUPS, CMP_BLOCK, KV_WIDTH, CMP_HIDDEN).astype(BF)


def _nsa_proj_weights(w_in):
    depth = w_in.shape[0]
    gate = w_in[:, :, OFF_GATE:OFF_AB].reshape(depth, D_MODEL, NSA_GROUPS, NSA_REP, 3)
    gate = jnp.swapaxes(gate, 3, 4).reshape(depth, D_MODEL, NSA_GROUPS, 3 * NSA_REP)
    gate = jnp.pad(gate, ((0, 0), (0, 0), (0, 0), (0, GATE_ROWS - 3 * NSA_REP)))
    gate = gate.reshape(depth, D_MODEL, NSA_GROUPS * GATE_ROWS)
    gate = jnp.pad(gate, ((0, 0), (0, 0), (0, LANES - NSA_GROUPS * GATE_ROWS)))
    return jnp.concatenate([w_in[:, :, OFF_Q:OFF_GATE], gate], axis=-1).astype(BF)


def _pe_rows(pe):
    nl = pe.shape[0]
    flat = pe.reshape(nl, 1, CMP_BLOCK * HEAD_DIM)
    return jnp.concatenate([flat, jnp.zeros((nl, 7, CMP_BLOCK * HEAD_DIM), pe.dtype)], axis=1).astype(BF)


def _nsa_params(seq, q_norm, k_norm, pe_k, w1_k, w2_k, pe_v, w1_v, w2_v):
    row = lambda a: a[:, None, :]
    p = _nsa_constants(seq)
    p.update(q_gain=row(jnp.tile(q_norm, (1, NSA_HEADS))), k_gain2=row(jnp.tile(k_norm, (1, NSA_GROUPS))),
             k_gain=row(k_norm), w1k=_compress_weights(w1_k), w1v=_compress_weights(w1_v),
             w1kf=w1_k.astype(BF), w1vf=w1_v.astype(BF), w2k=w2_k.astype(BF), w2v=w2_v.astype(BF),
             pek=_pe_rows(pe_k), pev=_pe_rows(pe_v),
             bd_q=_block_diag_ones(WIDTH), bd_k=_block_diag_ones(KV_WIDTH))
    return p


def _token_mixers(x3, mxn, sgu_p, ssd_p, conv_p, w_c, p, layer):
    y_a, y_b, y_d, *attn_in = _mixers(x3, mxn, sgu_p, ssd_p, conv_p, w_c, p, layer)
    y_c = _nsa_attn(*attn_in,
                    p["w1k"], p["w1kf"], p["w2k"], p["pek"], p["w1v"], p["w1vf"], p["w2v"], p["pev"], p["k_gain"],
                    p["cmp_const"], p["overlap_t"], p["win_mask"], p["diag_mask"], layer)
    return y_a, y_b, y_c, y_d


def kernel(x, ffn1_norm, ffn1_w_in, ffn1_w_out, mix_norm, w_in, sgu_v_norm, sgu_w, sgu_b, ssm_conv_w, ssm_conv_b, ssm_dt_bias, ssm_a_log, ssm_d, ssm_norm, nsa_q_norm, nsa_k_norm, nsa_pe_k, nsa_w1_k, nsa_w2_k, nsa_pe_v, nsa_w1_v, nsa_w2_v, conv_dw_w, conv_dw_b, conv_norm, w_branch, w_out, ffn2_norm, ffn2_w_in, ffn2_w_out):
    bsz, seq, _ = x.shape
    depth = w_in.shape[0]
    n_blk = seq // CHUNK
    row = lambda a: a[:, None, :]
    rep = lambda a, n: jnp.repeat(a, n, axis=-1)

    f1n, f2n, mxn = row(ffn1_norm), row(ffn2_norm), row(mix_norm)
    f1_in, f1_out = ffn1_w_in.astype(BF), ffn1_w_out.astype(BF)
    f2_in, f2_out = ffn2_w_in.astype(BF), ffn2_w_out.astype(BF)
    w_a = w_in[:, :, OFF_UV:OFF_Z].astype(BF)
    w_b = jnp.concatenate([w_in[:, :, OFF_Z:OFF_DT], rep(w_in[:, :, OFF_DT:OFF_Q], SSM_HEAD_DIM)], axis=-1).astype(BF)
    w_c = _nsa_proj_weights(w_in)
    w_d = w_in[:, :, OFF_AB:OFF_MERGE].astype(BF)
    w_g = w_in[:, :, OFF_MERGE:].astype(BF)
    sgu_bias = rep(jnp.swapaxes(sgu_b, 1, 2), WIDTH // SGU_GROUPS)
    dt_bias = row(rep(ssm_dt_bias, SSM_HEAD_DIM))
    a_neg = row(rep(-jnp.exp(ssm_a_log), SSM_HEAD_DIM))
    d_skip = row(rep(ssm_d, SSM_HEAD_DIM))
    wb, wo = w_branch.astype(BF), w_out.astype(BF)
    nsa_p = _nsa_params(seq, nsa_q_norm, nsa_k_norm, nsa_pe_k, nsa_w1_k, nsa_w2_k, nsa_pe_v, nsa_w1_v, nsa_w2_v)
    sgu_p = (w_a, row(sgu_v_norm), sgu_w, sgu_bias)
    ssd_p = (w_b, ssm_conv_w, row(ssm_conv_b), dt_bias, a_neg, d_skip, row(ssm_norm))
    conv_p = (w_d, conv_dw_w, row(conv_dw_b), row(conv_norm))

    x2 = x.reshape(bsz * seq, D_MODEL)
    for l in range(depth):
        x2 = _ffn(x2, f1n, f1_in, f1_out, l)
        x3 = x2.reshape(bsz, seq, D_MODEL)
        y_a, y_b, y_c, y_d = _token_mixers(x3, mxn, sgu_p, ssd_p, conv_p, w_c, nsa_p, l)
        flat = lambda y: y.reshape(bsz * seq, WIDTH)
        x2 = _merge(x2, flat(y_a), flat(y_b), flat(y_c), flat(y_d), mxn, w_g, wb, wo, l)
        x2 = _ffn(x2, f2n, f2_in, f2_out, l)
    return x2.reshape(bsz, seq, D_MODEL)
```

```python
import functools
import math

import numpy as np
import jax
import jax.numpy as jnp
from jax import lax
from jax.experimental import pallas as pl
from jax.experimental.pallas import tpu as pltpu

F32 = jnp.float32
BF = jnp.bfloat16

D_MODEL = 1024
FFN_DIM = 2816
WIDTH = 512
N_BRANCH = 4
SGU_GROUPS = 4
CHUNK = 128
SSM_HEADS = 8
SSM_HEAD_DIM = 64
SSM_GROUPS = 2
SSM_STATE = 128
SSM_CONV = 4
SSM_CONV_DIM = WIDTH + 2 * SSM_GROUPS * SSM_STATE
SSM_REP = SSM_HEADS // SSM_GROUPS
NSA_HEADS = 8
NSA_GROUPS = 2
NSA_REP = NSA_HEADS // NSA_GROUPS
HEAD_DIM = 64
KV_WIDTH = NSA_GROUPS * HEAD_DIM
CMP_BLOCK = 32
CMP_STRIDE = 16
CMP_HIDDEN = 128
SEL_BLOCK = 64
SEL_TOP_N = 8
WINDOW = 256
CONV_KERNEL = 31
EPS = 1e-6
NEG = -1e30
FORCE = 1e9

OFF_UV = 0
OFF_Z = OFF_UV + 2 * WIDTH
OFF_XBC = OFF_Z + WIDTH
OFF_DT = OFF_XBC + SSM_CONV_DIM
OFF_Q = OFF_DT + SSM_HEADS
OFF_KV = OFF_Q + WIDTH
OFF_GATE = OFF_KV + 6 * KV_WIDTH
OFF_AB = OFF_GATE + 3 * NSA_HEADS
OFF_MERGE = OFF_AB + 2 * WIDTH
IN_PROJ_DIM = OFF_MERGE + N_BRANCH * D_MODEL

LANES = 128
SUBLANES = 8
MXU_DIM = 256
VMEM_LIMIT_BYTES = 56 * 1024 * 1024

TOKEN_TILE = 256
FFN_TILE = 512
CONV_HIST = 32
SSM_HIST = 8
SEL_KEYS = 256
ATTN_BLOCKS = SEL_KEYS // CHUNK
V_ROWS = 80
GATE_ROWS = 16


def _params(n_axes):
    return pltpu.CompilerParams(dimension_semantics=("arbitrary",) * n_axes,
                                vmem_limit_bytes=VMEM_LIMIT_BYTES)


def _rms(x, g):
    return x * lax.rsqrt(jnp.mean(x * x, axis=-1, keepdims=True) + EPS) * g


def _silu(x):
    return x * jax.nn.sigmoid(x)


def _gelu(x):
    return 0.5 * x * (1.0 + jnp.tanh(math.sqrt(2.0 / math.pi) * (x + 0.044715 * (x * x * x))))


def _dot(a, b):
    return jnp.dot(a, b, preferred_element_type=F32)


def _dot_nt(a, b):
    return lax.dot_general(a, b, (((1,), (1,)), ((), ())), preferred_element_type=F32)


def _dot_tn(a, b):
    return lax.dot_general(a, b, (((0,), (0,)), ((), ())), preferred_element_type=F32)


def _split3(x):
    hi = x.astype(BF)
    r = x - hi.astype(F32)
    mid = r.astype(BF)
    lo = (r - mid.astype(F32)).astype(BF)
    return hi, mid, lo


def _split2(x):
    hi = x.astype(BF)
    lo = (x - hi.astype(F32)).astype(BF)
    return hi, lo


def _layer_spec(shape, layer):
    nd = len(shape)
    return pl.BlockSpec((None,) + tuple(shape[1:]), lambda *_: (layer,) + (0,) * (nd - 1))


def _ffn_chunks():
    tiles = FFN_DIM // MXU_DIM
    assert tiles * MXU_DIM == FFN_DIM
    cut = (tiles // 2) * MXU_DIM
    return ((0, cut), (cut, FFN_DIM))


def _ffn_body(x_ref, g_ref, win_ref, wout_ref, o_ref):
    x = x_ref[...]
    hn = _rms(x, g_ref[...]).astype(win_ref.dtype)
    acc = jnp.zeros_like(x)
    for lo, hi in _ffn_chunks():
        gate = _dot(hn, win_ref[:, lo:hi])
        up = _dot(hn, win_ref[:, FFN_DIM + lo:FFN_DIM + hi])
        a = (_silu(gate) * up).astype(wout_ref.dtype)
        acc = acc + _dot(a, wout_ref[lo:hi, :])
    o_ref[...] = x + 0.5 * acc


def _ffn(x2, norm, w_in, w_out, layer):
    t = x2.shape[0]
    return pl.pallas_call(
        _ffn_body,
        out_shape=jax.ShapeDtypeStruct(x2.shape, F32),
        grid=(t // FFN_TILE,),
        in_specs=[pl.BlockSpec((FFN_TILE, D_MODEL), lambda i: (i, 0)),
                  _layer_spec(norm.shape, layer),
                  _layer_spec(w_in.shape, layer),
                  _layer_spec(w_out.shape, layer)],
        out_specs=pl.BlockSpec((FFN_TILE, D_MODEL), lambda i: (i, 0)),
        compiler_params=_params(1),
        name="ffn",
    )(x2, norm, w_in, w_out)


def _sgu_part(hn, w_ref, vn_ref, ws_ref, bs_ref, o_ref):
    uv = _gelu(_dot(hn, w_ref[...]))
    u = uv[:, :WIDTH]
    v = _rms(uv[:, WIDTH:], vn_ref[...]).astype(BF)
    row = lax.broadcasted_iota(jnp.int32, (CHUNK, CHUNK), 0)
    col = lax.broadcasted_iota(jnp.int32, (CHUNK, CHUNK), 1)
    gw = WIDTH // SGU_GROUPS
    ws = [jnp.where(row >= col, ws_ref[g], 0.0).astype(BF) for g in range(SGU_GROUPS)]
    bias = bs_ref[...]
    for c in range(TOKEN_TILE // CHUNK):
        rows = slice(c * CHUNK, (c + 1) * CHUNK)
        mixed = jnp.concatenate(
            [_dot(ws[g], v[rows, g * gw:(g + 1) * gw]) for g in range(SGU_GROUPS)], axis=1)
        o_ref[rows, :] = (u[rows, :] * (mixed + bias)).astype(BF)


def _conv_part(hn, w_ref, dw_ref, db_ref, cn_ref, o_ref, buf_ref, shift_ref):
    ab = _dot(hn, w_ref[...])
    buf_ref[CONV_HIST:CONV_HIST + TOKEN_TILE, :] = ab[:, :WIDTH] * jax.nn.sigmoid(ab[:, WIDTH:])
    acc = jnp.zeros((TOKEN_TILE, WIDTH), F32) + db_ref[...]
    first = CONV_HIST - (CONV_KERNEL - 1)
    for res in range(min(SUBLANES, CONV_KERNEL)):
        n_taps = (CONV_KERNEL - 1 - res) // SUBLANES + 1
        rows = TOKEN_TILE + (n_taps - 1) * SUBLANES
        shift_ref[0:rows, :] = buf_ref[first + res:first + res + rows, :]
        for a in range(n_taps):
            k = a * SUBLANES + res
            acc = acc + dw_ref[k:k + 1, :] * shift_ref[a * SUBLANES:a * SUBLANES + TOKEN_TILE, :]
    o_ref[...] = _silu(_rms(acc, cn_ref[...])).astype(BF)
    buf_ref[0:CONV_HIST, :] = buf_ref[TOKEN_TILE:TOKEN_TILE + CONV_HIST, :]


def _ssd_part(hn, w_ref, cw_ref, cb_ref, dtb_ref, a_ref, dsk_ref, ng_ref, o_ref, buf_ref, st_ref):
    hw = SSM_REP * SSM_HEAD_DIM
    proj = _dot(hn, w_ref[...])
    z = proj[:, :WIDTH]
    buf_ref[SSM_HIST:SSM_HIST + TOKEN_TILE, :] = proj[:, WIDTH:WIDTH + SSM_CONV_DIM]
    dt = jax.nn.softplus(proj[:, WIDTH + SSM_CONV_DIM:] + dtb_ref[...])
    xbc = jnp.zeros((TOKEN_TILE, SSM_CONV_DIM), F32) + cb_ref[...]
    first = SSM_HIST - (SSM_CONV - 1)
    for k in range(SSM_CONV):
        xbc = xbc + cw_ref[k:k + 1, :] * buf_ref[first + k:first + k + TOKEN_TILE, :]
    buf_ref[0:SSM_HIST, :] = buf_ref[TOKEN_TILE:TOKEN_TILE + SSM_HIST, :]
    xbc = _silu(xbc)
    xs = xbc[:, :WIDTH]
    a_all = dt * a_ref[...]
    xdt = xs * dt

    row = lax.broadcasted_iota(jnp.int32, (CHUNK, CHUNK), 0)
    col = lax.broadcasted_iota(jnp.int32, (CHUNK, CHUNK), 1)
    causal = row >= col
    tril = jnp.where(causal, 1.0, 0.0).astype(BF)
    lane = lax.broadcasted_iota(jnp.int32, (CHUNK, 2 * SSM_HEAD_DIM), 1)
    low_half = lane < SSM_HEAD_DIM

    for c in range(TOKEN_TILE // CHUNK):
        rows = slice(c * CHUNK, (c + 1) * CHUNK)
        hi, mid, lo = _split3(a_all[rows, :])
        cs = _dot(tril, hi) + _dot(tril, mid) + _dot(tril, lo)
        cs_t = cs.T
        cs_last = cs[CHUNK - 1:CHUNK, :]
        xc = xdt[rows, :]
        x_decay = (xc * jnp.exp(cs_last - cs)).astype(BF)
        xc_bf = xc.astype(BF)
        grow = jnp.exp(cs)
        y_parts = []
        for g in range(SSM_GROUPS):
            bg = xbc[rows, WIDTH + g * SSM_STATE:WIDTH + (g + 1) * SSM_STATE].astype(BF)
            cg = xbc[rows, WIDTH + (SSM_GROUPS + g) * SSM_STATE:
                     WIDTH + (SSM_GROUPS + g + 1) * SSM_STATE].astype(BF)
            cb = _dot_nt(cg, bg)
            gl = slice(g * hw, (g + 1) * hw)
            y_off = _dot(cg, st_ref[g].astype(BF)) * grow[:, gl]
            y_diag = []
            for pair in range(SSM_REP // 2):
                scores = []
                for hh in range(2):
                    hc = g * hw + (2 * pair + hh) * SSM_HEAD_DIM
                    seg = cs[:, hc:hc + 1] - cs_t[hc:hc + 1, :]
                    decay = jnp.exp(jnp.where(causal, seg, NEG))
                    scores.append((cb * decay).astype(BF))
                xp = xc_bf[:, g * hw + pair * 2 * SSM_HEAD_DIM:g * hw + (pair + 1) * 2 * SSM_HEAD_DIM]
                zero = jnp.zeros_like(xp)
                rhs = jnp.concatenate([jnp.where(low_half, xp, zero),
                                       jnp.where(low_half, zero, xp)], axis=0)
                y_diag.append(_dot(jnp.concatenate(scores, axis=1), rhs))
            y_parts.append(jnp.concatenate(y_diag, axis=1) + y_off)
            st_ref[g] = jnp.exp(cs_last[:, gl]) * st_ref[g] + _dot_tn(bg, x_decay[:, gl])
        y = jnp.concatenate(y_parts, axis=1) + xs[rows, :] * dsk_ref[...]
        y = y * _silu(z[rows, :])
        o_ref[rows, :] = _rms(y, ng_ref[...]).astype(BF)


def _head_rms(x, ones_blockdiag, gain, scale):
    hi, lo = _split2(x * x)
    ss = _dot(hi, ones_blockdiag) + _dot(lo, ones_blockdiag)
    return x * lax.rsqrt(ss * (1.0 / HEAD_DIM) + EPS) * (gain * scale)


def _nsa_proj_part(hn, w_ref, qn_ref, kn_ref, bdq_ref, bdk_ref, qc_ref, ksc_ref, kwc_ref, vcn_ref,
                   qt_ref, gate_ref, kc_ref, vc_ref, ks_ref, vst_ref, kw_ref, vwt_ref):
    proj = _dot(hn, w_ref[...])
    o = WIDTH
    kc_ref[...] = proj[:, o:o + KV_WIDTH]
    vc_ref[...] = proj[:, o + KV_WIDTH:o + 2 * KV_WIDTH]
    q = _head_rms(proj[:, :WIDTH], bdq_ref[...], qn_ref[...], HEAD_DIM ** -0.5)
    ks = _head_rms(proj[:, o + 2 * KV_WIDTH:o + 3 * KV_WIDTH], bdk_ref[...], kn_ref[...], 1.0)
    kw = _head_rms(proj[:, o + 4 * KV_WIDTH:o + 5 * KV_WIDTH], bdk_ref[...], kn_ref[...], 1.0)
    vs = proj[:, o + 3 * KV_WIDTH:o + 4 * KV_WIDTH]
    vw = proj[:, o + 5 * KV_WIDTH:o + 6 * KV_WIDTH]
    gate_t = jax.nn.sigmoid(proj[:, o + 6 * KV_WIDTH:]).T
    lane = lax.broadcasted_iota(jnp.int32, (TOKEN_TILE, KV_WIDTH), 1)
    gw = NSA_REP * HEAD_DIM
    for g in range(NSA_GROUPS):
        own = (lane >= g * HEAD_DIM) & (lane < (g + 1) * HEAD_DIM)
        ks_ref[g] = (jnp.where(own, ks, 0.0) + ksc_ref[g]).astype(BF)
        kw_ref[g] = (jnp.where(own, kw, 0.0) + kwc_ref[g]).astype(BF)
    for c in range(TOKEN_TILE // CHUNK):
        rows = slice(c * CHUNK, (c + 1) * CHUNK)
        vs_t = vs[rows, :].T.astype(BF)
        vw_t = vw[rows, :].T.astype(BF)
        for g in range(NSA_GROUPS):
            feat = slice(g * HEAD_DIM, (g + 1) * HEAD_DIM)
            rest = slice((1 - g) * HEAD_DIM, (2 - g) * HEAD_DIM)
            q_t = q[rows, g * gw:(g + 1) * gw].T
            qt_ref[g, c, feat, :] = jnp.concatenate(
                [q_t[r * HEAD_DIM:(r + 1) * HEAD_DIM] for r in range(NSA_REP)], axis=1).astype(BF)
            qt_ref[g, c, rest, :] = qc_ref[g]
            gate_ref[g, c] = gate_t[g * GATE_ROWS:(g + 1) * GATE_ROWS, rows]
            vst_ref[g, c, 0:HEAD_DIM, :] = vs_t[feat]
            vst_ref[g, c, HEAD_DIM:V_ROWS, :] = vcn_ref[...]
            vwt_ref[g, c, 0:HEAD_DIM, :] = vw_t[feat]
            vwt_ref[g, c, HEAD_DIM:V_ROWS, :] = vcn_ref[...]


N_SGU_IN, N_SSD_IN, N_CONV_IN, N_NSA_IN = 4, 7, 4, 9


def _mixers_body(x_ref, g_ref, *refs):
    n_in = N_SGU_IN + N_SSD_IN + N_CONV_IN + N_NSA_IN
    ins, outs, scratch = refs[:n_in], refs[n_in:n_in + 11], refs[n_in + 11:]
    sgu_in, ins = ins[:N_SGU_IN], ins[N_SGU_IN:]
    ssd_in, ins = ins[:N_SSD_IN], ins[N_SSD_IN:]
    conv_in, nsa_in = ins[:N_CONV_IN], ins[N_CONV_IN:]
    ya_ref, yb_ref, yd_ref = outs[:3]
    conv_buf, conv_shift, ssd_buf, ssd_state = scratch

    @pl.when(pl.program_id(1) == 0)
    def _():
        conv_buf[0:CONV_HIST, :] = jnp.zeros((CONV_HIST, WIDTH), F32)
        ssd_buf[0:SSM_HIST, :] = jnp.zeros((SSM_HIST, SSM_CONV_DIM), F32)
        ssd_state[...] = jnp.zeros(ssd_state.shape, F32)

    hn = _rms(x_ref[...], g_ref[...]).astype(BF)
    _sgu_part(hn, *sgu_in, ya_ref)
    _ssd_part(hn, *ssd_in, yb_ref, ssd_buf, ssd_state)
    _conv_part(hn, *conv_in, yd_ref, conv_buf, conv_shift)
    _nsa_proj_part(hn, *nsa_in, *outs[3:])


def _mixers(x3, norm, sgu_p, ssd_p, conv_p, w_c, p, layer):
    b, s, _ = x3.shape
    g, nq, blocks = NSA_GROUPS, NSA_REP * CHUNK, TOKEN_TILE // CHUNK
    n_blk = s // CHUNK
    tile = lambda n: pl.BlockSpec((None, TOKEN_TILE, n), lambda i, j: (i, j, 0))
    const = lambda a: pl.BlockSpec(a.shape, lambda i, j: (0,) * a.ndim)
    lay = lambda a: _layer_spec(a.shape, layer)
    key_const = lambda a: pl.BlockSpec((g, TOKEN_TILE, LANES), lambda i, j: (0, j, 0))
    per_blk = lambda rows, lanes: pl.BlockSpec((None, g, blocks, rows, lanes), lambda i, j: (i, 0, j, 0, 0))
    keys = pl.BlockSpec((None, g, TOKEN_TILE, LANES), lambda i, j: (i, 0, j, 0))
    y = jax.ShapeDtypeStruct((b, s, WIDTH), BF)
    raw = jax.ShapeDtypeStruct((b, s, KV_WIDTH), F32)
    k_aug = jax.ShapeDtypeStruct((b, g, s, LANES), BF)
    v_t = jax.ShapeDtypeStruct((b, g, n_blk, V_ROWS, CHUNK), BF)
    assert (len(sgu_p), len(ssd_p), len(conv_p)) == (N_SGU_IN, N_SSD_IN, N_CONV_IN)
    nsa_layer = (w_c, p["q_gain"], p["k_gain2"])
    nsa_const = (p["bd_q"], p["bd_k"], p["q_const"])
    layered = tuple(sgu_p) + tuple(ssd_p) + tuple(conv_p) + nsa_layer
    return pl.pallas_call(
        _mixers_body,
        out_shape=(y, y, y,
                   jax.ShapeDtypeStruct((b, g, n_blk, LANES, nq), BF),
                   jax.ShapeDtypeStruct((b, g, n_blk, GATE_ROWS, CHUNK), F32),
                   raw, raw, k_aug, v_t, k_aug, v_t),
        grid=(b, s // TOKEN_TILE),
        in_specs=[tile(D_MODEL), lay(norm)] + [lay(a) for a in layered] + [const(a) for a in nsa_const]
        + [key_const(p["k_sel"]), key_const(p["k_win"]), const(p["v_const"])],
        out_specs=(tile(WIDTH),) * 3
        + (per_blk(LANES, nq), per_blk(GATE_ROWS, CHUNK), tile(KV_WIDTH), tile(KV_WIDTH),
           keys, per_blk(V_ROWS, CHUNK), keys, per_blk(V_ROWS, CHUNK)),
        scratch_shapes=[pltpu.VMEM((TOKEN_TILE + CONV_HIST, WIDTH), F32),
                        pltpu.VMEM((TOKEN_TILE + CONV_HIST, WIDTH), F32),
                        pltpu.VMEM((TOKEN_TILE + SSM_HIST, SSM_CONV_DIM), F32),
                        pltpu.VMEM((SSM_GROUPS, SSM_STATE, SSM_REP * SSM_HEAD_DIM), F32)],
        compiler_params=_params(2),
        name="mixers",
    )(x3, norm, *layered, *nsa_const, p["k_sel"], p["k_win"], p["v_const"])


def _nsa_attn_body(qt_ref, gate_ref, kc_ref, vc_ref, ks_ref, vst_ref, kw_ref, vwt_ref,
                   w1k_ref, w1kf_ref, w2k_ref, pek_ref, w1v_ref, w1vf_ref, w2v_ref, pev_ref, kn_ref,
                   cpos_ref, ovt_ref, wmask_ref, dmask_ref,
                   o_ref, qs_ref, kcmp_ref, vcmpt_ref, m_ref, acc_ref, part_ref, s_ref):
    step = pl.program_id(1)
    nq = NSA_REP * CHUNK
    n_sel = ovt_ref.shape[0]
    groups = range(NSA_GROUPS)
    q_blocks = range(ATTN_BLOCKS)

    def compress(x_ref, w1_ref, pe_ref, w1f_ref, w2_ref):
        pe_bias = _dot(pe_ref[...], w1f_ref[...])[0:1, :]
        head = [jnp.zeros((CHUNK, CMP_HIDDEN), F32) for _ in groups]
        tail = [jnp.zeros((CHUNK, CMP_HIDDEN), F32) for _ in groups]
        for i in range(CMP_STRIDE):
            x = x_ref[pl.ds(i, CHUNK, stride=CMP_STRIDE), :].astype(BF)
            for g in groups:
                head[g] = head[g] + _dot(x, w1_ref[g, i])
                tail[g] = tail[g] + _dot(x, w1_ref[g, CMP_STRIDE + i])
        return [_dot(_gelu(head[g] + pltpu.roll(tail[g], CHUNK - 1, 0) + pe_bias).astype(BF), w2_ref[...])
                for g in groups]

    @pl.when(step == 0)
    def _():
        k_cmp = compress(kc_ref, w1k_ref, pek_ref, w1kf_ref, w2k_ref)
        v_cmp = compress(vc_ref, w1v_ref, pev_ref, w1vf_ref, w2v_ref)
        lane = lax.broadcasted_iota(jnp.int32, (CHUNK, LANES - HEAD_DIM), 1)
        ones_col = jnp.where(lane == 0, 1.0, 0.0)
        for g in groups:
            halves = [_rms(k_cmp[g], kn_ref[...]).astype(BF), cpos_ref[...]]
            kcmp_ref[g] = jnp.concatenate(halves if g == 0 else halves[::-1], axis=1)
            vcmpt_ref[g] = jnp.concatenate([v_cmp[g], ones_col], axis=1).T[0:V_ROWS, :].astype(BF)

    def q_index(rows):
        return lax.broadcasted_iota(jnp.int32, (rows, nq), 1) % CHUNK

    def k_index(rows):
        return lax.broadcasted_iota(jnp.int32, (rows, nq), 0)

    def select_blocks(p_c, t0):
        p_sum = p_c[:, 0:CHUNK]
        for r in range(1, NSA_REP):
            p_sum = p_sum + p_c[:, r * CHUNK:(r + 1) * CHUNK]
        p_hi, p_lo = _split2(p_sum)
        imp = _dot(ovt_ref[...], p_hi) + _dot(ovt_ref[...], p_lo)
        blk = lax.broadcasted_iota(jnp.int32, (n_sel, CHUNK), 0)
        tq = t0 + lax.broadcasted_iota(jnp.int32, (n_sel, CHUNK), 1)
        cur = tq // SEL_BLOCK
        forced = (blk == 0) | (blk == cur) | (blk == cur - 1)
        score = jnp.where(forced, FORCE, jnp.where(blk * SEL_BLOCK <= tq, imp, NEG))
        blk_f = blk.astype(F32)
        chosen = jnp.zeros((n_sel, CHUNK), F32)
        for _ in range(SEL_TOP_N):
            best = jnp.max(score, axis=0, keepdims=True)
            first = jnp.min(jnp.where(score == best, blk_f, 1e9), axis=0, keepdims=True)
            pick = blk_f == first
            chosen = jnp.where(pick, 1.0, chosen)
            score = jnp.where(pick, -3e38, score)
        return ((chosen - 1.0) * (-NEG)).astype(BF)

    n_w = WINDOW + CHUNK
    sub_blocks = SEL_KEYS // CHUNK

    def gate_row(g, qb, branch):
        rows = gate_ref[g, qb]
        return jnp.concatenate([rows[branch * NSA_REP + r:branch * NSA_REP + r + 1] for r in range(NSA_REP)], axis=1)

    for g in groups:
        for qb in q_blocks:
            j = step * ATTN_BLOCKS + qb
            t0 = j * CHUNK
            lanes = slice(qb * nq, (qb + 1) * nq)
            qt = qt_ref[g, qb]
            slot = (1 - g) * HEAD_DIM
            ci = k_index(CHUNK)
            valid_c = (ci * CMP_STRIDE + (CMP_BLOCK - 1) <= t0 + q_index(CHUNK)) & (ci < CHUNK - 1)
            s_c = jnp.where(valid_c, _dot(kcmp_ref[g], qt), NEG)
            e_c = jnp.where(valid_c, jnp.exp(s_c - jnp.max(s_c, axis=0, keepdims=True)), 0.0)
            p_c = e_c / jnp.maximum(jnp.sum(e_c, axis=0, keepdims=True), 1e-30)
            o_c = _dot(vcmpt_ref[g], p_c.astype(BF))[0:HEAD_DIM]
            qs_ref[g, :, lanes] = qt
            qs_ref[g, slot:slot + n_sel, lanes] = jnp.concatenate([select_blocks(p_c, t0)] * NSA_REP, axis=1)
            w_blk = jnp.maximum(j - WINDOW // CHUNK, 0)
            w_start = pl.multiple_of(w_blk * CHUNK, CHUNK)
            s_w = _dot(kw_ref[g, pl.ds(w_start, n_w), :], qt) + wmask_ref[j - w_blk]
            p_w = jnp.exp(s_w - jnp.max(s_w, axis=0, keepdims=True)).astype(BF)
            o_w = _dot(vwt_ref[g, w_blk], p_w[0:CHUNK])
            for sub in range(1, n_w // CHUNK):
                o_w = o_w + _dot(vwt_ref[g, w_blk + sub], p_w[sub * CHUNK:(sub + 1) * CHUNK])
            part_ref[g, :, lanes] = (gate_row(g, qb, 0) * o_c
                                     + gate_row(g, qb, 2) * (o_w[0:HEAD_DIM] / o_w[HEAD_DIM:HEAD_DIM + 1]))

    m_ref[...] = jnp.full(m_ref.shape, NEG, F32)
    acc_ref[...] = jnp.zeros(acc_ref.shape, F32)

    def scores(g, blk0):
        start = pl.multiple_of(blk0 * CHUNK, SEL_KEYS)
        return _dot(ks_ref[g, pl.ds(start, SEL_KEYS), :], qs_ref[g])

    def absorb(g, s, blk0):
        m_old = m_ref[g]
        m_new = jnp.maximum(m_old, jnp.max(s, axis=0, keepdims=True))
        p = jnp.exp(s - m_new).astype(BF)
        pv = _dot(vst_ref[g, blk0], p[0:CHUNK])
        for sub in range(1, sub_blocks):
            pv = pv + _dot(vst_ref[g, blk0 + sub], p[sub * CHUNK:(sub + 1) * CHUNK])
        acc_ref[g] = jnp.exp(m_old - m_new) * acc_ref[g] + pv
        m_ref[g] = m_new

    for g in groups:
        s_ref[g] = scores(g, 0)

    def loop_body(i, carry):
        for g in groups:
            s = s_ref[g]
            s_ref[g] = scores(g, (i + 1) * sub_blocks)
            absorb(g, s, i * sub_blocks)
        return carry

    lax.fori_loop(0, step, loop_body, 0)
    for g in groups:
        absorb(g, s_ref[g] + dmask_ref[...], step * sub_blocks)
        acc = acc_ref[g]
        o_s = acc[0:HEAD_DIM] / acc[HEAD_DIM:HEAD_DIM + 1]
        gw = NSA_REP * HEAD_DIM
        for qb in q_blocks:
            lanes = slice(qb * nq, (qb + 1) * nq)
            out_t = part_ref[g, :, lanes] + gate_row(g, qb, 1) * o_s[:, lanes]
            by_head = jnp.concatenate([out_t[:, r * CHUNK:(r + 1) * CHUNK] for r in range(NSA_REP)], axis=0)
            o_ref[qb * CHUNK:(qb + 1) * CHUNK, g * gw:(g + 1) * gw] = by_head.T.astype(BF)


def _nsa_attn(qt_aug, gates_t, kc, vc, ks_aug, vs_t, kw_aug, vw_t,
              w1k, w1kf, w2k, pek, w1v, w1vf, w2v, pev, k_gain, cpos, ov_t, win_mask, diag_mask, layer):
    b, g, nblk, _, nq = qt_aug.shape
    lanes = ATTN_BLOCKS * nq
    blk = lambda a: pl.BlockSpec((None, g, ATTN_BLOCKS) + a.shape[3:], lambda i, j: (i, 0, j, 0, 0))
    seq = lambda a: pl.BlockSpec((None,) + a.shape[1:], lambda i, j: (i,) + (0,) * (a.ndim - 1))
    lay = lambda a: pl.BlockSpec((None,) + a.shape[1:], lambda i, j: (layer,) + (0,) * (a.ndim - 1))
    const = lambda a: pl.BlockSpec(a.shape, lambda i, j: (0,) * a.ndim)
    return pl.pallas_call(
        _nsa_attn_body,
        out_shape=jax.ShapeDtypeStruct((b, nblk * CHUNK, g * NSA_REP * HEAD_DIM), BF),
        grid=(b, nblk // ATTN_BLOCKS),
        in_specs=[blk(qt_aug), blk(gates_t), seq(kc), seq(vc),
                  seq(ks_aug), seq(vs_t), seq(kw_aug), seq(vw_t),
                  lay(w1k), lay(w1kf), lay(w2k), lay(pek),
                  lay(w1v), lay(w1vf), lay(w2v), lay(pev), lay(k_gain),
                  const(cpos), const(ov_t), const(win_mask), const(diag_mask)],
        out_specs=pl.BlockSpec((None, ATTN_BLOCKS * CHUNK, g * NSA_REP * HEAD_DIM), lambda i, j: (i, j, 0)),
        scratch_shapes=[pltpu.VMEM((g, LANES, lanes), BF),
                        pltpu.VMEM((g, CHUNK, LANES), BF),
                        pltpu.VMEM((g, V_ROWS, CHUNK), BF),
                        pltpu.VMEM((g, 1, lanes), F32),
                        pltpu.VMEM((g, V_ROWS, lanes), F32),
                        pltpu.VMEM((g, HEAD_DIM, lanes), F32),
                        pltpu.VMEM((g, SEL_KEYS, lanes), F32)],
        compiler_params=_params(2),
        name="nsa_attn",
    )(qt_aug, gates_t, kc, vc, ks_aug, vs_t, kw_aug, vw_t,
      w1k, w1kf, w2k, pek, w1v, w1vf, w2v, pev, k_gain, cpos, ov_t, win_mask, diag_mask)


def _merge_body(x_ref, ya_ref, yb_ref, yc_ref, yd_ref, g_ref, wg_ref, wb_ref, wo_ref, o_ref):
    x = x_ref[...]
    hn = _rms(x, g_ref[...]).astype(BF)
    merged = jnp.zeros_like(x)
    for i, y_ref in enumerate((ya_ref, yb_ref, yc_ref, yd_ref)):
        gate = jax.nn.sigmoid(_dot(hn, wg_ref[:, i * D_MODEL:(i + 1) * D_MODEL]))
        merged = merged + gate * _dot(y_ref[...], wb_ref[i])
    o_ref[...] = x + _dot(merged.astype(BF), wo_ref[...])


def _merge(x2, ya, yb, yc, yd, norm, w_gate, w_branch, w_out, layer):
    t = x2.shape[0]
    tile = lambda n: pl.BlockSpec((FFN_TILE, n), lambda i: (i, 0))
    return pl.pallas_call(
        _merge_body,
        out_shape=jax.ShapeDtypeStruct(x2.shape, F32),
        grid=(t // FFN_TILE,),
        in_specs=[tile(D_MODEL)] + [tile(WIDTH)] * 4
        + [_layer_spec(a.shape, layer) for a in (norm, w_gate, w_branch, w_out)],
        out_specs=tile(D_MODEL),
        compiler_params=_params(1),
        name="merge",
    )(x2, ya, yb, yc, yd, norm, w_gate, w_branch, w_out)


def _alibi_slopes():
    return np.array([2.0 ** (-8.0 * (i + 1) / NSA_HEADS) for i in range(NSA_HEADS)], np.float32)


def _nsa_constants(seq):
    n_sel = seq // SEL_BLOCK
    n_cmp = (seq - CMP_BLOCK) // CMP_STRIDE + 1
    assert n_cmp == CHUNK - 1 and n_sel <= SEL_BLOCK // 2
    pos_cols = LANES - HEAD_DIM - n_sel
    slopes = _alibi_slopes().reshape(NSA_GROUPS, NSA_REP)
    q_const = np.zeros((NSA_GROUPS, LANES - HEAD_DIM, NSA_REP * CHUNK), np.float32)
    lane_slope = np.repeat(slopes, CHUNK, axis=1)
    q_const[:, n_sel, :] = lane_slope * SEL_BLOCK
    q_const[:, n_sel + 1, :] = lane_slope
    pos = np.arange(seq)
    pos_part = np.zeros((seq, pos_cols), np.float32)
    pos_part[:, 0] = pos // SEL_BLOCK
    pos_part[:, 1] = pos % SEL_BLOCK
    onehot = (pos[:, None] // SEL_BLOCK == np.arange(n_sel)[None, :]).astype(np.float32)

    def key_const(first):
        half = np.concatenate([first, pos_part], axis=1)
        return np.stack([np.concatenate([np.zeros_like(half), half][::1 if g == 0 else -1], axis=1)
                         for g in range(NSA_GROUPS)])

    k_sel_const = key_const(onehot)
    k_win_const = key_const(np.zeros_like(onehot))
    v_const = np.zeros((V_ROWS - HEAD_DIM, CHUNK), np.float32)
    v_const[0, :] = 1.0
    c_mid = np.arange(CHUNK) * CMP_STRIDE + (CMP_BLOCK - 1) / 2.0
    cmp_const = np.zeros((CHUNK, LANES - HEAD_DIM), np.float32)
    cmp_const[:, n_sel] = c_mid // SEL_BLOCK
    cmp_const[:, n_sel + 1] = c_mid % SEL_BLOCK
    c_start = np.arange(CHUNK) * CMP_STRIDE
    s_start = np.arange(n_sel) * SEL_BLOCK
    overlap_t = ((c_start[None, :] <= s_start[:, None] + SEL_BLOCK - 1)
                 & (c_start[None, :] + CMP_BLOCK - 1 >= s_start[:, None])
                 & (np.arange(CHUNK)[None, :] < n_cmp)).astype(np.float32)
    qi = np.arange(NSA_REP * CHUNK)[None, :] % CHUNK

    def mask(rows, n_cases, valid):
        ki = np.arange(rows)[:, None]
        return np.stack([np.where(valid(d * CHUNK + qi - ki), 0.0, NEG) for d in range(n_cases)]).astype(np.float32)

    win_mask = mask(WINDOW + CHUNK, WINDOW // CHUNK + 1, lambda rel: (rel >= 0) & (rel < WINDOW))
    diag_mask = np.concatenate(list(mask(SEL_KEYS, ATTN_BLOCKS, lambda rel: rel >= 0)), axis=1)
    bf = lambda a: jnp.asarray(a, BF)
    return dict(q_const=bf(q_const), k_sel=bf(k_sel_const), k_win=bf(k_win_const), v_const=bf(v_const),
                cmp_const=bf(cmp_const), overlap_t=bf(overlap_t),
                win_mask=jnp.asarray(win_mask), diag_mask=jnp.asarray(diag_mask))


def _block_diag_ones(n):
    idx = np.arange(n) // HEAD_DIM
    return jnp.asarray(idx[:, None] == idx[None, :], BF)


def _compress_weights(w1):
    nl = w1.shape[0]
    per_pos = w1.reshape(nl, 1, CMP_BLOCK, 1, HEAD_DIM, CMP_HIDDEN)
    group_sel = jnp.eye(NSA_GROUPS, dtype=w1.dtype)
    big = per_pos * group_sel[None, :, None, :, None, None]
    return big.reshape(nl, NSA_GROUPS, CMP_BLOCK, KV_WIDTH, CMP_HIDDEN).astype(BF)


def _nsa_proj_weights(w_in):
    depth = w_in.shape[0]
    gate = w_in[:, :, OFF_GATE:OFF_AB].reshape(depth, D_MODEL, NSA_GROUPS, NSA_REP, 3)
    gate = jnp.swapaxes(gate, 3, 4).reshape(depth, D_MODEL, NSA_GROUPS, 3 * NSA_REP)
    gate = jnp.pad(gate, ((0, 0), (0, 0), (0, 0), (0, GATE_ROWS - 3 * NSA_REP)))
    gate = gate.reshape(depth, D_MODEL, NSA_GROUPS * GATE_ROWS)
    gate = jnp.pad(gate, ((0, 0), (0, 0), (0, LANES - NSA_GROUPS * GATE_ROWS)))
    return jnp.concatenate([w_in[:, :, OFF_Q:OFF_GATE], gate], axis=-1).astype(BF)


def _pe_rows(pe):
    nl = pe.shape[0]
    flat = pe.reshape(nl, 1, CMP_BLOCK * HEAD_DIM)
    return jnp.concatenate([flat, jnp.zeros((nl, 7, CMP_BLOCK * HEAD_DIM), pe.dtype)], axis=1).astype(BF)


def _nsa_params(seq, q_norm, k_norm, pe_k, w1_k, w2_k, pe_v, w1_v, w2_v):
    row = lambda a: a[:, None, :]
    p = _nsa_constants(seq)
    p.update(q_gain=row(jnp.tile(q_norm, (1, NSA_HEADS))), k_gain2=row(jnp.tile(k_norm, (1, NSA_GROUPS))),
             k_gain=row(k_norm), w1k=_compress_weights(w1_k), w1v=_compress_weights(w1_v),
             w1kf=w1_k.astype(BF), w1vf=w1_v.astype(BF), w2k=w2_k.astype(BF), w2v=w2_v.astype(BF),
             pek=_pe_rows(pe_k), pev=_pe_rows(pe_v),
             bd_q=_block_diag_ones(WIDTH), bd_k=_block_diag_ones(KV_WIDTH))
    return p


def _token_mixers(x3, mxn, sgu_p, ssd_p, conv_p, w_c, p, layer):
    y_a, y_b, y_d, *attn_in = _mixers(x3, mxn, sgu_p, ssd_p, conv_p, w_c, p, layer)
    y_c = _nsa_attn(*attn_in,
                    p["w1k"], p["w1kf"], p["w2k"], p["pek"], p["w1v"], p["w1vf"], p["w2v"], p["pev"], p["k_gain"],
                    p["cmp_const"], p["overlap_t"], p["win_mask"], p["diag_mask"], layer)
    return y_a, y_b, y_c, y_d


def kernel(x, ffn1_norm, ffn1_w_in, ffn1_w_out, mix_norm, w_in, sgu_v_norm, sgu_w, sgu_b, ssm_conv_w, ssm_conv_b, ssm_dt_bias, ssm_a_log, ssm_d, ssm_norm, nsa_q_norm, nsa_k_norm, nsa_pe_k, nsa_w1_k, nsa_w2_k, nsa_pe_v, nsa_w1_v, nsa_w2_v, conv_dw_w, conv_dw_b, conv_norm, w_branch, w_out, ffn2_norm, ffn2_w_in, ffn2_w_out):
    bsz, seq, _ = x.shape
    depth = w_in.shape[0]
    n_blk = seq // CHUNK
    row = lambda a: a[:, None, :]
    rep = lambda a, n: jnp.repeat(a, n, axis=-1)

    f1n, f2n, mxn = row(ffn1_norm), row(ffn2_norm), row(mix_norm)
    f1_in, f1_out = ffn1_w_in, ffn1_w_out
    f2_in, f2_out = ffn2_w_in, ffn2_w_out
    w_a = w_in[:, :, OFF_UV:OFF_Z].astype(BF)
    w_b = jnp.concatenate([w_in[:, :, OFF_Z:OFF_DT], rep(w_in[:, :, OFF_DT:OFF_Q], SSM_HEAD_DIM)], axis=-1).astype(BF)
    w_c = _nsa_proj_weights(w_in)
    w_d = w_in[:, :, OFF_AB:OFF_MERGE].astype(BF)
    w_g = w_in[:, :, OFF_MERGE:].astype(BF)
    sgu_bias = rep(jnp.swapaxes(sgu_b, 1, 2), WIDTH // SGU_GROUPS)
    dt_bias = row(rep(ssm_dt_bias, SSM_HEAD_DIM))
    a_neg = row(rep(-jnp.exp(ssm_a_log), SSM_HEAD_DIM))
    d_skip = row(rep(ssm_d, SSM_HEAD_DIM))
    wb, wo = w_branch.astype(BF), w_out.astype(BF)
    nsa_p = _nsa_params(seq, nsa_q_norm, nsa_k_norm, nsa_pe_k, nsa_w1_k, nsa_w2_k, nsa_pe_v, nsa_w1_v, nsa_w2_v)
    sgu_p = (w_a, row(sgu_v_norm), sgu_w, sgu_bias)
    ssd_p = (w_b, ssm_conv_w, row(ssm_conv_b), dt_bias, a_neg, d_skip, row(ssm_norm))
    conv_p = (w_d, conv_dw_w, row(conv_dw_b), row(conv_norm))

    x2 = x.reshape(bsz * seq, D_MODEL)
    for l in range(depth):
        x2 = _ffn(x2, f1n, f1_in, f1_out, l)
        x3 = x2.reshape(bsz, seq, D_MODEL)
        y_a, y_b, y_c, y_d = _token_mixers(x3, mxn, sgu_p, ssd_p, conv_p, w_c, nsa_p, l)
        flat = lambda y: y.reshape(bsz * seq, WIDTH)
        x2 = _merge(x2, flat(y_a), flat(y_b), flat(y_c), flat(y_d), mxn, w_g, wb, wo, l)
        x2 = _ffn(x2, f2n, f2_in, f2_out, l)
    return x2.reshape(bsz, seq, D_MODEL)
```

```python
import functools
import math

import numpy as np
import jax
import jax.numpy as jnp
from jax import lax
from jax.experimental import pallas as pl
from jax.experimental.pallas import tpu as pltpu

F32 = jnp.float32
BF = jnp.bfloat16

D_MODEL = 1024
FFN_DIM = 2816
WIDTH = 512
N_BRANCH = 4
SGU_GROUPS = 4
CHUNK = 128
SSM_HEADS = 8
SSM_HEAD_DIM = 64
SSM_GROUPS = 2
SSM_STATE = 128
SSM_CONV = 4
SSM_CONV_DIM = WIDTH + 2 * SSM_GROUPS * SSM_STATE
SSM_REP = SSM_HEADS // SSM_GROUPS
NSA_HEADS = 8
NSA_GROUPS = 2
NSA_REP = NSA_HEADS // NSA_GROUPS
HEAD_DIM = 64
KV_WIDTH = NSA_GROUPS * HEAD_DIM
CMP_BLOCK = 32
CMP_STRIDE = 16
CMP_HIDDEN = 128
SEL_BLOCK = 64
SEL_TOP_N = 8
WINDOW = 256
CONV_KERNEL = 31
EPS = 1e-6
NEG = -1e30
FORCE = 1e9

OFF_UV = 0
OFF_Z = OFF_UV + 2 * WIDTH
OFF_XBC = OFF_Z + WIDTH
OFF_DT = OFF_XBC + SSM_CONV_DIM
OFF_Q = OFF_DT + SSM_HEADS
OFF_KV = OFF_Q + WIDTH
OFF_GATE = OFF_KV + 6 * KV_WIDTH
OFF_AB = OFF_GATE + 3 * NSA_HEADS
OFF_MERGE = OFF_AB + 2 * WIDTH
IN_PROJ_DIM = OFF_MERGE + N_BRANCH * D_MODEL

LANES = 128
SUBLANES = 8
MXU_DIM = 256
VMEM_LIMIT_BYTES = 56 * 1024 * 1024

TOKEN_TILE = 256
FFN_TILE = 512
CONV_HIST = 32
SSM_HIST = 8
SEL_KEYS = 256
ATTN_BLOCKS = SEL_KEYS // CHUNK
V_ROWS = 80
GATE_ROWS = 16


def _params(n_axes):
    return pltpu.CompilerParams(dimension_semantics=("arbitrary",) * n_axes,
                                vmem_limit_bytes=VMEM_LIMIT_BYTES)


def _rms(x, g):
    return x * lax.rsqrt(jnp.mean(x * x, axis=-1, keepdims=True) + EPS) * g


def _silu(x):
    return x * jax.nn.sigmoid(x)


def _gelu(x):
    return 0.5 * x * (1.0 + jnp.tanh(math.sqrt(2.0 / math.pi) * (x + 0.044715 * (x * x * x))))


def _dot(a, b):
    return jnp.dot(a, b, preferred_element_type=F32)


def _dot_nt(a, b):
    return lax.dot_general(a, b, (((1,), (1,)), ((), ())), preferred_element_type=F32)


def _dot_tn(a, b):
    return lax.dot_general(a, b, (((0,), (0,)), ((), ())), preferred_element_type=F32)


def _split3(x):
    hi = x.astype(BF)
    r = x - hi.astype(F32)
    mid = r.astype(BF)
    lo = (r - mid.astype(F32)).astype(BF)
    return hi, mid, lo


def _split2(x):
    hi = x.astype(BF)
    lo = (x - hi.astype(F32)).astype(BF)
    return hi, lo


def _layer_spec(shape, layer):
    nd = len(shape)
    return pl.BlockSpec((None,) + tuple(shape[1:]), lambda *_: (layer,) + (0,) * (nd - 1))


def _ffn_chunks():
    tiles = FFN_DIM // MXU_DIM
    assert tiles * MXU_DIM == FFN_DIM
    cut = (tiles // 2) * MXU_DIM
    return ((0, cut), (cut, FFN_DIM))


def _ffn_body(x_ref, g_ref, win_ref, wout_ref, o_ref):
    x = x_ref[...]
    hn = _rms(x, g_ref[...]).astype(win_ref.dtype)
    acc = jnp.zeros_like(x)
    for lo, hi in _ffn_chunks():
        gate = _dot(hn, win_ref[:, lo:hi])
        up = _dot(hn, win_ref[:, FFN_DIM + lo:FFN_DIM + hi])
        a = (_silu(gate) * up).astype(wout_ref.dtype)
        acc = acc + _dot(a, wout_ref[lo:hi, :])
    o_ref[...] = x + 0.5 * acc


def _ffn(x2, norm, w_in, w_out, layer):
    t = x2.shape[0]
    return pl.pallas_call(
        _ffn_body,
        out_shape=jax.ShapeDtypeStruct(x2.shape, F32),
        grid=(t // FFN_TILE,),
        in_specs=[pl.BlockSpec((FFN_TILE, D_MODEL), lambda i: (i, 0)),
                  _layer_spec(norm.shape, layer),
                  _layer_spec(w_in.shape, layer),
                  _layer_spec(w_out.shape, layer)],
        out_specs=pl.BlockSpec((FFN_TILE, D_MODEL), lambda i: (i, 0)),
        compiler_params=_params(1),
        name="ffn",
    )(x2, norm, w_in, w_out)


def _sgu_part(hn, w_ref, vn_ref, ws_ref, bs_ref, o_ref):
    uv = _gelu(_dot(hn, w_ref[...]))
    u = uv[:, :WIDTH]
    v = _rms(uv[:, WIDTH:], vn_ref[...]).astype(BF)
    row = lax.broadcasted_iota(jnp.int32, (CHUNK, CHUNK), 0)
    col = lax.broadcasted_iota(jnp.int32, (CHUNK, CHUNK), 1)
    gw = WIDTH // SGU_GROUPS
    ws = [jnp.where(row >= col, ws_ref[g], 0.0).astype(BF) for g in range(SGU_GROUPS)]
    bias = bs_ref[...]
    for c in range(TOKEN_TILE // CHUNK):
        rows = slice(c * CHUNK, (c + 1) * CHUNK)
        mixed = jnp.concatenate(
            [_dot(ws[g], v[rows, g * gw:(g + 1) * gw]) for g in range(SGU_GROUPS)], axis=1)
        o_ref[rows, :] = (u[rows, :] * (mixed + bias)).astype(BF)


def _conv_part(hn, w_ref, dw_ref, db_ref, cn_ref, o_ref, buf_ref, shift_ref):
    ab = _dot(hn, w_ref[...])
    buf_ref[CONV_HIST:CONV_HIST + TOKEN_TILE, :] = ab[:, :WIDTH] * jax.nn.sigmoid(ab[:, WIDTH:])
    acc = jnp.zeros((TOKEN_TILE, WIDTH), F32) + db_ref[...]
    first = CONV_HIST - (CONV_KERNEL - 1)
    for res in range(min(SUBLANES, CONV_KERNEL)):
        n_taps = (CONV_KERNEL - 1 - res) // SUBLANES + 1
        rows = TOKEN_TILE + (n_taps - 1) * SUBLANES
        shift_ref[0:rows, :] = buf_ref[first + res:first + res + rows, :]
        for a in range(n_taps):
            k = a * SUBLANES + res
            acc = acc + dw_ref[k:k + 1, :] * shift_ref[a * SUBLANES:a * SUBLANES + TOKEN_TILE, :]
    o_ref[...] = _silu(_rms(acc, cn_ref[...])).astype(BF)
    buf_ref[0:CONV_HIST, :] = buf_ref[TOKEN_TILE:TOKEN_TILE + CONV_HIST, :]


def _ssd_part(hn, w_ref, cw_ref, cb_ref, dtb_ref, a_ref, dsk_ref, ng_ref, o_ref, buf_ref, st_ref):
    hw = SSM_REP * SSM_HEAD_DIM
    proj = _dot(hn, w_ref[...])
    z = proj[:, :WIDTH]
    buf_ref[SSM_HIST:SSM_HIST + TOKEN_TILE, :] = proj[:, WIDTH:WIDTH + SSM_CONV_DIM]
    dt = jax.nn.softplus(proj[:, WIDTH + SSM_CONV_DIM:] + dtb_ref[...])
    xbc = jnp.zeros((TOKEN_TILE, SSM_CONV_DIM), F32) + cb_ref[...]
    first = SSM_HIST - (SSM_CONV - 1)
    for k in range(SSM_CONV):
        xbc = xbc + cw_ref[k:k + 1, :] * buf_ref[first + k:first + k + TOKEN_TILE, :]
    buf_ref[0:SSM_HIST, :] = buf_ref[TOKEN_TILE:TOKEN_TILE + SSM_HIST, :]
    xbc = _silu(xbc)
    xs = xbc[:, :WIDTH]
    a_all = dt * a_ref[...]
    xdt = xs * dt

    row = lax.broadcasted_iota(jnp.int32, (CHUNK, CHUNK), 0)
    col = lax.broadcasted_iota(jnp.int32, (CHUNK, CHUNK), 1)
    causal = row >= col
    tril = jnp.where(causal, 1.0, 0.0).astype(BF)
    lane = lax.broadcasted_iota(jnp.int32, (CHUNK, 2 * SSM_HEAD_DIM), 1)
    low_half = lane < SSM_HEAD_DIM

    for c in range(TOKEN_TILE // CHUNK):
        rows = slice(c * CHUNK, (c + 1) * CHUNK)
        hi, mid, lo = _split3(a_all[rows, :])
        cs = _dot(tril, hi) + _dot(tril, mid) + _dot(tril, lo)
        cs_t = cs.T
        cs_last = cs[CHUNK - 1:CHUNK, :]
        xc = xdt[rows, :]
        x_decay = (xc * jnp.exp(cs_last - cs)).astype(BF)
        xc_bf = xc.astype(BF)
        grow = jnp.exp(cs)
        y_parts = []
        for g in range(SSM_GROUPS):
            bg = xbc[rows, WIDTH + g * SSM_STATE:WIDTH + (g + 1) * SSM_STATE].astype(BF)
            cg = xbc[rows, WIDTH + (SSM_GROUPS + g) * SSM_STATE:
                     WIDTH + (SSM_GROUPS + g + 1) * SSM_STATE].astype(BF)
            cb = _dot_nt(cg, bg)
            gl = slice(g * hw, (g + 1) * hw)
            y_off = _dot(cg, st_ref[g].astype(BF)) * grow[:, gl]
            y_diag = []
            for pair in range(SSM_REP // 2):
                scores = []
                for hh in range(2):
                    hc = g * hw + (2 * pair + hh) * SSM_HEAD_DIM
                    seg = cs[:, hc:hc + 1] - cs_t[hc:hc + 1, :]
                    decay = jnp.exp(jnp.where(causal, seg, NEG))
                    scores.append((cb * decay).astype(BF))
                xp = xc_bf[:, g * hw + pair * 2 * SSM_HEAD_DIM:g * hw + (pair + 1) * 2 * SSM_HEAD_DIM]
                zero = jnp.zeros_like(xp)
                rhs = jnp.concatenate([jnp.where(low_half, xp, zero),
                                       jnp.where(low_half, zero, xp)], axis=0)
                y_diag.append(_dot(jnp.concatenate(scores, axis=1), rhs))
            y_parts.append(jnp.concatenate(y_diag, axis=1) + y_off)
            st_ref[g] = jnp.exp(cs_last[:, gl]) * st_ref[g] + _dot_tn(bg, x_decay[:, gl])
        y = jnp.concatenate(y_parts, axis=1) + xs[rows, :] * dsk_ref[...]
        y = y * _silu(z[rows, :])
        o_ref[rows, :] = _rms(y, ng_ref[...]).astype(BF)


def _head_rms(x, ones_blockdiag, gain, scale):
    hi, lo = _split2(x * x)
    ss = _dot(hi, ones_blockdiag) + _dot(lo, ones_blockdiag)
    return x * lax.rsqrt(ss * (1.0 / HEAD_DIM) + EPS) * (gain * scale)


def _nsa_proj_part(hn, w_ref, qn_ref, kn_ref, bdq_ref, bdk_ref, qc_ref, ksc_ref, kwc_ref, vcn_ref,
                   qt_ref, gate_ref, kc_ref, vc_ref, ks_ref, vst_ref, kw_ref, vwt_ref):
    proj = _dot(hn, w_ref[...])
    o = WIDTH
    kc_ref[...] = proj[:, o:o + KV_WIDTH]
    vc_ref[...] = proj[:, o + KV_WIDTH:o + 2 * KV_WIDTH]
    q = _head_rms(proj[:, :WIDTH], bdq_ref[...], qn_ref[...], HEAD_DIM ** -0.5)
    ks = _head_rms(proj[:, o + 2 * KV_WIDTH:o + 3 * KV_WIDTH], bdk_ref[...], kn_ref[...], 1.0)
    kw = _head_rms(proj[:, o + 4 * KV_WIDTH:o + 5 * KV_WIDTH], bdk_ref[...], kn_ref[...], 1.0)
    vs = proj[:, o + 3 * KV_WIDTH:o + 4 * KV_WIDTH]
    vw = proj[:, o + 5 * KV_WIDTH:o + 6 * KV_WIDTH]
    gate_t = jax.nn.sigmoid(proj[:, o + 6 * KV_WIDTH:]).T
    lane = lax.broadcasted_iota(jnp.int32, (TOKEN_TILE, KV_WIDTH), 1)
    gw = NSA_REP * HEAD_DIM
    for g in range(NSA_GROUPS):
        own = (lane >= g * HEAD_DIM) & (lane < (g + 1) * HEAD_DIM)
        ks_ref[g] = (jnp.where(own, ks, 0.0) + ksc_ref[g]).astype(BF)
        kw_ref[g] = (jnp.where(own, kw, 0.0) + kwc_ref[g]).astype(BF)
    vs_t = vs.T.astype(BF)
    for g in range(NSA_GROUPS):
        vst_ref[g, 0, 0:HEAD_DIM, :] = vs_t[g * HEAD_DIM:(g + 1) * HEAD_DIM]
        vst_ref[g, 0, HEAD_DIM:V_ROWS, :] = jnp.concatenate([vcn_ref[...]] * (SEL_KEYS // CHUNK), axis=1)
    for c in range(TOKEN_TILE // CHUNK):
        rows = slice(c * CHUNK, (c + 1) * CHUNK)
        vw_t = vw[rows, :].T.astype(BF)
        for g in range(NSA_GROUPS):
            feat = slice(g * HEAD_DIM, (g + 1) * HEAD_DIM)
            rest = slice((1 - g) * HEAD_DIM, (2 - g) * HEAD_DIM)
            q_t = q[rows, g * gw:(g + 1) * gw].T
            qt_ref[g, c, feat, :] = jnp.concatenate(
                [q_t[r * HEAD_DIM:(r + 1) * HEAD_DIM] for r in range(NSA_REP)], axis=1).astype(BF)
            qt_ref[g, c, rest, :] = qc_ref[g]
            gate_ref[g, c] = gate_t[g * GATE_ROWS:(g + 1) * GATE_ROWS, rows]
            vwt_ref[g, c, 0:HEAD_DIM, :] = vw_t[feat]
            vwt_ref[g, c, HEAD_DIM:V_ROWS, :] = vcn_ref[...]


N_SGU_IN, N_SSD_IN, N_CONV_IN, N_NSA_IN = 4, 7, 4, 9


def _mixers_body(x_ref, g_ref, *refs):
    n_in = N_SGU_IN + N_SSD_IN + N_CONV_IN + N_NSA_IN
    ins, outs, scratch = refs[:n_in], refs[n_in:n_in + 11], refs[n_in + 11:]
    sgu_in, ins = ins[:N_SGU_IN], ins[N_SGU_IN:]
    ssd_in, ins = ins[:N_SSD_IN], ins[N_SSD_IN:]
    conv_in, nsa_in = ins[:N_CONV_IN], ins[N_CONV_IN:]
    ya_ref, yb_ref, yd_ref = outs[:3]
    conv_buf, conv_shift, ssd_buf, ssd_state = scratch

    @pl.when(pl.program_id(1) == 0)
    def _():
        conv_buf[0:CONV_HIST, :] = jnp.zeros((CONV_HIST, WIDTH), F32)
        ssd_buf[0:SSM_HIST, :] = jnp.zeros((SSM_HIST, SSM_CONV_DIM), F32)
        ssd_state[...] = jnp.zeros(ssd_state.shape, F32)

    hn = _rms(x_ref[...], g_ref[...]).astype(BF)
    _sgu_part(hn, *sgu_in, ya_ref)
    _ssd_part(hn, *ssd_in, yb_ref, ssd_buf, ssd_state)
    _conv_part(hn, *conv_in, yd_ref, conv_buf, conv_shift)
    _nsa_proj_part(hn, *nsa_in, *outs[3:])


def _mixers(x3, norm, sgu_p, ssd_p, conv_p, w_c, p, layer):
    b, s, _ = x3.shape
    g, nq, blocks = NSA_GROUPS, NSA_REP * CHUNK, TOKEN_TILE // CHUNK
    n_blk = s // CHUNK
    tile = lambda n: pl.BlockSpec((None, TOKEN_TILE, n), lambda i, j: (i, j, 0))
    const = lambda a: pl.BlockSpec(a.shape, lambda i, j: (0,) * a.ndim)
    lay = lambda a: _layer_spec(a.shape, layer)
    key_const = lambda a: pl.BlockSpec((g, TOKEN_TILE, LANES), lambda i, j: (0, j, 0))
    per_blk = lambda rows, lanes: pl.BlockSpec((None, g, blocks, rows, lanes), lambda i, j: (i, 0, j, 0, 0))
    keys = pl.BlockSpec((None, g, TOKEN_TILE, LANES), lambda i, j: (i, 0, j, 0))
    y = jax.ShapeDtypeStruct((b, s, WIDTH), BF)
    raw = jax.ShapeDtypeStruct((b, s, KV_WIDTH), F32)
    k_aug = jax.ShapeDtypeStruct((b, g, s, LANES), BF)
    v_t = jax.ShapeDtypeStruct((b, g, n_blk, V_ROWS, CHUNK), BF)
    assert TOKEN_TILE == SEL_KEYS
    v_sel = jax.ShapeDtypeStruct((b, g, s // SEL_KEYS, V_ROWS, SEL_KEYS), BF)
    v_sel_spec = pl.BlockSpec((None, g, 1, V_ROWS, SEL_KEYS), lambda i, j: (i, 0, j, 0, 0))
    assert (len(sgu_p), len(ssd_p), len(conv_p)) == (N_SGU_IN, N_SSD_IN, N_CONV_IN)
    nsa_layer = (w_c, p["q_gain"], p["k_gain2"])
    nsa_const = (p["bd_q"], p["bd_k"], p["q_const"])
    layered = tuple(sgu_p) + tuple(ssd_p) + tuple(conv_p) + nsa_layer
    return pl.pallas_call(
        _mixers_body,
        out_shape=(y, y, y,
                   jax.ShapeDtypeStruct((b, g, n_blk, LANES, nq), BF),
                   jax.ShapeDtypeStruct((b, g, n_blk, GATE_ROWS, CHUNK), F32),
                   raw, raw, k_aug, v_sel, k_aug, v_t),
        grid=(b, s // TOKEN_TILE),
        in_specs=[tile(D_MODEL), lay(norm)] + [lay(a) for a in layered] + [const(a) for a in nsa_const]
        + [key_const(p["k_sel"]), key_const(p["k_win"]), const(p["v_const"])],
        out_specs=(tile(WIDTH),) * 3
        + (per_blk(LANES, nq), per_blk(GATE_ROWS, CHUNK), tile(KV_WIDTH), tile(KV_WIDTH),
           keys, v_sel_spec, keys, per_blk(V_ROWS, CHUNK)),
        scratch_shapes=[pltpu.VMEM((TOKEN_TILE + CONV_HIST, WIDTH), F32),
                        pltpu.VMEM((TOKEN_TILE + CONV_HIST, WIDTH), F32),
                        pltpu.VMEM((TOKEN_TILE + SSM_HIST, SSM_CONV_DIM), F32),
                        pltpu.VMEM((SSM_GROUPS, SSM_STATE, SSM_REP * SSM_HEAD_DIM), F32)],
        compiler_params=_params(2),
        name="mixers",
    )(x3, norm, *layered, *nsa_const, p["k_sel"], p["k_win"], p["v_const"])


def _nsa_attn_body(qt_ref, gate_ref, kc_ref, vc_ref, ks_ref, vst_ref, kw_ref, vwt_ref,
                   w1k_ref, w1kf_ref, w2k_ref, pek_ref, w1v_ref, w1vf_ref, w2v_ref, pev_ref, kn_ref,
                   cpos_ref, ovt_ref, wmask_ref, dmask_ref, cmask_ref,
                   o_ref, qs_ref, kcmp_ref, vcmpt_ref, m_ref, acc_ref, part_ref, s_ref):
    step = pl.program_id(1)
    nq = NSA_REP * CHUNK
    n_sel = ovt_ref.shape[0]
    groups = range(NSA_GROUPS)
    q_blocks = range(ATTN_BLOCKS)

    def compress(x_ref, w1_ref, pe_ref, w1f_ref, w2_ref):
        pe_bias = _dot(pe_ref[...], w1f_ref[...])[0:1, :]
        head = [jnp.zeros((CHUNK, CMP_HIDDEN), F32) for _ in groups]
        tail = [jnp.zeros((CHUNK, CMP_HIDDEN), F32) for _ in groups]
        for i in range(CMP_STRIDE):
            x = x_ref[pl.ds(i, CHUNK, stride=CMP_STRIDE), :].astype(BF)
            for g in groups:
                head[g] = head[g] + _dot(x, w1_ref[g, i])
                tail[g] = tail[g] + _dot(x, w1_ref[g, CMP_STRIDE + i])
        return [_dot(_gelu(head[g] + pltpu.roll(tail[g], CHUNK - 1, 0) + pe_bias).astype(BF), w2_ref[...])
                for g in groups]

    @pl.when(step == 0)
    def _():
        k_cmp = compress(kc_ref, w1k_ref, pek_ref, w1kf_ref, w2k_ref)
        v_cmp = compress(vc_ref, w1v_ref, pev_ref, w1vf_ref, w2v_ref)
        lane = lax.broadcasted_iota(jnp.int32, (CHUNK, LANES - HEAD_DIM), 1)
        ones_col = jnp.where(lane == 0, 1.0, 0.0)
        for g in groups:
            halves = [_rms(k_cmp[g], kn_ref[...]).astype(BF), cpos_ref[...]]
            kcmp_ref[g] = jnp.concatenate(halves if g == 0 else halves[::-1], axis=1)
            vcmpt_ref[g] = jnp.concatenate([v_cmp[g], ones_col], axis=1).T[0:V_ROWS, :].astype(BF)

    def select_blocks(p_c, t0):
        p_sum = p_c[:, 0:CHUNK]
        for r in range(1, NSA_REP):
            p_sum = p_sum + p_c[:, r * CHUNK:(r + 1) * CHUNK]
        p_hi, p_lo = _split2(p_sum)
        imp = _dot(ovt_ref[...], p_hi) + _dot(ovt_ref[...], p_lo)
        blk = lax.broadcasted_iota(jnp.int32, (n_sel, CHUNK), 0)
        tq = t0 + lax.broadcasted_iota(jnp.int32, (n_sel, CHUNK), 1)
        cur = tq // SEL_BLOCK
        forced = (blk == 0) | (blk == cur) | (blk == cur - 1)
        score = jnp.where(forced, FORCE, jnp.where(blk * SEL_BLOCK <= tq, imp, NEG))
        blk_f = blk.astype(F32)
        chosen = jnp.zeros((n_sel, CHUNK), F32)
        for _ in range(SEL_TOP_N):
            best = jnp.max(score, axis=0, keepdims=True)
            first = jnp.min(jnp.where(score == best, blk_f, 1e9), axis=0, keepdims=True)
            pick = blk_f == first
            chosen = jnp.where(pick, 1.0, chosen)
            score = jnp.where(pick, -3e38, score)
        return ((chosen - 1.0) * (-NEG)).astype(BF)

    n_w = WINDOW + CHUNK

    def gate_row(g, qb, branch):
        rows = gate_ref[g, qb]
        return jnp.concatenate([rows[branch * NSA_REP + r:branch * NSA_REP + r + 1] for r in range(NSA_REP)], axis=1)

    for g in groups:
        for qb in q_blocks:
            j = step * ATTN_BLOCKS + qb
            t0 = j * CHUNK
            lanes = slice(qb * nq, (qb + 1) * nq)
            qt = qt_ref[g, qb]
            slot = (1 - g) * HEAD_DIM
            cmask = cmask_ref[j]
            s_c = _dot(kcmp_ref[g], qt) + cmask
            e_c = jnp.exp(s_c - jnp.max(s_c, axis=0, keepdims=True))
            sees_any = jnp.where(cmask[0:1, :] == 0.0, 1.0, 0.0)
            p_c = e_c * (sees_any / jnp.sum(e_c, axis=0, keepdims=True))
            o_c = _dot(vcmpt_ref[g], p_c.astype(BF))[0:HEAD_DIM]
            qs_ref[g, :, lanes] = qt
            qs_ref[g, slot:slot + n_sel, lanes] = jnp.concatenate([select_blocks(p_c, t0)] * NSA_REP, axis=1)
            w_blk = jnp.maximum(j - WINDOW // CHUNK, 0)
            w_start = pl.multiple_of(w_blk * CHUNK, CHUNK)
            s_w = _dot(kw_ref[g, pl.ds(w_start, n_w), :], qt) + wmask_ref[j - w_blk]
            p_w = jnp.exp(s_w - jnp.max(s_w, axis=0, keepdims=True)).astype(BF)
            o_w = _dot(vwt_ref[g, w_blk], p_w[0:CHUNK])
            for sub in range(1, n_w // CHUNK):
                o_w = o_w + _dot(vwt_ref[g, w_blk + sub], p_w[sub * CHUNK:(sub + 1) * CHUNK])
            part_ref[g, :, lanes] = (gate_row(g, qb, 0) * o_c
                                     + gate_row(g, qb, 2) * (o_w[0:HEAD_DIM] / o_w[HEAD_DIM:HEAD_DIM + 1]))

    m_ref[...] = jnp.full(m_ref.shape, NEG, F32)
    acc_ref[...] = jnp.zeros(acc_ref.shape, F32)

    def scores(g, key_step):
        start = pl.multiple_of(key_step * SEL_KEYS, SEL_KEYS)
        return _dot(ks_ref[g, pl.ds(start, SEL_KEYS), :], qs_ref[g])

    def absorb(g, s, key_step):
        m_old = m_ref[g]
        m_new = jnp.maximum(m_old, jnp.max(s, axis=0, keepdims=True))
        p = jnp.exp(s - m_new).astype(BF)
        acc_ref[g] = jnp.exp(m_old - m_new) * acc_ref[g] + _dot(vst_ref[g, key_step], p)
        m_ref[g] = m_new

    for g in groups:
        s_ref[g] = scores(g, 0)

    def loop_body(i, carry):
        for g in groups:
            s = s_ref[g]
            s_ref[g] = scores(g, i + 1)
            absorb(g, s, i)
        return carry

    lax.fori_loop(0, step, loop_body, 0)
    for g in groups:
        absorb(g, s_ref[g] + dmask_ref[...], step)
        acc = acc_ref[g]
        o_s = acc[0:HEAD_DIM] / acc[HEAD_DIM:HEAD_DIM + 1]
        gw = NSA_REP * HEAD_DIM
        for qb in q_blocks:
            lanes = slice(qb * nq, (qb + 1) * nq)
            out_t = part_ref[g, :, lanes] + gate_row(g, qb, 1) * o_s[:, lanes]
            by_head = jnp.concatenate([out_t[:, r * CHUNK:(r + 1) * CHUNK] for r in range(NSA_REP)], axis=0)
            o_ref[qb * CHUNK:(qb + 1) * CHUNK, g * gw:(g + 1) * gw] = by_head.T.astype(BF)


def _nsa_attn(qt_aug, gates_t, kc, vc, ks_aug, vs_t, kw_aug, vw_t,
              w1k, w1kf, w2k, pek, w1v, w1vf, w2v, pev, k_gain, cpos, ov_t, win_mask, diag_mask, cmp_mask, layer):
    b, g, nblk, _, nq = qt_aug.shape
    lanes = ATTN_BLOCKS * nq
    blk = lambda a: pl.BlockSpec((None, g, ATTN_BLOCKS) + a.shape[3:], lambda i, j: (i, 0, j, 0, 0))
    seq = lambda a: pl.BlockSpec((None,) + a.shape[1:], lambda i, j: (i,) + (0,) * (a.ndim - 1))
    lay = lambda a: pl.BlockSpec((None,) + a.shape[1:], lambda i, j: (layer,) + (0,) * (a.ndim - 1))
    const = lambda a: pl.BlockSpec(a.shape, lambda i, j: (0,) * a.ndim)
    return pl.pallas_call(
        _nsa_attn_body,
        out_shape=jax.ShapeDtypeStruct((b, nblk * CHUNK, g * NSA_REP * HEAD_DIM), BF),
        grid=(b, nblk // ATTN_BLOCKS),
        in_specs=[blk(qt_aug), blk(gates_t), seq(kc), seq(vc),
                  seq(ks_aug), seq(vs_t), seq(kw_aug), seq(vw_t),
                  lay(w1k), lay(w1kf), lay(w2k), lay(pek),
                  lay(w1v), lay(w1vf), lay(w2v), lay(pev), lay(k_gain),
                  const(cpos), const(ov_t), const(win_mask), const(diag_mask), const(cmp_mask)],
        out_specs=pl.BlockSpec((None, ATTN_BLOCKS * CHUNK, g * NSA_REP * HEAD_DIM), lambda i, j: (i, j, 0)),
        scratch_shapes=[pltpu.VMEM((g, LANES, lanes), BF),
                        pltpu.VMEM((g, CHUNK, LANES), BF),
                        pltpu.VMEM((g, V_ROWS, CHUNK), BF),
                        pltpu.VMEM((g, 1, lanes), F32),
                        pltpu.VMEM((g, V_ROWS, lanes), F32),
                        pltpu.VMEM((g, HEAD_DIM, lanes), F32),
                        pltpu.VMEM((g, SEL_KEYS, lanes), F32)],
        compiler_params=_params(2),
        name="nsa_attn",
    )(qt_aug, gates_t, kc, vc, ks_aug, vs_t, kw_aug, vw_t,
      w1k, w1kf, w2k, pek, w1v, w1vf, w2v, pev, k_gain, cpos, ov_t, win_mask, diag_mask, cmp_mask)


def _merge_body(x_ref, ya_ref, yb_ref, yc_ref, yd_ref, g_ref, wg_ref, wb_ref, wo_ref, o_ref):
    x = x_ref[...]
    hn = _rms(x, g_ref[...]).astype(BF)
    merged = jnp.zeros_like(x)
    for i, y_ref in enumerate((ya_ref, yb_ref, yc_ref, yd_ref)):
        gate = jax.nn.sigmoid(_dot(hn, wg_ref[:, i * D_MODEL:(i + 1) * D_MODEL]))
        merged = merged + gate * _dot(y_ref[...], wb_ref[i])
    o_ref[...] = x + _dot(merged.astype(BF), wo_ref[...])


def _merge(x2, ya, yb, yc, yd, norm, w_gate, w_branch, w_out, layer):
    t = x2.shape[0]
    tile = lambda n: pl.BlockSpec((FFN_TILE, n), lambda i: (i, 0))
    return pl.pallas_call(
        _merge_body,
        out_shape=jax.ShapeDtypeStruct(x2.shape, F32),
        grid=(t // FFN_TILE,),
        in_specs=[tile(D_MODEL)] + [tile(WIDTH)] * 4
        + [_layer_spec(a.shape, layer) for a in (norm, w_gate, w_branch, w_out)],
        out_specs=tile(D_MODEL),
        compiler_params=_params(1),
        name="merge",
    )(x2, ya, yb, yc, yd, norm, w_gate, w_branch, w_out)


def _alibi_slopes():
    return np.array([2.0 ** (-8.0 * (i + 1) / NSA_HEADS) for i in range(NSA_HEADS)], np.float32)


def _nsa_constants(seq):
    n_sel = seq // SEL_BLOCK
    n_cmp = (seq - CMP_BLOCK) // CMP_STRIDE + 1
    assert n_cmp == CHUNK - 1 and n_sel <= SEL_BLOCK // 2
    pos_cols = LANES - HEAD_DIM - n_sel
    slopes = _alibi_slopes().reshape(NSA_GROUPS, NSA_REP)
    q_const = np.zeros((NSA_GROUPS, LANES - HEAD_DIM, NSA_REP * CHUNK), np.float32)
    lane_slope = np.repeat(slopes, CHUNK, axis=1)
    q_const[:, n_sel, :] = lane_slope * SEL_BLOCK
    q_const[:, n_sel + 1, :] = lane_slope
    pos = np.arange(seq)
    pos_part = np.zeros((seq, pos_cols), np.float32)
    pos_part[:, 0] = pos // SEL_BLOCK
    pos_part[:, 1] = pos % SEL_BLOCK
    onehot = (pos[:, None] // SEL_BLOCK == np.arange(n_sel)[None, :]).astype(np.float32)

    def key_const(first):
        half = np.concatenate([first, pos_part], axis=1)
        return np.stack([np.concatenate([np.zeros_like(half), half][::1 if g == 0 else -1], axis=1)
                         for g in range(NSA_GROUPS)])

    k_sel_const = key_const(onehot)
    k_win_const = key_const(np.zeros_like(onehot))
    v_const = np.zeros((V_ROWS - HEAD_DIM, CHUNK), np.float32)
    v_const[0, :] = 1.0
    c_mid = np.arange(CHUNK) * CMP_STRIDE + (CMP_BLOCK - 1) / 2.0
    cmp_const = np.zeros((CHUNK, LANES - HEAD_DIM), np.float32)
    cmp_const[:, n_sel] = c_mid // SEL_BLOCK
    cmp_const[:, n_sel + 1] = c_mid % SEL_BLOCK
    c_start = np.arange(CHUNK) * CMP_STRIDE
    s_start = np.arange(n_sel) * SEL_BLOCK
    overlap_t = ((c_start[None, :] <= s_start[:, None] + SEL_BLOCK - 1)
                 & (c_start[None, :] + CMP_BLOCK - 1 >= s_start[:, None])
                 & (np.arange(CHUNK)[None, :] < n_cmp)).astype(np.float32)
    qi = np.arange(NSA_REP * CHUNK)[None, :] % CHUNK

    def mask(rows, n_cases, valid):
        ki = np.arange(rows)[:, None]
        return np.stack([np.where(valid(d * CHUNK + qi - ki), 0.0, NEG) for d in range(n_cases)]).astype(np.float32)

    win_mask = mask(WINDOW + CHUNK, WINDOW // CHUNK + 1, lambda rel: (rel >= 0) & (rel < WINDOW))
    diag_mask = np.concatenate(list(mask(SEL_KEYS, ATTN_BLOCKS, lambda rel: rel >= 0)), axis=1)
    c_last = (np.arange(CHUNK) * CMP_STRIDE + CMP_BLOCK - 1)[None, :, None]
    t_query = (np.arange(seq // CHUNK) * CHUNK)[:, None, None] + qi[None]
    cmp_mask = np.where((c_last <= t_query) & (np.arange(CHUNK) < n_cmp)[None, :, None], 0.0, NEG).astype(np.float32)
    bf = lambda a: jnp.asarray(a, BF)
    return dict(q_const=bf(q_const), k_sel=bf(k_sel_const), k_win=bf(k_win_const), v_const=bf(v_const),
                cmp_const=bf(cmp_const), overlap_t=bf(overlap_t),
                win_mask=jnp.asarray(win_mask), diag_mask=jnp.asarray(diag_mask), cmp_mask=jnp.asarray(cmp_mask))


def _block_diag_ones(n):
    idx = np.arange(n) // HEAD_DIM
    return jnp.asarray(idx[:, None] == idx[None, :], BF)


def _compress_weights(w1):
    nl = w1.shape[0]
    per_pos = w1.reshape(nl, 1, CMP_BLOCK, 1, HEAD_DIM, CMP_HIDDEN)
    group_sel = jnp.eye(NSA_GROUPS, dtype=w1.dtype)
    big = per_pos * group_sel[None, :, None, :, None, None]
    return big.reshape(nl, NSA_GROUPS, CMP_BLOCK, KV_WIDTH, CMP_HIDDEN).astype(BF)


def _nsa_proj_weights(w_in):
    depth = w_in.shape[0]
    gate = w_in[:, :, OFF_GATE:OFF_AB].reshape(depth, D_MODEL, NSA_GROUPS, NSA_REP, 3)
    gate = jnp.swapaxes(gate, 3, 4).reshape(depth, D_MODEL, NSA_GROUPS, 3 * NSA_REP)
    gate = jnp.pad(gate, ((0, 0), (0, 0), (0, 0), (0, GATE_ROWS - 3 * NSA_REP)))
    gate = gate.reshape(depth, D_MODEL, NSA_GROUPS * GATE_ROWS)
    gate = jnp.pad(gate, ((0, 0), (0, 0), (0, LANES - NSA_GROUPS * GATE_ROWS)))
    return jnp.concatenate([w_in[:, :, OFF_Q:OFF_GATE], gate], axis=-1).astype(BF)


def _pe_rows(pe):
    nl = pe.shape[0]
    flat = pe.reshape(nl, 1, CMP_BLOCK * HEAD_DIM)
    return jnp.concatenate([flat, jnp.zeros((nl, 7, CMP_BLOCK * HEAD_DIM), pe.dtype)], axis=1).astype(BF)


def _nsa_params(seq, q_norm, k_norm, pe_k, w1_k, w2_k, pe_v, w1_v, w2_v):
    row = lambda a: a[:, None, :]
    p = _nsa_constants(seq)
    p.update(q_gain=row(jnp.tile(q_norm, (1, NSA_HEADS))), k_gain2=row(jnp.tile(k_norm, (1, NSA_GROUPS))),
             k_gain=row(k_norm), w1k=_compress_weights(w1_k), w1v=_compress_weights(w1_v),
             w1kf=w1_k.astype(BF), w1vf=w1_v.astype(BF), w2k=w2_k.astype(BF), w2v=w2_v.astype(BF),
             pek=_pe_rows(pe_k), pev=_pe_rows(pe_v),
             bd_q=_block_diag_ones(WIDTH), bd_k=_block_diag_ones(KV_WIDTH))
    return p


def _token_mixers(x3, mxn, sgu_p, ssd_p, conv_p, w_c, p, layer):
    y_a, y_b, y_d, *attn_in = _mixers(x3, mxn, sgu_p, ssd_p, conv_p, w_c, p, layer)
    y_c = _nsa_attn(*attn_in,
                    p["w1k"], p["w1kf"], p["w2k"], p["pek"], p["w1v"], p["w1vf"], p["w2v"], p["pev"], p["k_gain"],
                    p["cmp_const"], p["overlap_t"], p["win_mask"], p["diag_mask"], p["cmp_mask"], layer)
    return y_a, y_b, y_c, y_d


def kernel(x, ffn1_norm, ffn1_w_in, ffn1_w_out, mix_norm, w_in, sgu_v_norm, sgu_w, sgu_b, ssm_conv_w, ssm_conv_b, ssm_dt_bias, ssm_a_log, ssm_d, ssm_norm, nsa_q_norm, nsa_k_norm, nsa_pe_k, nsa_w1_k, nsa_w2_k, nsa_pe_v, nsa_w1_v, nsa_w2_v, conv_dw_w, conv_dw_b, conv_norm, w_branch, w_out, ffn2_norm, ffn2_w_in, ffn2_w_out):
    bsz, seq, _ = x.shape
    depth = w_in.shape[0]
    n_blk = seq // CHUNK
    row = lambda a: a[:, None, :]
    rep = lambda a, n: jnp.repeat(a, n, axis=-1)

    f1n, f2n, mxn = row(ffn1_norm), row(ffn2_norm), row(mix_norm)
    f1_in, f1_out = ffn1_w_in, ffn1_w_out
    f2_in, f2_out = ffn2_w_in, ffn2_w_out
    w_a = w_in[:, :, OFF_UV:OFF_Z].astype(BF)
    w_b = jnp.concatenate([w_in[:, :, OFF_Z:OFF_DT], rep(w_in[:, :, OFF_DT:OFF_Q], SSM_HEAD_DIM)], axis=-1).astype(BF)
    w_c = _nsa_proj_weights(w_in)
    w_d = w_in[:, :, OFF_AB:OFF_MERGE].astype(BF)
    w_g = w_in[:, :, OFF_MERGE:].astype(BF)
    sgu_bias = rep(jnp.swapaxes(sgu_b, 1, 2), WIDTH // SGU_GROUPS)
    dt_bias = row(rep(ssm_dt_bias, SSM_HEAD_DIM))
    a_neg = row(rep(-jnp.exp(ssm_a_log), SSM_HEAD_DIM))
    d_skip = row(rep(ssm_d, SSM_HEAD_DIM))
    wb, wo = w_branch.astype(BF), w_out.astype(BF)
    nsa_p = _nsa_params(seq, nsa_q_norm, nsa_k_norm, nsa_pe_k, nsa_w1_k, nsa_w2_k, nsa_pe_v, nsa_w1_v, nsa_w2_v)
    sgu_p = (w_a, row(sgu_v_norm), sgu_w, sgu_bias)
    ssd_p = (w_b, ssm_conv_w, row(ssm_conv_b), dt_bias, a_neg, d_skip, row(ssm_norm))
    conv_p = (w_d, conv_dw_w, row(conv_dw_b), row(conv_norm))

    x2 = x.reshape(bsz * seq, D_MODEL)
    for l in range(depth):
        x2 = _ffn(x2, f1n, f1_in, f1_out, l)
        x3 = x2.reshape(bsz, seq, D_MODEL)
        y_a, y_b, y_c, y_d = _token_mixers(x3, mxn, sgu_p, ssd_p, conv_p, w_c, nsa_p, l)
        flat = lambda y: y.reshape(bsz * seq, WIDTH)
        x2 = _merge(x2, flat(y_a), flat(y_b), flat(y_c), flat(y_d), mxn, w_g, wb, wo, l)
        x2 = _ffn(x2, f2n, f2_in, f2_out, l)
    return x2.reshape(bsz, seq, D_MODEL)
```

```python
import math

import numpy as np
import jax
import jax.numpy as jnp
from jax import lax
from jax.experimental import pallas as pl
from jax.experimental.pallas import tpu as pltpu

F32 = jnp.float32
BF = jnp.bfloat16

D_MODEL = 1024
FFN_DIM = 2816
WIDTH = 512
SGU_GROUPS = 4
CHUNK = 128
SSM_HEADS = 8
SSM_HEAD_DIM = 64
SSM_GROUPS = 2
SSM_STATE = 128
SSM_CONV = 4
SSM_CONV_DIM = WIDTH + 2 * SSM_GROUPS * SSM_STATE
SSM_REP = SSM_HEADS // SSM_GROUPS
NSA_HEADS = 8
NSA_GROUPS = 2
NSA_REP = NSA_HEADS // NSA_GROUPS
HEAD_DIM = 64
KV_WIDTH = NSA_GROUPS * HEAD_DIM
CMP_BLOCK = 32
CMP_STRIDE = 16
CMP_HIDDEN = 128
SEL_BLOCK = 64
SEL_TOP_N = 8
WINDOW = 256
CONV_KERNEL = 31
EPS = 1e-6
NEG = -1e30
FORCE = 1e9

OFF_UV = 0
OFF_Z = OFF_UV + 2 * WIDTH
OFF_XBC = OFF_Z + WIDTH
OFF_DT = OFF_XBC + SSM_CONV_DIM
OFF_Q = OFF_DT + SSM_HEADS
OFF_KV = OFF_Q + WIDTH
OFF_GATE = OFF_KV + 6 * KV_WIDTH
OFF_AB = OFF_GATE + 3 * NSA_HEADS
OFF_MERGE = OFF_AB + 2 * WIDTH

LANES = 128
SUBLANES = 8
MXU_DIM = 256
VMEM_LIMIT_BYTES = 56 * 1024 * 1024

TOKEN_TILE = 256
FFN_TILE = 512
CONV_HIST = 32
SSM_HIST = 8
SEL_KEYS = 256
ATTN_BLOCKS = SEL_KEYS // CHUNK
V_ROWS = 80
GATE_ROWS = 16


def _params(n_axes):
    return pltpu.CompilerParams(dimension_semantics=("arbitrary",) * n_axes,
                                vmem_limit_bytes=VMEM_LIMIT_BYTES)


def _rms(x, g):
    return x * lax.rsqrt(jnp.mean(x * x, axis=-1, keepdims=True) + EPS) * g


def _silu(x):
    return x * jax.nn.sigmoid(x)


def _gelu(x):
    return 0.5 * x * (1.0 + jnp.tanh(math.sqrt(2.0 / math.pi) * (x + 0.044715 * (x * x * x))))


def _dot(a, b):
    return jnp.dot(a, b, preferred_element_type=F32)


def _dot_nt(a, b):
    return lax.dot_general(a, b, (((1,), (1,)), ((), ())), preferred_element_type=F32)


def _dot_tn(a, b):
    return lax.dot_general(a, b, (((0,), (0,)), ((), ())), preferred_element_type=F32)


def _split3(x):
    hi = x.astype(BF)
    r = x - hi.astype(F32)
    mid = r.astype(BF)
    lo = (r - mid.astype(F32)).astype(BF)
    return hi, mid, lo


def _split2(x):
    hi = x.astype(BF)
    lo = (x - hi.astype(F32)).astype(BF)
    return hi, lo


def _layer_spec(shape, layer):
    nd = len(shape)
    return pl.BlockSpec((None,) + tuple(shape[1:]), lambda *_: (layer,) + (0,) * (nd - 1))


def _ffn_chunks():
    tiles = FFN_DIM // MXU_DIM
    assert tiles * MXU_DIM == FFN_DIM
    cut = (tiles // 2) * MXU_DIM
    return ((0, cut), (cut, FFN_DIM))


def _ffn_body(x_ref, g_ref, win_ref, wout_ref, o_ref):
    x = x_ref[...]
    hn = _rms(x, g_ref[...]).astype(win_ref.dtype)
    acc = jnp.zeros_like(x)
    for lo, hi in _ffn_chunks():
        gate = _dot(hn, win_ref[:, lo:hi])
        up = _dot(hn, win_ref[:, FFN_DIM + lo:FFN_DIM + hi])
        a = (_silu(gate) * up).astype(wout_ref.dtype)
        acc = acc + _dot(a, wout_ref[lo:hi, :])
    o_ref[...] = x + 0.5 * acc


def _ffn(x2, norm, w_in, w_out, layer):
    t = x2.shape[0]
    return pl.pallas_call(
        _ffn_body,
        out_shape=jax.ShapeDtypeStruct(x2.shape, F32),
        grid=(t // FFN_TILE,),
        in_specs=[pl.BlockSpec((FFN_TILE, D_MODEL), lambda i: (i, 0)),
                  _layer_spec(norm.shape, layer),
                  _layer_spec(w_in.shape, layer),
                  _layer_spec(w_out.shape, layer)],
        out_specs=pl.BlockSpec((FFN_TILE, D_MODEL), lambda i: (i, 0)),
        compiler_params=_params(1),
        name="ffn",
    )(x2, norm, w_in, w_out)


def _sgu_part(hn, w_ref, vn_ref, ws_ref, bs_ref, o_ref):
    uv = _gelu(_dot(hn, w_ref[...]))
    u = uv[:, :WIDTH]
    v = _rms(uv[:, WIDTH:], vn_ref[...]).astype(BF)
    row = lax.broadcasted_iota(jnp.int32, (CHUNK, CHUNK), 0)
    col = lax.broadcasted_iota(jnp.int32, (CHUNK, CHUNK), 1)
    gw = WIDTH // SGU_GROUPS
    ws = [jnp.where(row >= col, ws_ref[g], 0.0).astype(BF) for g in range(SGU_GROUPS)]
    bias = bs_ref[...]
    for c in range(TOKEN_TILE // CHUNK):
        rows = slice(c * CHUNK, (c + 1) * CHUNK)
        mixed = jnp.concatenate(
            [_dot(ws[g], v[rows, g * gw:(g + 1) * gw]) for g in range(SGU_GROUPS)], axis=1)
        o_ref[rows, :] = (u[rows, :] * (mixed + bias)).astype(BF)


def _conv_part(hn, w_ref, dw_ref, db_ref, cn_ref, o_ref, buf_ref, shift_ref):
    ab = _dot(hn, w_ref[...])
    buf_ref[CONV_HIST:CONV_HIST + TOKEN_TILE, :] = ab[:, :WIDTH] * jax.nn.sigmoid(ab[:, WIDTH:])
    acc = jnp.zeros((TOKEN_TILE, WIDTH), F32) + db_ref[...]
    first = CONV_HIST - (CONV_KERNEL - 1)
    for res in range(min(SUBLANES, CONV_KERNEL)):
        n_taps = (CONV_KERNEL - 1 - res) // SUBLANES + 1
        rows = TOKEN_TILE + (n_taps - 1) * SUBLANES
        shift_ref[0:rows, :] = buf_ref[first + res:first + res + rows, :]
        for a in range(n_taps):
            k = a * SUBLANES + res
            acc = acc + dw_ref[k:k + 1, :] * shift_ref[a * SUBLANES:a * SUBLANES + TOKEN_TILE, :]
    o_ref[...] = _silu(_rms(acc, cn_ref[...])).astype(BF)
    buf_ref[0:CONV_HIST, :] = buf_ref[TOKEN_TILE:TOKEN_TILE + CONV_HIST, :]


def _ssd_part(hn, w_ref, cw_ref, cb_ref, dtb_ref, a_ref, dsk_ref, ng_ref, o_ref, buf_ref, st_ref):
    hw = SSM_REP * SSM_HEAD_DIM
    proj = _dot(hn, w_ref[...])
    z = proj[:, :WIDTH]
    buf_ref[SSM_HIST:SSM_HIST + TOKEN_TILE, :] = proj[:, WIDTH:WIDTH + SSM_CONV_DIM]
    dt = jax.nn.softplus(proj[:, WIDTH + SSM_CONV_DIM:] + dtb_ref[...])
    xbc = jnp.zeros((TOKEN_TILE, SSM_CONV_DIM), F32) + cb_ref[...]
    first = SSM_HIST - (SSM_CONV - 1)
    for k in range(SSM_CONV):
        xbc = xbc + cw_ref[k:k + 1, :] * buf_ref[first + k:first + k + TOKEN_TILE, :]
    buf_ref[0:SSM_HIST, :] = buf_ref[TOKEN_TILE:TOKEN_TILE + SSM_HIST, :]
    xbc = _silu(xbc)
    xs = xbc[:, :WIDTH]
    a_all = dt * a_ref[...]
    xdt = xs * dt

    row = lax.broadcasted_iota(jnp.int32, (CHUNK, CHUNK), 0)
    col = lax.broadcasted_iota(jnp.int32, (CHUNK, CHUNK), 1)
    causal = row >= col
    tril = jnp.where(causal, 1.0, 0.0).astype(BF)
    lane = lax.broadcasted_iota(jnp.int32, (CHUNK, 2 * SSM_HEAD_DIM), 1)
    low_half = lane < SSM_HEAD_DIM

    for c in range(TOKEN_TILE // CHUNK):
        rows = slice(c * CHUNK, (c + 1) * CHUNK)
        hi, mid, lo = _split3(a_all[rows, :])
        cs = _dot(tril, hi) + _dot(tril, mid) + _dot(tril, lo)
        cs_t = cs.T
        cs_last = cs[CHUNK - 1:CHUNK, :]
        xc = xdt[rows, :]
        x_decay = (xc * jnp.exp(cs_last - cs)).astype(BF)
        xc_bf = xc.astype(BF)
        grow = jnp.exp(cs)
        y_parts = []
        for g in range(SSM_GROUPS):
            bg = xbc[rows, WIDTH + g * SSM_STATE:WIDTH + (g + 1) * SSM_STATE].astype(BF)
            cg = xbc[rows, WIDTH + (SSM_GROUPS + g) * SSM_STATE:
                     WIDTH + (SSM_GROUPS + g + 1) * SSM_STATE].astype(BF)
            cb = _dot_nt(cg, bg)
            gl = slice(g * hw, (g + 1) * hw)
            y_off = _dot(cg, st_ref[g].astype(BF)) * grow[:, gl]
            y_diag = []
            for pair in range(SSM_REP // 2):
                scores = []
                for hh in range(2):
                    hc = g * hw + (2 * pair + hh) * SSM_HEAD_DIM
                    seg = cs[:, hc:hc + 1] - cs_t[hc:hc + 1, :]
                    decay = jnp.exp(jnp.where(causal, seg, NEG))
                    scores.append((cb * decay).astype(BF))
                xp = xc_bf[:, g * hw + pair * 2 * SSM_HEAD_DIM:g * hw + (pair + 1) * 2 * SSM_HEAD_DIM]
                zero = jnp.zeros_like(xp)
                rhs = jnp.concatenate([jnp.where(low_half, xp, zero),
                                       jnp.where(low_half, zero, xp)], axis=0)
                y_diag.append(_dot(jnp.concatenate(scores, axis=1), rhs))
            y_parts.append(jnp.concatenate(y_diag, axis=1) + y_off)
            st_ref[g] = jnp.exp(cs_last[:, gl]) * st_ref[g] + _dot_tn(bg, x_decay[:, gl])
        y = jnp.concatenate(y_parts, axis=1) + xs[rows, :] * dsk_ref[...]
        y = y * _silu(z[rows, :])
        o_ref[rows, :] = _rms(y, ng_ref[...]).astype(BF)


def _head_rms(x, ones_blockdiag, gain, scale):
    hi, lo = _split2(x * x)
    ss = _dot(hi, ones_blockdiag) + _dot(lo, ones_blockdiag)
    return x * lax.rsqrt(ss * (1.0 / HEAD_DIM) + EPS) * (gain * scale)


def _nsa_proj_part(hn, w_ref, qn_ref, kn_ref, bdq_ref, bdk_ref, qc_ref, ksc_ref, kwc_ref, vcn_ref,
                   qt_ref, gate_ref, kc_ref, vc_ref, ks_ref, vst_ref, kw_ref, vwt_ref):
    proj = _dot(hn, w_ref[...])
    o = WIDTH
    kc_ref[...] = proj[:, o:o + KV_WIDTH]
    vc_ref[...] = proj[:, o + KV_WIDTH:o + 2 * KV_WIDTH]
    q = _head_rms(proj[:, :WIDTH], bdq_ref[...], qn_ref[...], HEAD_DIM ** -0.5)
    ks = _head_rms(proj[:, o + 2 * KV_WIDTH:o + 3 * KV_WIDTH], bdk_ref[...], kn_ref[...], 1.0)
    kw = _head_rms(proj[:, o + 4 * KV_WIDTH:o + 5 * KV_WIDTH], bdk_ref[...], kn_ref[...], 1.0)
    vs = proj[:, o + 3 * KV_WIDTH:o + 4 * KV_WIDTH]
    vw = proj[:, o + 5 * KV_WIDTH:o + 6 * KV_WIDTH]
    gate_t = jax.nn.sigmoid(proj[:, o + 6 * KV_WIDTH:]).T
    lane = lax.broadcasted_iota(jnp.int32, (TOKEN_TILE, KV_WIDTH), 1)
    gw = NSA_REP * HEAD_DIM
    for g in range(NSA_GROUPS):
        own = (lane >= g * HEAD_DIM) & (lane < (g + 1) * HEAD_DIM)
        ks_ref[g] = (jnp.where(own, ks, 0.0) + ksc_ref[g]).astype(BF)
        kw_ref[g] = (jnp.where(own, kw, 0.0) + kwc_ref[g]).astype(BF)
    vs_t = vs.T.astype(BF)
    for g in range(NSA_GROUPS):
        vst_ref[g, 0, 0:HEAD_DIM, :] = vs_t[g * HEAD_DIM:(g + 1) * HEAD_DIM]
        vst_ref[g, 0, HEAD_DIM:V_ROWS, :] = jnp.concatenate([vcn_ref[...]] * (SEL_KEYS // CHUNK), axis=1)
    for c in range(TOKEN_TILE // CHUNK):
        rows = slice(c * CHUNK, (c + 1) * CHUNK)
        vw_t = vw[rows, :].T.astype(BF)
        for g in range(NSA_GROUPS):
            feat = slice(g * HEAD_DIM, (g + 1) * HEAD_DIM)
            rest = slice((1 - g) * HEAD_DIM, (2 - g) * HEAD_DIM)
            q_t = q[rows, g * gw:(g + 1) * gw].T
            qt_ref[g, c, feat, :] = jnp.concatenate(
                [q_t[r * HEAD_DIM:(r + 1) * HEAD_DIM] for r in range(NSA_REP)], axis=1).astype(BF)
            qt_ref[g, c, rest, :] = qc_ref[g]
            gate_ref[g, c] = gate_t[g * GATE_ROWS:(g + 1) * GATE_ROWS, rows]
            vwt_ref[g, c, 0:HEAD_DIM, :] = vw_t[feat]
            vwt_ref[g, c, HEAD_DIM:V_ROWS, :] = vcn_ref[...]


N_SGU_IN, N_SSD_IN, N_CONV_IN, N_NSA_IN = 4, 7, 4, 9


def _mixers_body(x_ref, g_ref, *refs):
    n_in = N_SGU_IN + N_SSD_IN + N_CONV_IN + N_NSA_IN
    ins, outs, scratch = refs[:n_in], refs[n_in:n_in + 11], refs[n_in + 11:]
    sgu_in, ins = ins[:N_SGU_IN], ins[N_SGU_IN:]
    ssd_in, ins = ins[:N_SSD_IN], ins[N_SSD_IN:]
    conv_in, nsa_in = ins[:N_CONV_IN], ins[N_CONV_IN:]
    ya_ref, yb_ref, yd_ref = outs[:3]
    conv_buf, conv_shift, ssd_buf, ssd_state = scratch

    @pl.when(pl.program_id(1) == 0)
    def _():
        conv_buf[0:CONV_HIST, :] = jnp.zeros((CONV_HIST, WIDTH), F32)
        ssd_buf[0:SSM_HIST, :] = jnp.zeros((SSM_HIST, SSM_CONV_DIM), F32)
        ssd_state[...] = jnp.zeros(ssd_state.shape, F32)

    hn = _rms(x_ref[...], g_ref[...]).astype(BF)
    _nsa_proj_part(hn, *nsa_in, *outs[3:])
    _conv_part(hn, *conv_in, yd_ref, conv_buf, conv_shift)
    _sgu_part(hn, *sgu_in, ya_ref)
    _ssd_part(hn, *ssd_in, yb_ref, ssd_buf, ssd_state)


def _mixers(x3, norm, sgu_p, ssd_p, conv_p, w_c, p, layer):
    b, s, _ = x3.shape
    g, nq, blocks = NSA_GROUPS, NSA_REP * CHUNK, TOKEN_TILE // CHUNK
    n_blk = s // CHUNK
    tile = lambda n: pl.BlockSpec((None, TOKEN_TILE, n), lambda i, j: (i, j, 0))
    const = lambda a: pl.BlockSpec(a.shape, lambda i, j: (0,) * a.ndim)
    lay = lambda a: _layer_spec(a.shape, layer)
    key_const = lambda a: pl.BlockSpec((g, TOKEN_TILE, LANES), lambda i, j: (0, j, 0))
    per_blk = lambda rows, lanes: pl.BlockSpec((None, g, blocks, rows, lanes), lambda i, j: (i, 0, j, 0, 0))
    keys = pl.BlockSpec((None, g, TOKEN_TILE, LANES), lambda i, j: (i, 0, j, 0))
    y = jax.ShapeDtypeStruct((b, s, WIDTH), BF)
    raw = jax.ShapeDtypeStruct((b, s, KV_WIDTH), F32)
    k_aug = jax.ShapeDtypeStruct((b, g, s, LANES), BF)
    v_t = jax.ShapeDtypeStruct((b, g, n_blk, V_ROWS, CHUNK), BF)
    assert TOKEN_TILE == SEL_KEYS
    v_sel = jax.ShapeDtypeStruct((b, g, s // SEL_KEYS, V_ROWS, SEL_KEYS), BF)
    v_sel_spec = pl.BlockSpec((None, g, 1, V_ROWS, SEL_KEYS), lambda i, j: (i, 0, j, 0, 0))
    assert (len(sgu_p), len(ssd_p), len(conv_p)) == (N_SGU_IN, N_SSD_IN, N_CONV_IN)
    nsa_layer = (w_c, p["q_gain"], p["k_gain2"])
    nsa_const = (p["bd_q"], p["bd_k"], p["q_const"])
    layered = tuple(sgu_p) + tuple(ssd_p) + tuple(conv_p) + nsa_layer
    return pl.pallas_call(
        _mixers_body,
        out_shape=(y, y, y,
                   jax.ShapeDtypeStruct((b, g, n_blk, LANES, nq), BF),
                   jax.ShapeDtypeStruct((b, g, n_blk, GATE_ROWS, CHUNK), F32),
                   raw, raw, k_aug, v_sel, k_aug, v_t),
        grid=(b, s // TOKEN_TILE),
        in_specs=[tile(D_MODEL), lay(norm)] + [lay(a) for a in layered] + [const(a) for a in nsa_const]
        + [key_const(p["k_sel"]), key_const(p["k_win"]), const(p["v_const"])],
        out_specs=(tile(WIDTH),) * 3
        + (per_blk(LANES, nq), per_blk(GATE_ROWS, CHUNK), tile(KV_WIDTH), tile(KV_WIDTH),
           keys, v_sel_spec, keys, per_blk(V_ROWS, CHUNK)),
        scratch_shapes=[pltpu.VMEM((TOKEN_TILE + CONV_HIST, WIDTH), F32),
                        pltpu.VMEM((TOKEN_TILE + CONV_HIST, WIDTH), F32),
                        pltpu.VMEM((TOKEN_TILE + SSM_HIST, SSM_CONV_DIM), F32),
                        pltpu.VMEM((SSM_GROUPS, SSM_STATE, SSM_REP * SSM_HEAD_DIM), F32)],
        compiler_params=_params(2),
        name="mixers",
    )(x3, norm, *layered, *nsa_const, p["k_sel"], p["k_win"], p["v_const"])


def _nsa_attn_body(qt_ref, gate_ref, kc_ref, vc_ref, ks_ref, vst_ref, kw_ref, vwt_ref,
                   w1k_ref, w1kf_ref, w2k_ref, pek_ref, w1v_ref, w1vf_ref, w2v_ref, pev_ref, kn_ref,
                   cpos_ref, ovt_ref, wmask_ref, dmask_ref, cmask_ref,
                   o_ref, qs_ref, kcmp_ref, vcmpt_ref, m_ref, acc_ref, part_ref, s_ref):
    step = pl.program_id(1)
    nq = NSA_REP * CHUNK
    n_sel = ovt_ref.shape[0]
    groups = range(NSA_GROUPS)
    q_blocks = range(ATTN_BLOCKS)

    def compress(x_ref, w1_ref, pe_ref, w1f_ref, w2_ref):
        pe_bias = _dot(pe_ref[...], w1f_ref[...])[0:1, :]
        head = [jnp.zeros((CHUNK, CMP_HIDDEN), F32) for _ in groups]
        tail = [jnp.zeros((CHUNK, CMP_HIDDEN), F32) for _ in groups]
        for i in range(CMP_STRIDE):
            x = x_ref[pl.ds(i, CHUNK, stride=CMP_STRIDE), :].astype(BF)
            for g in groups:
                head[g] = head[g] + _dot(x, w1_ref[g, i])
                tail[g] = tail[g] + _dot(x, w1_ref[g, CMP_STRIDE + i])
        return [_dot(_gelu(head[g] + pltpu.roll(tail[g], CHUNK - 1, 0) + pe_bias).astype(BF), w2_ref[...])
                for g in groups]

    @pl.when(step == 0)
    def _():
        k_cmp = compress(kc_ref, w1k_ref, pek_ref, w1kf_ref, w2k_ref)
        v_cmp = compress(vc_ref, w1v_ref, pev_ref, w1vf_ref, w2v_ref)
        lane = lax.broadcasted_iota(jnp.int32, (CHUNK, LANES - HEAD_DIM), 1)
        ones_col = jnp.where(lane == 0, 1.0, 0.0)
        for g in groups:
            halves = [_rms(k_cmp[g], kn_ref[...]).astype(BF), cpos_ref[...]]
            kcmp_ref[g] = jnp.concatenate(halves if g == 0 else halves[::-1], axis=1)
            vcmpt_ref[g] = jnp.concatenate([v_cmp[g], ones_col], axis=1).T[0:V_ROWS, :].astype(BF)

    def select_blocks(p_c, t0):
        p_sum = p_c[:, 0:CHUNK]
        for r in range(1, NSA_REP):
            p_sum = p_sum + p_c[:, r * CHUNK:(r + 1) * CHUNK]
        p_hi, p_lo = _split2(p_sum)
        imp = _dot(ovt_ref[...], p_hi) + _dot(ovt_ref[...], p_lo)
        blk = lax.broadcasted_iota(jnp.int32, (n_sel, CHUNK), 0)
        tq = t0 + lax.broadcasted_iota(jnp.int32, (n_sel, CHUNK), 1)
        cur = tq // SEL_BLOCK
        forced = (blk == 0) | (blk == cur) | (blk == cur - 1)
        score = jnp.where(forced, FORCE, jnp.where(blk * SEL_BLOCK <= tq, imp, NEG))
        blk_f = blk.astype(F32)
        chosen = jnp.zeros((n_sel, CHUNK), F32)
        for _ in range(SEL_TOP_N):
            best = jnp.max(score, axis=0, keepdims=True)
            first = jnp.min(jnp.where(score == best, blk_f, 1e9), axis=0, keepdims=True)
            pick = blk_f == first
            chosen = jnp.where(pick, 1.0, chosen)
            score = jnp.where(pick, -3e38, score)
        return ((chosen - 1.0) * (-NEG)).astype(BF)

    n_w = WINDOW + CHUNK

    def gate_row(g, qb, branch):
        rows = gate_ref[g, qb]
        return jnp.concatenate([rows[branch * NSA_REP + r:branch * NSA_REP + r + 1] for r in range(NSA_REP)], axis=1)

    for g in groups:
        for qb in q_blocks:
            j = step * ATTN_BLOCKS + qb
            t0 = j * CHUNK
            lanes = slice(qb * nq, (qb + 1) * nq)
            qt = qt_ref[g, qb]
            slot = (1 - g) * HEAD_DIM
            cmask = cmask_ref[j]
            s_c = _dot(kcmp_ref[g], qt) + cmask
            e_c = jnp.exp(s_c - jnp.max(s_c, axis=0, keepdims=True))
            sees_any = jnp.where(cmask[0:1, :] == 0.0, 1.0, 0.0)
            p_c = e_c * (sees_any / jnp.sum(e_c, axis=0, keepdims=True))
            o_c = _dot(vcmpt_ref[g], p_c.astype(BF))[0:HEAD_DIM]
            qs_ref[g, :, lanes] = qt
            qs_ref[g, slot:slot + n_sel, lanes] = jnp.concatenate([select_blocks(p_c, t0)] * NSA_REP, axis=1)
            w_blk = jnp.maximum(j - WINDOW // CHUNK, 0)
            w_start = pl.multiple_of(w_blk * CHUNK, CHUNK)
            s_w = _dot(kw_ref[g, pl.ds(w_start, n_w), :], qt) + wmask_ref[j - w_blk]
            p_w = jnp.exp(s_w - jnp.max(s_w, axis=0, keepdims=True)).astype(BF)
            o_w = _dot(vwt_ref[g, w_blk], p_w[0:CHUNK])
            for sub in range(1, n_w // CHUNK):
                o_w = o_w + _dot(vwt_ref[g, w_blk + sub], p_w[sub * CHUNK:(sub + 1) * CHUNK])
            part_ref[g, :, lanes] = (gate_row(g, qb, 0) * o_c
                                     + gate_row(g, qb, 2) * (o_w[0:HEAD_DIM] / o_w[HEAD_DIM:HEAD_DIM + 1]))

    m_ref[...] = jnp.full(m_ref.shape, NEG, F32)
    acc_ref[...] = jnp.zeros(acc_ref.shape, F32)

    def scores(g, key_step):
        start = pl.multiple_of(key_step * SEL_KEYS, SEL_KEYS)
        return _dot(ks_ref[g, pl.ds(start, SEL_KEYS), :], qs_ref[g])

    def absorb(g, s, key_step):
        m_old = m_ref[g]
        m_new = jnp.maximum(m_old, jnp.max(s, axis=0, keepdims=True))
        p = jnp.exp(s - m_new).astype(BF)
        acc_ref[g] = jnp.exp(m_old - m_new) * acc_ref[g] + _dot(vst_ref[g, key_step], p)
        m_ref[g] = m_new

    for g in groups:
        s_ref[g] = scores(g, 0)

    def loop_body(i, carry):
        for g in groups:
            s = s_ref[g]
            s_ref[g] = scores(g, i + 1)
            absorb(g, s, i)
        return carry

    lax.fori_loop(0, step, loop_body, 0)
    for g in groups:
        absorb(g, s_ref[g] + dmask_ref[...], step)
        acc = acc_ref[g]
        o_s = acc[0:HEAD_DIM] / acc[HEAD_DIM:HEAD_DIM + 1]
        gw = NSA_REP * HEAD_DIM
        for qb in q_blocks:
            lanes = slice(qb * nq, (qb + 1) * nq)
            out_t = part_ref[g, :, lanes] + gate_row(g, qb, 1) * o_s[:, lanes]
            by_head = jnp.concatenate([out_t[:, r * CHUNK:(r + 1) * CHUNK] for r in range(NSA_REP)], axis=0)
            o_ref[qb * CHUNK:(qb + 1) * CHUNK, g * gw:(g + 1) * gw] = by_head.T.astype(BF)


def _nsa_attn(qt_aug, gates_t, kc, vc, ks_aug, vs_t, kw_aug, vw_t,
              w1k, w1kf, w2k, pek, w1v, w1vf, w2v, pev, k_gain, cpos, ov_t, win_mask, diag_mask, cmp_mask, layer):
    b, g, nblk, _, nq = qt_aug.shape
    lanes = ATTN_BLOCKS * nq
    blk = lambda a: pl.BlockSpec((None, g, ATTN_BLOCKS) + a.shape[3:], lambda i, j: (i, 0, j, 0, 0))
    seq = lambda a: pl.BlockSpec((None,) + a.shape[1:], lambda i, j: (i,) + (0,) * (a.ndim - 1))
    lay = lambda a: pl.BlockSpec((None,) + a.shape[1:], lambda i, j: (layer,) + (0,) * (a.ndim - 1))
    const = lambda a: pl.BlockSpec(a.shape, lambda i, j: (0,) * a.ndim)
    return pl.pallas_call(
        _nsa_attn_body,
        out_shape=jax.ShapeDtypeStruct((b, nblk * CHUNK, g * NSA_REP * HEAD_DIM), BF),
        grid=(b, nblk // ATTN_BLOCKS),
        in_specs=[blk(qt_aug), blk(gates_t), seq(kc), seq(vc),
                  seq(ks_aug), seq(vs_t), seq(kw_aug), seq(vw_t),
                  lay(w1k), lay(w1kf), lay(w2k), lay(pek),
                  lay(w1v), lay(w1vf), lay(w2v), lay(pev), lay(k_gain),
                  const(cpos), const(ov_t), const(win_mask), const(diag_mask), const(cmp_mask)],
        out_specs=pl.BlockSpec((None, ATTN_BLOCKS * CHUNK, g * NSA_REP * HEAD_DIM), lambda i, j: (i, j, 0)),
        scratch_shapes=[pltpu.VMEM((g, LANES, lanes), BF),
                        pltpu.VMEM((g, CHUNK, LANES), BF),
                        pltpu.VMEM((g, V_ROWS, CHUNK), BF),
                        pltpu.VMEM((g, 1, lanes), F32),
                        pltpu.VMEM((g, V_ROWS, lanes), F32),
                        pltpu.VMEM((g, HEAD_DIM, lanes), F32),
                        pltpu.VMEM((g, SEL_KEYS, lanes), F32)],
        compiler_params=_params(2),
        name="nsa_attn",
    )(qt_aug, gates_t, kc, vc, ks_aug, vs_t, kw_aug, vw_t,
      w1k, w1kf, w2k, pek, w1v, w1vf, w2v, pev, k_gain, cpos, ov_t, win_mask, diag_mask, cmp_mask)


def _merge_body(x_ref, ya_ref, yb_ref, yc_ref, yd_ref, g_ref, wg_ref, wb_ref, wo_ref, o_ref):
    x = x_ref[...]
    hn = _rms(x, g_ref[...]).astype(BF)
    merged = jnp.zeros_like(x)
    for i, y_ref in enumerate((ya_ref, yb_ref, yc_ref, yd_ref)):
        gate = jax.nn.sigmoid(_dot(hn, wg_ref[:, i * D_MODEL:(i + 1) * D_MODEL]))
        merged = merged + gate * _dot(y_ref[...], wb_ref[i])
    o_ref[...] = x + _dot(merged.astype(BF), wo_ref[...])


def _merge(x2, ya, yb, yc, yd, norm, w_gate, w_branch, w_out, layer):
    t = x2.shape[0]
    tile = lambda n: pl.BlockSpec((FFN_TILE, n), lambda i: (i, 0))
    return pl.pallas_call(
        _merge_body,
        out_shape=jax.ShapeDtypeStruct(x2.shape, F32),
        grid=(t // FFN_TILE,),
        in_specs=[tile(D_MODEL)] + [tile(WIDTH)] * 4
        + [_layer_spec(a.shape, layer) for a in (norm, w_gate, w_branch, w_out)],
        out_specs=tile(D_MODEL),
        compiler_params=_params(1),
        name="merge",
    )(x2, ya, yb, yc, yd, norm, w_gate, w_branch, w_out)


def _alibi_slopes():
    return np.array([2.0 ** (-8.0 * (i + 1) / NSA_HEADS) for i in range(NSA_HEADS)], np.float32)


def _nsa_constants(seq):
    n_sel = seq // SEL_BLOCK
    n_cmp = (seq - CMP_BLOCK) // CMP_STRIDE + 1
    assert n_cmp == CHUNK - 1 and n_sel <= SEL_BLOCK // 2
    pos_cols = LANES - HEAD_DIM - n_sel
    slopes = _alibi_slopes().reshape(NSA_GROUPS, NSA_REP)
    q_const = np.zeros((NSA_GROUPS, LANES - HEAD_DIM, NSA_REP * CHUNK), np.float32)
    lane_slope = np.repeat(slopes, CHUNK, axis=1)
    q_const[:, n_sel, :] = lane_slope * SEL_BLOCK
    q_const[:, n_sel + 1, :] = lane_slope
    pos = np.arange(seq)
    pos_part = np.zeros((seq, pos_cols), np.float32)
    pos_part[:, 0] = pos // SEL_BLOCK
    pos_part[:, 1] = pos % SEL_BLOCK
    onehot = (pos[:, None] // SEL_BLOCK == np.arange(n_sel)[None, :]).astype(np.float32)

    def key_const(first):
        half = np.concatenate([first, pos_part], axis=1)
        return np.stack([np.concatenate([np.zeros_like(half), half][::1 if g == 0 else -1], axis=1)
                         for g in range(NSA_GROUPS)])

    k_sel_const = key_const(onehot)
    k_win_const = key_const(np.zeros_like(onehot))
    v_const = np.zeros((V_ROWS - HEAD_DIM, CHUNK), np.float32)
    v_const[0, :] = 1.0
    c_mid = np.arange(CHUNK) * CMP_STRIDE + (CMP_BLOCK - 1) / 2.0
    cmp_const = np.zeros((CHUNK, LANES - HEAD_DIM), np.float32)
    cmp_const[:, n_sel] = c_mid // SEL_BLOCK
    cmp_const[:, n_sel + 1] = c_mid % SEL_BLOCK
    c_start = np.arange(CHUNK) * CMP_STRIDE
    s_start = np.arange(n_sel) * SEL_BLOCK
    overlap_t = ((c_start[None, :] <= s_start[:, None] + SEL_BLOCK - 1)
                 & (c_start[None, :] + CMP_BLOCK - 1 >= s_start[:, None])
                 & (np.arange(CHUNK)[None, :] < n_cmp)).astype(np.float32)
    qi = np.arange(NSA_REP * CHUNK)[None, :] % CHUNK

    def mask(rows, n_cases, valid):
        ki = np.arange(rows)[:, None]
        return np.stack([np.where(valid(d * CHUNK + qi - ki), 0.0, NEG) for d in range(n_cases)]).astype(np.float32)

    win_mask = mask(WINDOW + CHUNK, WINDOW // CHUNK + 1, lambda rel: (rel >= 0) & (rel < WINDOW))
    diag_mask = np.concatenate(list(mask(SEL_KEYS, ATTN_BLOCKS, lambda rel: rel >= 0)), axis=1)
    c_last = (np.arange(CHUNK) * CMP_STRIDE + CMP_BLOCK - 1)[None, :, None]
    t_query = (np.arange(seq // CHUNK) * CHUNK)[:, None, None] + qi[None]
    cmp_mask = np.where((c_last <= t_query) & (np.arange(CHUNK) < n_cmp)[None, :, None], 0.0, NEG).astype(np.float32)
    bf = lambda a: jnp.asarray(a, BF)
    return dict(q_const=bf(q_const), k_sel=bf(k_sel_const), k_win=bf(k_win_const), v_const=bf(v_const),
                cmp_const=bf(cmp_const), overlap_t=bf(overlap_t),
                win_mask=jnp.asarray(win_mask), diag_mask=jnp.asarray(diag_mask), cmp_mask=jnp.asarray(cmp_mask))


def _block_diag_ones(n):
    idx = np.arange(n) // HEAD_DIM
    return jnp.asarray(idx[:, None] == idx[None, :], BF)


def _compress_weights(w1):
    nl = w1.shape[0]
    per_pos = w1.reshape(nl, 1, CMP_BLOCK, 1, HEAD_DIM, CMP_HIDDEN)
    group_sel = jnp.eye(NSA_GROUPS, dtype=w1.dtype)
    big = per_pos * group_sel[None, :, None, :, None, None]
    return big.reshape(nl, NSA_GROUPS, CMP_BLOCK, KV_WIDTH, CMP_HIDDEN).astype(BF)


def _nsa_proj_weights(w_in):
    depth = w_in.shape[0]
    gate = w_in[:, :, OFF_GATE:OFF_AB].reshape(depth, D_MODEL, NSA_GROUPS, NSA_REP, 3)
    gate = jnp.swapaxes(gate, 3, 4).reshape(depth, D_MODEL, NSA_GROUPS, 3 * NSA_REP)
    gate = jnp.pad(gate, ((0, 0), (0, 0), (0, 0), (0, GATE_ROWS - 3 * NSA_REP)))
    gate = gate.reshape(depth, D_MODEL, NSA_GROUPS * GATE_ROWS)
    gate = jnp.pad(gate, ((0, 0), (0, 0), (0, LANES - NSA_GROUPS * GATE_ROWS)))
    return jnp.concatenate([w_in[:, :, OFF_Q:OFF_GATE], gate], axis=-1).astype(BF)


def _pe_rows(pe):
    nl = pe.shape[0]
    flat = pe.reshape(nl, 1, CMP_BLOCK * HEAD_DIM)
    return jnp.concatenate([flat, jnp.zeros((nl, 7, CMP_BLOCK * HEAD_DIM), pe.dtype)], axis=1).astype(BF)


def _nsa_params(seq, q_norm, k_norm, pe_k, w1_k, w2_k, pe_v, w1_v, w2_v):
    row = lambda a: a[:, None, :]
    p = _nsa_constants(seq)
    p.update(q_gain=row(jnp.tile(q_norm, (1, NSA_HEADS))), k_gain2=row(jnp.tile(k_norm, (1, NSA_GROUPS))),
             k_gain=row(k_norm), w1k=_compress_weights(w1_k), w1v=_compress_weights(w1_v),
             w1kf=w1_k.astype(BF), w1vf=w1_v.astype(BF), w2k=w2_k.astype(BF), w2v=w2_v.astype(BF),
             pek=_pe_rows(pe_k), pev=_pe_rows(pe_v),
             bd_q=_block_diag_ones(WIDTH), bd_k=_block_diag_ones(KV_WIDTH))
    return p


def _token_mixers(x3, mxn, sgu_p, ssd_p, conv_p, w_c, p, layer):
    y_a, y_b, y_d, *attn_in = _mixers(x3, mxn, sgu_p, ssd_p, conv_p, w_c, p, layer)
    y_c = _nsa_attn(*attn_in,
                    p["w1k"], p["w1kf"], p["w2k"], p["pek"], p["w1v"], p["w1vf"], p["w2v"], p["pev"], p["k_gain"],
                    p["cmp_const"], p["overlap_t"], p["win_mask"], p["diag_mask"], p["cmp_mask"], layer)
    return y_a, y_b, y_c, y_d


def kernel(x, ffn1_norm, ffn1_w_in, ffn1_w_out, mix_norm, w_in, sgu_v_norm, sgu_w, sgu_b, ssm_conv_w, ssm_conv_b, ssm_dt_bias, ssm_a_log, ssm_d, ssm_norm, nsa_q_norm, nsa_k_norm, nsa_pe_k, nsa_w1_k, nsa_w2_k, nsa_pe_v, nsa_w1_v, nsa_w2_v, conv_dw_w, conv_dw_b, conv_norm, w_branch, w_out, ffn2_norm, ffn2_w_in, ffn2_w_out):
    bsz, seq, _ = x.shape
    depth = w_in.shape[0]
    row = lambda a: a[:, None, :]
    rep = lambda a, n: jnp.repeat(a, n, axis=-1)

    f1n, f2n, mxn = row(ffn1_norm), row(ffn2_norm), row(mix_norm)
    w_a = w_in[:, :, OFF_UV:OFF_Z].astype(BF)
    w_b = jnp.concatenate([w_in[:, :, OFF_Z:OFF_DT], rep(w_in[:, :, OFF_DT:OFF_Q], SSM_HEAD_DIM)], axis=-1).astype(BF)
    w_c = _nsa_proj_weights(w_in)
    w_d = w_in[:, :, OFF_AB:OFF_MERGE].astype(BF)
    w_g = w_in[:, :, OFF_MERGE:].astype(BF)
    sgu_bias = rep(jnp.swapaxes(sgu_b, 1, 2), WIDTH // SGU_GROUPS)
    dt_bias = row(rep(ssm_dt_bias, SSM_HEAD_DIM))
    a_neg = row(rep(-jnp.exp(ssm_a_log), SSM_HEAD_DIM))
    d_skip = row(rep(ssm_d, SSM_HEAD_DIM))
    wb, wo = w_branch.astype(BF), w_out.astype(BF)
    nsa_p = _nsa_params(seq, nsa_q_norm, nsa_k_norm, nsa_pe_k, nsa_w1_k, nsa_w2_k, nsa_pe_v, nsa_w1_v, nsa_w2_v)
    sgu_p = (w_a, row(sgu_v_norm), sgu_w, sgu_bias)
    ssd_p = (w_b, ssm_conv_w, row(ssm_conv_b), dt_bias, a_neg, d_skip, row(ssm_norm))
    conv_p = (w_d, conv_dw_w, row(conv_dw_b), row(conv_norm))

    x2 = x.reshape(bsz * seq, D_MODEL)
    for l in range(depth):
        x2 = _ffn(x2, f1n, ffn1_w_in, ffn1_w_out, l)
        x3 = x2.reshape(bsz, seq, D_MODEL)
        y_a, y_b, y_c, y_d = _token_mixers(x3, mxn, sgu_p, ssd_p, conv_p, w_c, nsa_p, l)
        flat = lambda y: y.reshape(bsz * seq, WIDTH)
        x2 = _merge(x2, flat(y_a), flat(y_b), flat(y_c), flat(y_d), mxn, w_g, wb, wo, l)
        x2 = _ffn(x2, f2n, ffn2_w_in, ffn2_w_out, l)
    return x2.reshape(bsz, seq, D_MODEL)
```

```python
import math

import numpy as np
import jax
import jax.numpy as jnp
from jax import lax
from jax.experimental import pallas as pl
from jax.experimental.pallas import tpu as pltpu

F32 = jnp.float32
BF = jnp.bfloat16

D_MODEL = 1024
FFN_DIM = 2816
WIDTH = 512
SGU_GROUPS = 4
CHUNK = 128
SSM_HEADS = 8
SSM_HEAD_DIM = 64
SSM_GROUPS = 2
SSM_STATE = 128
SSM_CONV = 4
SSM_CONV_DIM = WIDTH + 2 * SSM_GROUPS * SSM_STATE
SSM_REP = SSM_HEADS // SSM_GROUPS
NSA_HEADS = 8
NSA_GROUPS = 2
NSA_REP = NSA_HEADS // NSA_GROUPS
HEAD_DIM = 64
KV_WIDTH = NSA_GROUPS * HEAD_DIM
CMP_BLOCK = 32
CMP_STRIDE = 16
CMP_HIDDEN = 128
SEL_BLOCK = 64
SEL_TOP_N = 8
WINDOW = 256
CONV_KERNEL = 31
EPS = 1e-6
NEG = -1e30
FORCE = 1e9

OFF_UV = 0
OFF_Z = OFF_UV + 2 * WIDTH
OFF_XBC = OFF_Z + WIDTH
OFF_DT = OFF_XBC + SSM_CONV_DIM
OFF_Q = OFF_DT + SSM_HEADS
OFF_KV = OFF_Q + WIDTH
OFF_GATE = OFF_KV + 6 * KV_WIDTH
OFF_AB = OFF_GATE + 3 * NSA_HEADS
OFF_MERGE = OFF_AB + 2 * WIDTH

LANES = 128
SUBLANES = 8
MXU_DIM = 256
VMEM_LIMIT_BYTES = 56 * 1024 * 1024

TOKEN_TILE = 256
FFN_TILE = 512
CONV_HIST = 32
SSM_HIST = 8
SEL_KEYS = 256
ATTN_BLOCKS = SEL_KEYS // CHUNK
V_ROWS = 80
GATE_ROWS = 16


def _params(n_axes):
    return pltpu.CompilerParams(dimension_semantics=("arbitrary",) * n_axes,
                                vmem_limit_bytes=VMEM_LIMIT_BYTES)


def _rms(x, g):
    return x * lax.rsqrt(jnp.mean(x * x, axis=-1, keepdims=True) + EPS) * g


def _silu(x):
    return x * jax.nn.sigmoid(x)


def _gelu(x):
    return 0.5 * x * (1.0 + jnp.tanh(math.sqrt(2.0 / math.pi) * (x + 0.044715 * (x * x * x))))


def _dot(a, b):
    return jnp.dot(a, b, preferred_element_type=F32)


def _dot_nt(a, b):
    return lax.dot_general(a, b, (((1,), (1,)), ((), ())), preferred_element_type=F32)


def _dot_tn(a, b):
    return lax.dot_general(a, b, (((0,), (0,)), ((), ())), preferred_element_type=F32)


def _split3(x):
    hi = x.astype(BF)
    r = x - hi.astype(F32)
    mid = r.astype(BF)
    lo = (r - mid.astype(F32)).astype(BF)
    return hi, mid, lo


def _split2(x):
    hi = x.astype(BF)
    lo = (x - hi.astype(F32)).astype(BF)
    return hi, lo


def _layer_spec(shape, layer):
    nd = len(shape)
    return pl.BlockSpec((None,) + tuple(shape[1:]), lambda *_: (layer,) + (0,) * (nd - 1))


def _ffn_chunks():
    tiles = FFN_DIM // MXU_DIM
    assert tiles * MXU_DIM == FFN_DIM
    cut = (tiles // 2) * MXU_DIM
    return ((0, cut), (cut, FFN_DIM))


def _ffn_body(x_ref, g_ref, win_ref, wout_ref, o_ref):
    x = x_ref[...]
    hn = _rms(x, g_ref[...]).astype(win_ref.dtype)
    acc = jnp.zeros_like(x)
    for lo, hi in _ffn_chunks():
        gate = _dot(hn, win_ref[:, lo:hi])
        up = _dot(hn, win_ref[:, FFN_DIM + lo:FFN_DIM + hi])
        a = (_silu(gate) * up).astype(wout_ref.dtype)
        acc = acc + _dot(a, wout_ref[lo:hi, :])
    o_ref[...] = x + 0.5 * acc


def _ffn(x2, norm, w_in, w_out, layer):
    t = x2.shape[0]
    return pl.pallas_call(
        _ffn_body,
        out_shape=jax.ShapeDtypeStruct(x2.shape, F32),
        grid=(t // FFN_TILE,),
        in_specs=[pl.BlockSpec((FFN_TILE, D_MODEL), lambda i: (i, 0)),
                  _layer_spec(norm.shape, layer),
                  _layer_spec(w_in.shape, layer),
                  _layer_spec(w_out.shape, layer)],
        out_specs=pl.BlockSpec((FFN_TILE, D_MODEL), lambda i: (i, 0)),
        compiler_params=_params(1),
        name="ffn",
    )(x2, norm, w_in, w_out)


def _sgu_part(hn, w_ref, vn_ref, ws_ref, bs_ref, o_ref):
    uv = _gelu(_dot(hn, w_ref[...]))
    u = uv[:, :WIDTH]
    v = _rms(uv[:, WIDTH:], vn_ref[...]).astype(BF)
    row = lax.broadcasted_iota(jnp.int32, (CHUNK, CHUNK), 0)
    col = lax.broadcasted_iota(jnp.int32, (CHUNK, CHUNK), 1)
    gw = WIDTH // SGU_GROUPS
    ws = [jnp.where(row >= col, ws_ref[g], 0.0).astype(BF) for g in range(SGU_GROUPS)]
    bias = bs_ref[...]
    for c in range(TOKEN_TILE // CHUNK):
        rows = slice(c * CHUNK, (c + 1) * CHUNK)
        mixed = jnp.concatenate(
            [_dot(ws[g], v[rows, g * gw:(g + 1) * gw]) for g in range(SGU_GROUPS)], axis=1)
        o_ref[rows, :] = (u[rows, :] * (mixed + bias)).astype(BF)


def _conv_part(hn, w_ref, dw_ref, db_ref, cn_ref, o_ref, buf_ref, shift_ref):
    ab = _dot(hn, w_ref[...])
    buf_ref[CONV_HIST:CONV_HIST + TOKEN_TILE, :] = ab[:, :WIDTH] * jax.nn.sigmoid(ab[:, WIDTH:])
    acc = jnp.zeros((TOKEN_TILE, WIDTH), F32) + db_ref[...]
    first = CONV_HIST - (CONV_KERNEL - 1)
    for res in range(min(SUBLANES, CONV_KERNEL)):
        n_taps = (CONV_KERNEL - 1 - res) // SUBLANES + 1
        rows = TOKEN_TILE + (n_taps - 1) * SUBLANES
        shift_ref[0:rows, :] = buf_ref[first + res:first + res + rows, :]
        for a in range(n_taps):
            k = a * SUBLANES + res
            acc = acc + dw_ref[k:k + 1, :] * shift_ref[a * SUBLANES:a * SUBLANES + TOKEN_TILE, :]
    o_ref[...] = _silu(_rms(acc, cn_ref[...])).astype(BF)
    buf_ref[0:CONV_HIST, :] = buf_ref[TOKEN_TILE:TOKEN_TILE + CONV_HIST, :]


def _ssd_part(hn, w_ref, cw_ref, cb_ref, dtb_ref, a_ref, dsk_ref, ng_ref, o_ref, buf_ref, st_ref):
    hw = SSM_REP * SSM_HEAD_DIM
    proj = _dot(hn, w_ref[...])
    z = proj[:, :WIDTH]
    buf_ref[SSM_HIST:SSM_HIST + TOKEN_TILE, :] = proj[:, WIDTH:WIDTH + SSM_CONV_DIM]
    dt = jax.nn.softplus(proj[:, WIDTH + SSM_CONV_DIM:] + dtb_ref[...])
    xbc = jnp.zeros((TOKEN_TILE, SSM_CONV_DIM), F32) + cb_ref[...]
    first = SSM_HIST - (SSM_CONV - 1)
    for k in range(SSM_CONV):
        xbc = xbc + cw_ref[k:k + 1, :] * buf_ref[first + k:first + k + TOKEN_TILE, :]
    buf_ref[0:SSM_HIST, :] = buf_ref[TOKEN_TILE:TOKEN_TILE + SSM_HIST, :]
    xbc = _silu(xbc)
    xs = xbc[:, :WIDTH]
    a_all = dt * a_ref[...]
    xdt = xs * dt

    row = lax.broadcasted_iota(jnp.int32, (CHUNK, CHUNK), 0)
    col = lax.broadcasted_iota(jnp.int32, (CHUNK, CHUNK), 1)
    causal = row >= col
    tril = jnp.where(causal, 1.0, 0.0).astype(BF)
    lane = lax.broadcasted_iota(jnp.int32, (CHUNK, 2 * SSM_HEAD_DIM), 1)
    low_half = lane < SSM_HEAD_DIM

    for c in range(TOKEN_TILE // CHUNK):
        rows = slice(c * CHUNK, (c + 1) * CHUNK)
        hi, mid, lo = _split3(a_all[rows, :])
        cs = _dot(tril, hi) + _dot(tril, mid) + _dot(tril, lo)
        cs_t = cs.T
        cs_last = cs[CHUNK - 1:CHUNK, :]
        xc = xdt[rows, :]
        x_decay = (xc * jnp.exp(cs_last - cs)).astype(BF)
        xc_bf = xc.astype(BF)
        grow = jnp.exp(cs)
        y_parts = []
        for g in range(SSM_GROUPS):
            bg = xbc[rows, WIDTH + g * SSM_STATE:WIDTH + (g + 1) * SSM_STATE].astype(BF)
            cg = xbc[rows, WIDTH + (SSM_GROUPS + g) * SSM_STATE:
                     WIDTH + (SSM_GROUPS + g + 1) * SSM_STATE].astype(BF)
            cb = _dot_nt(cg, bg)
            gl = slice(g * hw, (g + 1) * hw)
            y_off = _dot(cg, st_ref[g].astype(BF)) * grow[:, gl]
            y_diag = []
            for pair in range(SSM_REP // 2):
                scores = []
                for hh in range(2):
                    hc = g * hw + (2 * pair + hh) * SSM_HEAD_DIM
                    seg = cs[:, hc:hc + 1] - cs_t[hc:hc + 1, :]
                    decay = jnp.exp(jnp.where(causal, seg, NEG))
                    scores.append((cb * decay).astype(BF))
                xp = xc_bf[:, g * hw + pair * 2 * SSM_HEAD_DIM:g * hw + (pair + 1) * 2 * SSM_HEAD_DIM]
                zero = jnp.zeros_like(xp)
                rhs = jnp.concatenate([jnp.where(low_half, xp, zero),
                                       jnp.where(low_half, zero, xp)], axis=0)
                y_diag.append(_dot(jnp.concatenate(scores, axis=1), rhs))
            y_parts.append(jnp.concatenate(y_diag, axis=1) + y_off)
            st_ref[g] = jnp.exp(cs_last[:, gl]) * st_ref[g] + _dot_tn(bg, x_decay[:, gl])
        y = jnp.concatenate(y_parts, axis=1) + xs[rows, :] * dsk_ref[...]
        y = y * _silu(z[rows, :])
        o_ref[rows, :] = _rms(y, ng_ref[...]).astype(BF)


def _head_rms(x, ones_blockdiag, gain, scale):
    hi, lo = _split2(x * x)
    ss = _dot(hi, ones_blockdiag) + _dot(lo, ones_blockdiag)
    return x * lax.rsqrt(ss * (1.0 / HEAD_DIM) + EPS) * (gain * scale)


def _nsa_proj_part(hn, w_ref, qn_ref, kn_ref, bdq_ref, bdk_ref, qc_ref, ksc_ref, kwc_ref, vcn_ref,
                   qt_ref, gate_ref, kc_ref, vc_ref, ks_ref, vst_ref, kw_ref, vwt_ref):
    proj = _dot(hn, w_ref[...])
    o = WIDTH
    kc_ref[...] = proj[:, o:o + KV_WIDTH]
    vc_ref[...] = proj[:, o + KV_WIDTH:o + 2 * KV_WIDTH]
    q = _head_rms(proj[:, :WIDTH], bdq_ref[...], qn_ref[...], HEAD_DIM ** -0.5)
    ks = _head_rms(proj[:, o + 2 * KV_WIDTH:o + 3 * KV_WIDTH], bdk_ref[...], kn_ref[...], 1.0)
    kw = _head_rms(proj[:, o + 4 * KV_WIDTH:o + 5 * KV_WIDTH], bdk_ref[...], kn_ref[...], 1.0)
    vs = proj[:, o + 3 * KV_WIDTH:o + 4 * KV_WIDTH]
    vw = proj[:, o + 5 * KV_WIDTH:o + 6 * KV_WIDTH]
    gate_t = jax.nn.sigmoid(proj[:, o + 6 * KV_WIDTH:]).T
    lane = lax.broadcasted_iota(jnp.int32, (TOKEN_TILE, KV_WIDTH), 1)
    gw = NSA_REP * HEAD_DIM
    for g in range(NSA_GROUPS):
        own = (lane >= g * HEAD_DIM) & (lane < (g + 1) * HEAD_DIM)
        ks_ref[g] = (jnp.where(own, ks, 0.0) + ksc_ref[g]).astype(BF)
        kw_ref[g] = (jnp.where(own, kw, 0.0) + kwc_ref[g]).astype(BF)
    vs_t = vs.T.astype(BF)
    for g in range(NSA_GROUPS):
        vst_ref[g, 0, 0:HEAD_DIM, :] = vs_t[g * HEAD_DIM:(g + 1) * HEAD_DIM]
        vst_ref[g, 0, HEAD_DIM:V_ROWS, :] = jnp.concatenate([vcn_ref[...]] * (SEL_KEYS // CHUNK), axis=1)
    for c in range(TOKEN_TILE // CHUNK):
        rows = slice(c * CHUNK, (c + 1) * CHUNK)
        vw_t = vw[rows, :].T.astype(BF)
        for g in range(NSA_GROUPS):
            feat = slice(g * HEAD_DIM, (g + 1) * HEAD_DIM)
            rest = slice((1 - g) * HEAD_DIM, (2 - g) * HEAD_DIM)
            q_t = q[rows, g * gw:(g + 1) * gw].T
            qt_ref[g, c, feat, :] = jnp.concatenate(
                [q_t[r * HEAD_DIM:(r + 1) * HEAD_DIM] for r in range(NSA_REP)], axis=1).astype(BF)
            qt_ref[g, c, rest, :] = qc_ref[g]
            gate_ref[g, c] = gate_t[g * GATE_ROWS:(g + 1) * GATE_ROWS, rows]
            vwt_ref[g, c, 0:HEAD_DIM, :] = vw_t[feat]
            vwt_ref[g, c, HEAD_DIM:V_ROWS, :] = vcn_ref[...]


N_SGU_IN, N_SSD_IN, N_CONV_IN, N_NSA_IN = 4, 7, 4, 9


def _mixers_body(x_ref, g_ref, *refs):
    n_in = N_SGU_IN + N_SSD_IN + N_CONV_IN + N_NSA_IN
    ins, outs, scratch = refs[:n_in], refs[n_in:n_in + 11], refs[n_in + 11:]
    sgu_in, ins = ins[:N_SGU_IN], ins[N_SGU_IN:]
    ssd_in, ins = ins[:N_SSD_IN], ins[N_SSD_IN:]
    conv_in, nsa_in = ins[:N_CONV_IN], ins[N_CONV_IN:]
    ya_ref, yb_ref, yd_ref = outs[:3]
    conv_buf, conv_shift, ssd_buf, ssd_state = scratch

    @pl.when(pl.program_id(1) == 0)
    def _():
        conv_buf[0:CONV_HIST, :] = jnp.zeros((CONV_HIST, WIDTH), F32)
        ssd_buf[0:SSM_HIST, :] = jnp.zeros((SSM_HIST, SSM_CONV_DIM), F32)
        ssd_state[...] = jnp.zeros(ssd_state.shape, F32)

    hn = _rms(x_ref[...], g_ref[...]).astype(BF)
    _nsa_proj_part(hn, *nsa_in, *outs[3:])
    _conv_part(hn, *conv_in, yd_ref, conv_buf, conv_shift)
    _sgu_part(hn, *sgu_in, ya_ref)
    _ssd_part(hn, *ssd_in, yb_ref, ssd_buf, ssd_state)


def _mixers(x3, norm, sgu_p, ssd_p, conv_p, w_c, p, layer):
    b, s, _ = x3.shape
    g, nq, blocks = NSA_GROUPS, NSA_REP * CHUNK, TOKEN_TILE // CHUNK
    n_blk = s // CHUNK
    tile = lambda n: pl.BlockSpec((None, TOKEN_TILE, n), lambda i, j: (i, j, 0))
    const = lambda a: pl.BlockSpec(a.shape, lambda i, j: (0,) * a.ndim)
    lay = lambda a: _layer_spec(a.shape, layer)
    key_const = lambda a: pl.BlockSpec((g, TOKEN_TILE, LANES), lambda i, j: (0, j, 0))
    per_blk = lambda rows, lanes: pl.BlockSpec((None, g, blocks, rows, lanes), lambda i, j: (i, 0, j, 0, 0))
    keys = pl.BlockSpec((None, g, TOKEN_TILE, LANES), lambda i, j: (i, 0, j, 0))
    y = jax.ShapeDtypeStruct((b, s, WIDTH), BF)
    raw = jax.ShapeDtypeStruct((b, s, KV_WIDTH), F32)
    k_aug = jax.ShapeDtypeStruct((b, g, s, LANES), BF)
    v_t = jax.ShapeDtypeStruct((b, g, n_blk, V_ROWS, CHUNK), BF)
    assert TOKEN_TILE == SEL_KEYS
    v_sel = jax.ShapeDtypeStruct((b, g, s // SEL_KEYS, V_ROWS, SEL_KEYS), BF)
    v_sel_spec = pl.BlockSpec((None, g, 1, V_ROWS, SEL_KEYS), lambda i, j: (i, 0, j, 0, 0))
    assert (len(sgu_p), len(ssd_p), len(conv_p)) == (N_SGU_IN, N_SSD_IN, N_CONV_IN)
    nsa_layer = (w_c, p["q_gain"], p["k_gain2"])
    nsa_const = (p["bd_q"], p["bd_k"], p["q_const"])
    layered = tuple(sgu_p) + tuple(ssd_p) + tuple(conv_p) + nsa_layer
    return pl.pallas_call(
        _mixers_body,
        out_shape=(y, y, y,
                   jax.ShapeDtypeStruct((b, g, n_blk, LANES, nq), BF),
                   jax.ShapeDtypeStruct((b, g, n_blk, GATE_ROWS, CHUNK), F32),
                   raw, raw, k_aug, v_sel, k_aug, v_t),
        grid=(b, s // TOKEN_TILE),
        in_specs=[tile(D_MODEL), lay(norm)] + [lay(a) for a in layered] + [const(a) for a in nsa_const]
        + [key_const(p["k_sel"]), key_const(p["k_win"]), const(p["v_const"])],
        out_specs=(tile(WIDTH),) * 3
        + (per_blk(LANES, nq), per_blk(GATE_ROWS, CHUNK), tile(KV_WIDTH), tile(KV_WIDTH),
           keys, v_sel_spec, keys, per_blk(V_ROWS, CHUNK)),
        scratch_shapes=[pltpu.VMEM((TOKEN_TILE + CONV_HIST, WIDTH), F32),
                        pltpu.VMEM((TOKEN_TILE + CONV_HIST, WIDTH), F32),
                        pltpu.VMEM((TOKEN_TILE + SSM_HIST, SSM_CONV_DIM), F32),
                        pltpu.VMEM((SSM_GROUPS, SSM_STATE, SSM_REP * SSM_HEAD_DIM), F32)],
        compiler_params=_params(2),
        name="mixers",
    )(x3, norm, *layered, *nsa_const, p["k_sel"], p["k_win"], p["v_const"])


def _nsa_attn_body(qt_ref, gate_ref, kc_ref, vc_ref, ks_ref, vst_ref, kw_ref, vwt_ref,
                   w1k_ref, w1kf_ref, w2k_ref, pek_ref, w1v_ref, w1vf_ref, w2v_ref, pev_ref, kn_ref,
                   cpos_ref, ovt_ref, wmask_ref, dmask_ref, cmask_ref,
                   o_ref, qs_ref, kcmp_ref, vcmpt_ref, m_ref, acc_ref, part_ref, s_ref):
    step = pl.program_id(1)
    nq = NSA_REP * CHUNK
    n_sel = ovt_ref.shape[0]
    groups = range(NSA_GROUPS)
    q_blocks = range(ATTN_BLOCKS)

    def compress(x_ref, w1_ref, pe_ref, w1f_ref, w2_ref):
        pe_bias = _dot(pe_ref[...], w1f_ref[...])[0:1, :]
        head = [jnp.zeros((CHUNK, CMP_HIDDEN), F32) for _ in groups]
        tail = [jnp.zeros((CHUNK, CMP_HIDDEN), F32) for _ in groups]
        for i in range(CMP_STRIDE):
            x = x_ref[pl.ds(i, CHUNK, stride=CMP_STRIDE), :].astype(BF)
            for g in groups:
                head[g] = head[g] + _dot(x, w1_ref[g, i])
                tail[g] = tail[g] + _dot(x, w1_ref[g, CMP_STRIDE + i])
        return [_dot(_gelu(head[g] + pltpu.roll(tail[g], CHUNK - 1, 0) + pe_bias).astype(BF), w2_ref[...])
                for g in groups]

    @pl.when(step == 0)
    def _():
        k_cmp = compress(kc_ref, w1k_ref, pek_ref, w1kf_ref, w2k_ref)
        v_cmp = compress(vc_ref, w1v_ref, pev_ref, w1vf_ref, w2v_ref)
        lane = lax.broadcasted_iota(jnp.int32, (CHUNK, LANES - HEAD_DIM), 1)
        ones_col = jnp.where(lane == 0, 1.0, 0.0)
        for g in groups:
            halves = [_rms(k_cmp[g], kn_ref[...]).astype(BF), cpos_ref[...]]
            kcmp_ref[g] = jnp.concatenate(halves if g == 0 else halves[::-1], axis=1)
            vcmpt_ref[g] = jnp.concatenate([v_cmp[g], ones_col], axis=1).T[0:V_ROWS, :].astype(BF)

    def select_blocks(p_c, t0):
        p_sum = p_c[:, 0:CHUNK]
        for r in range(1, NSA_REP):
            p_sum = p_sum + p_c[:, r * CHUNK:(r + 1) * CHUNK]
        p_hi, p_lo = _split2(p_sum)
        imp = _dot(ovt_ref[...], p_hi) + _dot(ovt_ref[...], p_lo)
        blk = lax.broadcasted_iota(jnp.int32, (n_sel, CHUNK), 0)
        tq = t0 + lax.broadcasted_iota(jnp.int32, (n_sel, CHUNK), 1)
        cur = tq // SEL_BLOCK
        forced = (blk == 0) | (blk == cur) | (blk == cur - 1)
        score = jnp.where(forced, FORCE, jnp.where(blk * SEL_BLOCK <= tq, imp, NEG))
        blk_f = blk.astype(F32)
        chosen = jnp.zeros((n_sel, CHUNK), F32)
        for _ in range(SEL_TOP_N):
            best = jnp.max(score, axis=0, keepdims=True)
            first = jnp.min(jnp.where(score == best, blk_f, 1e9), axis=0, keepdims=True)
            pick = blk_f == first
            chosen = jnp.where(pick, 1.0, chosen)
            score = jnp.where(pick, -3e38, score)
        return ((chosen - 1.0) * (-NEG)).astype(BF)

    n_w = WINDOW + CHUNK

    def gate_row(g, qb, branch):
        rows = gate_ref[g, qb]
        return jnp.concatenate([rows[branch * NSA_REP + r:branch * NSA_REP + r + 1] for r in range(NSA_REP)], axis=1)

    for g in groups:
        for qb in q_blocks:
            j = step * ATTN_BLOCKS + qb
            t0 = j * CHUNK
            lanes = slice(qb * nq, (qb + 1) * nq)
            qt = qt_ref[g, qb]
            slot = (1 - g) * HEAD_DIM
            cmask = cmask_ref[j]
            s_c = _dot(kcmp_ref[g], qt) + cmask
            e_c = jnp.exp(s_c - jnp.max(s_c, axis=0, keepdims=True))
            sees_any = jnp.where(cmask[0:1, :] == 0.0, 1.0, 0.0)
            p_c = e_c * (sees_any / jnp.sum(e_c, axis=0, keepdims=True))
            o_c = _dot(vcmpt_ref[g], p_c.astype(BF))[0:HEAD_DIM]
            qs_ref[g, :, lanes] = qt
            qs_ref[g, slot:slot + n_sel, lanes] = jnp.concatenate([select_blocks(p_c, t0)] * NSA_REP, axis=1)
            w_blk = jnp.maximum(j - WINDOW // CHUNK, 0)
            w_start = pl.multiple_of(w_blk * CHUNK, CHUNK)
            s_w = _dot(kw_ref[g, pl.ds(w_start, n_w), :], qt) + wmask_ref[j - w_blk]
            p_w = jnp.exp(s_w - jnp.max(s_w, axis=0, keepdims=True)).astype(BF)
            o_w = _dot(vwt_ref[g, w_blk], p_w[0:CHUNK])
            for sub in range(1, n_w // CHUNK):
                o_w = o_w + _dot(vwt_ref[g, w_blk + sub], p_w[sub * CHUNK:(sub + 1) * CHUNK])
            part_ref[g, :, lanes] = (gate_row(g, qb, 0) * o_c
                                     + gate_row(g, qb, 2) * (o_w[0:HEAD_DIM] / o_w[HEAD_DIM:HEAD_DIM + 1]))

    m_ref[...] = jnp.full(m_ref.shape, NEG, F32)
    acc_ref[...] = jnp.zeros(acc_ref.shape, F32)

    def scores(g, key_step):
        start = pl.multiple_of(key_step * SEL_KEYS, SEL_KEYS)
        return _dot(ks_ref[g, pl.ds(start, SEL_KEYS), :], qs_ref[g])

    def absorb(g, s, key_step):
        m_old = m_ref[g]
        m_new = jnp.maximum(m_old, jnp.max(s, axis=0, keepdims=True))
        p = jnp.exp(s - m_new).astype(BF)
        acc_ref[g] = jnp.exp(m_old - m_new) * acc_ref[g] + _dot(vst_ref[g, key_step], p)
        m_ref[g] = m_new

    for g in groups:
        s_ref[g] = scores(g, 0)

    def loop_body(i, carry):
        for g in groups:
            s = s_ref[g]
            s_ref[g] = scores(g, i + 1)
            absorb(g, s, i)
        return carry

    lax.fori_loop(0, step, loop_body, 0)
    for g in groups:
        absorb(g, s_ref[g] + dmask_ref[...], step)
        acc = acc_ref[g]
        o_s = acc[0:HEAD_DIM] / acc[HEAD_DIM:HEAD_DIM + 1]
        gw = NSA_REP * HEAD_DIM
        for qb in q_blocks:
            lanes = slice(qb * nq, (qb + 1) * nq)
            out_t = part_ref[g, :, lanes] + gate_row(g, qb, 1) * o_s[:, lanes]
            by_head = jnp.concatenate([out_t[:, r * CHUNK:(r + 1) * CHUNK] for r in range(NSA_REP)], axis=0)
            o_ref[qb * CHUNK:(qb + 1) * CHUNK, g * gw:(g + 1) * gw] = by_head.T.astype(BF)


def _nsa_attn(qt_aug, gates_t, kc, vc, ks_aug, vs_t, kw_aug, vw_t,
              w1k, w1kf, w2k, pek, w1v, w1vf, w2v, pev, k_gain, cpos, ov_t, win_mask, diag_mask, cmp_mask, layer):
    b, g, nblk, _, nq = qt_aug.shape
    lanes = ATTN_BLOCKS * nq
    blk = lambda a: pl.BlockSpec((None, g, ATTN_BLOCKS) + a.shape[3:], lambda i, j: (i, 0, j, 0, 0))
    seq = lambda a: pl.BlockSpec((None,) + a.shape[1:], lambda i, j: (i,) + (0,) * (a.ndim - 1))
    lay = lambda a: pl.BlockSpec((None,) + a.shape[1:], lambda i, j: (layer,) + (0,) * (a.ndim - 1))
    const = lambda a: pl.BlockSpec(a.shape, lambda i, j: (0,) * a.ndim)
    return pl.pallas_call(
        _nsa_attn_body,
        out_shape=jax.ShapeDtypeStruct((b, nblk * CHUNK, g * NSA_REP * HEAD_DIM), BF),
        grid=(b, nblk // ATTN_BLOCKS),
        in_specs=[blk(qt_aug), blk(gates_t), seq(kc), seq(vc),
                  seq(ks_aug), seq(vs_t), seq(kw_aug), seq(vw_t),
                  lay(w1k), lay(w1kf), lay(w2k), lay(pek),
                  lay(w1v), lay(w1vf), lay(w2v), lay(pev), lay(k_gain),
                  const(cpos), const(ov_t), const(win_mask), const(diag_mask), const(cmp_mask)],
        out_specs=pl.BlockSpec((None, ATTN_BLOCKS * CHUNK, g * NSA_REP * HEAD_DIM), lambda i, j: (i, j, 0)),
        scratch_shapes=[pltpu.VMEM((g, LANES, lanes), BF),
                        pltpu.VMEM((g, CHUNK, LANES), BF),
                        pltpu.VMEM((g, V_ROWS, CHUNK), BF),
                        pltpu.VMEM((g, 1, lanes), F32),
                        pltpu.VMEM((g, V_ROWS, lanes), F32),
                        pltpu.VMEM((g, HEAD_DIM, lanes), F32),
                        pltpu.VMEM((g, SEL_KEYS, lanes), F32)],
        compiler_params=_params(2),
        name="nsa_attn",
    )(qt_aug, gates_t, kc, vc, ks_aug, vs_t, kw_aug, vw_t,
      w1k, w1kf, w2k, pek, w1v, w1vf, w2v, pev, k_gain, cpos, ov_t, win_mask, diag_mask, cmp_mask)


def _merge_body(x_ref, ya_ref, yb_ref, yc_ref, yd_ref, g_ref, wg_ref, wb_ref, wo_ref, o_ref):
    x = x_ref[...]
    hn = _rms(x, g_ref[...]).astype(BF)
    merged = jnp.zeros_like(x)
    for i, y_ref in enumerate((ya_ref, yb_ref, yc_ref, yd_ref)):
        gate = jax.nn.sigmoid(_dot(hn, wg_ref[:, i * D_MODEL:(i + 1) * D_MODEL]))
        merged = merged + gate * _dot(y_ref[...], wb_ref[i])
    o_ref[...] = x + _dot(merged.astype(wo_ref.dtype), wo_ref[...])


def _merge(x2, ya, yb, yc, yd, norm, w_gate, w_branch, w_out, layer):
    t = x2.shape[0]
    tile = lambda n: pl.BlockSpec((FFN_TILE, n), lambda i: (i, 0))
    return pl.pallas_call(
        _merge_body,
        out_shape=jax.ShapeDtypeStruct(x2.shape, F32),
        grid=(t // FFN_TILE,),
        in_specs=[tile(D_MODEL)] + [tile(WIDTH)] * 4
        + [_layer_spec(a.shape, layer) for a in (norm, w_gate, w_branch, w_out)],
        out_specs=tile(D_MODEL),
        compiler_params=_params(1),
        name="merge",
    )(x2, ya, yb, yc, yd, norm, w_gate, w_branch, w_out)


def _alibi_slopes():
    return np.array([2.0 ** (-8.0 * (i + 1) / NSA_HEADS) for i in range(NSA_HEADS)], np.float32)


def _nsa_constants(seq):
    n_sel = seq // SEL_BLOCK
    n_cmp = (seq - CMP_BLOCK) // CMP_STRIDE + 1
    assert n_cmp == CHUNK - 1 and n_sel <= SEL_BLOCK // 2
    pos_cols = LANES - HEAD_DIM - n_sel
    slopes = _alibi_slopes().reshape(NSA_GROUPS, NSA_REP)
    q_const = np.zeros((NSA_GROUPS, LANES - HEAD_DIM, NSA_REP * CHUNK), np.float32)
    lane_slope = np.repeat(slopes, CHUNK, axis=1)
    q_const[:, n_sel, :] = lane_slope * SEL_BLOCK
    q_const[:, n_sel + 1, :] = lane_slope
    pos = np.arange(seq)
    pos_part = np.zeros((seq, pos_cols), np.float32)
    pos_part[:, 0] = pos // SEL_BLOCK
    pos_part[:, 1] = pos % SEL_BLOCK
    onehot = (pos[:, None] // SEL_BLOCK == np.arange(n_sel)[None, :]).astype(np.float32)

    def key_const(first):
        half = np.concatenate([first, pos_part], axis=1)
        return np.stack([np.concatenate([np.zeros_like(half), half][::1 if g == 0 else -1], axis=1)
                         for g in range(NSA_GROUPS)])

    k_sel_const = key_const(onehot)
    k_win_const = key_const(np.zeros_like(onehot))
    v_const = np.zeros((V_ROWS - HEAD_DIM, CHUNK), np.float32)
    v_const[0, :] = 1.0
    c_mid = np.arange(CHUNK) * CMP_STRIDE + (CMP_BLOCK - 1) / 2.0
    cmp_const = np.zeros((CHUNK, LANES - HEAD_DIM), np.float32)
    cmp_const[:, n_sel] = c_mid // SEL_BLOCK
    cmp_const[:, n_sel + 1] = c_mid % SEL_BLOCK
    c_start = np.arange(CHUNK) * CMP_STRIDE
    s_start = np.arange(n_sel) * SEL_BLOCK
    overlap_t = ((c_start[None, :] <= s_start[:, None] + SEL_BLOCK - 1)
                 & (c_start[None, :] + CMP_BLOCK - 1 >= s_start[:, None])
                 & (np.arange(CHUNK)[None, :] < n_cmp)).astype(np.float32)
    qi = np.arange(NSA_REP * CHUNK)[None, :] % CHUNK

    def mask(rows, n_cases, valid):
        ki = np.arange(rows)[:, None]
        return np.stack([np.where(valid(d * CHUNK + qi - ki), 0.0, NEG) for d in range(n_cases)]).astype(np.float32)

    win_mask = mask(WINDOW + CHUNK, WINDOW // CHUNK + 1, lambda rel: (rel >= 0) & (rel < WINDOW))
    diag_mask = np.concatenate(list(mask(SEL_KEYS, ATTN_BLOCKS, lambda rel: rel >= 0)), axis=1)
    c_last = (np.arange(CHUNK) * CMP_STRIDE + CMP_BLOCK - 1)[None, :, None]
    t_query = (np.arange(seq // CHUNK) * CHUNK)[:, None, None] + qi[None]
    cmp_mask = np.where((c_last <= t_query) & (np.arange(CHUNK) < n_cmp)[None, :, None], 0.0, NEG).astype(np.float32)
    bf = lambda a: jnp.asarray(a, BF)
    return dict(q_const=bf(q_const), k_sel=bf(k_sel_const), k_win=bf(k_win_const), v_const=bf(v_const),
                cmp_const=bf(cmp_const), overlap_t=bf(overlap_t),
                win_mask=jnp.asarray(win_mask), diag_mask=jnp.asarray(diag_mask), cmp_mask=jnp.asarray(cmp_mask))


def _block_diag_ones(n):
    idx = np.arange(n) // HEAD_DIM
    return jnp.asarray(idx[:, None] == idx[None, :], BF)


def _compress_weights(w1):
    nl = w1.shape[0]
    per_pos = w1.reshape(nl, 1, CMP_BLOCK, 1, HEAD_DIM, CMP_HIDDEN)
    group_sel = jnp.eye(NSA_GROUPS, dtype=w1.dtype)
    big = per_pos * group_sel[None, :, None, :, None, None]
    return big.reshape(nl, NSA_GROUPS, CMP_BLOCK, KV_WIDTH, CMP_HIDDEN).astype(BF)


def _nsa_proj_weights(w_in):
    depth = w_in.shape[0]
    gate = w_in[:, :, OFF_GATE:OFF_AB].reshape(depth, D_MODEL, NSA_GROUPS, NSA_REP, 3)
    gate = jnp.swapaxes(gate, 3, 4).reshape(depth, D_MODEL, NSA_GROUPS, 3 * NSA_REP)
    gate = jnp.pad(gate, ((0, 0), (0, 0), (0, 0), (0, GATE_ROWS - 3 * NSA_REP)))
    gate = gate.reshape(depth, D_MODEL, NSA_GROUPS * GATE_ROWS)
    gate = jnp.pad(gate, ((0, 0), (0, 0), (0, LANES - NSA_GROUPS * GATE_ROWS)))
    return jnp.concatenate([w_in[:, :, OFF_Q:OFF_GATE], gate], axis=-1).astype(BF)


def _pe_rows(pe):
    nl = pe.shape[0]
    flat = pe.reshape(nl, 1, CMP_BLOCK * HEAD_DIM)
    return jnp.concatenate([flat, jnp.zeros((nl, 7, CMP_BLOCK * HEAD_DIM), pe.dtype)], axis=1).astype(BF)


def _nsa_params(seq, q_norm, k_norm, pe_k, w1_k, w2_k, pe_v, w1_v, w2_v):
    row = lambda a: a[:, None, :]
    p = _nsa_constants(seq)
    p.update(q_gain=row(jnp.tile(q_norm, (1, NSA_HEADS))), k_gain2=row(jnp.tile(k_norm, (1, NSA_GROUPS))),
             k_gain=row(k_norm), w1k=_compress_weights(w1_k), w1v=_compress_weights(w1_v),
             w1kf=w1_k.astype(BF), w1vf=w1_v.astype(BF), w2k=w2_k.astype(BF), w2v=w2_v.astype(BF),
             pek=_pe_rows(pe_k), pev=_pe_rows(pe_v),
             bd_q=_block_diag_ones(WIDTH), bd_k=_block_diag_ones(KV_WIDTH))
    return p


def _token_mixers(x3, mxn, sgu_p, ssd_p, conv_p, w_c, p, layer):
    y_a, y_b, y_d, *attn_in = _mixers(x3, mxn, sgu_p, ssd_p, conv_p, w_c, p, layer)
    y_c = _nsa_attn(*attn_in,
                    p["w1k"], p["w1kf"], p["w2k"], p["pek"], p["w1v"], p["w1vf"], p["w2v"], p["pev"], p["k_gain"],
                    p["cmp_const"], p["overlap_t"], p["win_mask"], p["diag_mask"], p["cmp_mask"], layer)
    return y_a, y_b, y_c, y_d


def kernel(x, ffn1_norm, ffn1_w_in, ffn1_w_out, mix_norm, w_in, sgu_v_norm, sgu_w, sgu_b, ssm_conv_w, ssm_conv_b, ssm_dt_bias, ssm_a_log, ssm_d, ssm_norm, nsa_q_norm, nsa_k_norm, nsa_pe_k, nsa_w1_k, nsa_w2_k, nsa_pe_v, nsa_w1_v, nsa_w2_v, conv_dw_w, conv_dw_b, conv_norm, w_branch, w_out, ffn2_norm, ffn2_w_in, ffn2_w_out):
    bsz, seq, _ = x.shape
    depth = w_in.shape[0]
    row = lambda a: a[:, None, :]
    rep = lambda a, n: jnp.repeat(a, n, axis=-1)

    f1n, f2n, mxn = row(ffn1_norm), row(ffn2_norm), row(mix_norm)
    w_a = w_in[:, :, OFF_UV:OFF_Z].astype(BF)
    w_b = jnp.concatenate([w_in[:, :, OFF_Z:OFF_DT], rep(w_in[:, :, OFF_DT:OFF_Q], SSM_HEAD_DIM)], axis=-1).astype(BF)
    w_c = _nsa_proj_weights(w_in)
    w_d = w_in[:, :, OFF_AB:OFF_MERGE].astype(BF)
    w_g = w_in[:, :, OFF_MERGE:].astype(BF)
    sgu_bias = rep(jnp.swapaxes(sgu_b, 1, 2), WIDTH // SGU_GROUPS)
    dt_bias = row(rep(ssm_dt_bias, SSM_HEAD_DIM))
    a_neg = row(rep(-jnp.exp(ssm_a_log), SSM_HEAD_DIM))
    d_skip = row(rep(ssm_d, SSM_HEAD_DIM))
    wb, wo = w_branch, w_out
    nsa_p = _nsa_params(seq, nsa_q_norm, nsa_k_norm, nsa_pe_k, nsa_w1_k, nsa_w2_k, nsa_pe_v, nsa_w1_v, nsa_w2_v)
    sgu_p = (w_a, row(sgu_v_norm), sgu_w, sgu_bias)
    ssd_p = (w_b, ssm_conv_w, row(ssm_conv_b), dt_bias, a_neg, d_skip, row(ssm_norm))
    conv_p = (w_d, conv_dw_w, row(conv_dw_b), row(conv_norm))

    x2 = x.reshape(bsz * seq, D_MODEL)
    for l in range(depth):
        x2 = _ffn(x2, f1n, ffn1_w_in, ffn1_w_out, l)
        x3 = x2.reshape(bsz, seq, D_MODEL)
        y_a, y_b, y_c, y_d = _token_mixers(x3, mxn, sgu_p, ssd_p, conv_p, w_c, nsa_p, l)
        flat = lambda y: y.reshape(bsz * seq, WIDTH)
        x2 = _merge(x2, flat(y_a), flat(y_b), flat(y_c), flat(y_d), mxn, w_g, wb, wo, l)
        x2 = _ffn(x2, f2n, ffn2_w_in, ffn2_w_out, l)
    return x2.reshape(bsz, seq, D_MODEL)
```
